```python
import jax, jax.numpy as jnp
from jax import lax
import numpy as np

D_MODEL = 2048
BATCH = 2
SEQ = 4096
DEPTH = 1

H_RET = 8
DK_RET = 128
DV_RET = 128
RET_CHUNK = 128
H_ATT = 8
H_KV = 2
D_HEAD = 128
H_IDX = 16
D_IDX = 64
MAX_TOPK = 256
Q_BLOCK = 128
D_FF = 5632
ROPE_THETA = 10000.0
NORM_EPS = 1e-6

RET_W = H_RET * DV_RET
ATT_W = H_ATT * D_HEAD
MIX_W = RET_W + ATT_W
IN_SIZES = (H_RET * DK_RET, H_RET * DK_RET, RET_W, RET_W, ATT_W, H_KV * D_HEAD, H_KV * D_HEAD, H_IDX * D_IDX, D_IDX, H_IDX)
IN_COLS = sum(IN_SIZES)

kernel_name = 'hybrid_retention_dsa_macaron'


def rmsnorm(x, g):
    xf = x.astype(jnp.float32)
    y = xf * lax.rsqrt(jnp.mean(xf * xf, axis=-1, keepdims=True) + NORM_EPS)
    return (y * g.astype(jnp.float32)).astype(x.dtype)


def rope(x, pos):
    d = x.shape[-1]
    inv = ROPE_THETA ** (-jnp.arange(0, d, 2, dtype=jnp.float32) / d)
    ang = pos.astype(jnp.float32)[..., None] * inv
    cos = jnp.cos(ang)[:, :, None, :]
    sin = jnp.sin(ang)[:, :, None, :]
    xf = x.astype(jnp.float32)
    x1, x2 = xf[..., : d // 2], xf[..., d // 2:]
    return jnp.concatenate([x1 * cos - x2 * sin, x2 * cos + x1 * sin], axis=-1).astype(x.dtype)


def swiglu(h, w_gate, w_up, w_down):
    return (jax.nn.silu(h @ w_gate) * (h @ w_up)) @ w_down


def retention(q, k, v):
    B, T, H, dk = q.shape
    dv = v.shape[-1]
    C = RET_CHUNK
    N = T // C
    lg = jnp.log1p(-jnp.exp2(-5.0 - jnp.arange(H, dtype=jnp.float32)))
    q = q.astype(jnp.float32).reshape(B, N, C, H, dk)
    k = (k.astype(jnp.float32) * dk ** -0.5).reshape(B, N, C, H, dk)
    v = v.astype(jnp.float32).reshape(B, N, C, H, dv)
    j = jnp.arange(C, dtype=jnp.float32)
    diff = j[:, None] - j[None, :]
    decay = jnp.where(diff[None] >= 0, jnp.exp(jnp.maximum(diff, 0.0)[None] * lg[:, None, None]), 0.0)
    s = jnp.einsum('bnihd,bnjhd->bnhij', q, k) * decay
    intra = jnp.einsum('bnhij,bnjhe->bnihe', s, v)
    k_dec = k * jnp.exp((C - 1 - j)[:, None] * lg[None, :])[:, :, None]
    kv = jnp.einsum('bnjhd,bnjhe->nbhde', k_dec, v)
    g_chunk = jnp.exp(C * lg)[None, :, None, None]

    def step(state, kv_n):
        return state * g_chunk + kv_n, state

    _, states = lax.scan(step, jnp.zeros((B, H, dk, dv), jnp.float32), kv)
    q_dec = q * jnp.exp((j + 1)[:, None] * lg[None, :])[:, :, None]
    cross = jnp.einsum('bnihd,nbhde->bnihe', q_dec, states)
    return (intra + cross).reshape(B, T, H, dv)


def sparse_attention(q, k, v, q_idx, k_idx, w_idx):
    B, T = q.shape[0], q.shape[1]
    top_k = min(MAX_TOPK, T // 4)
    nb = T // Q_BLOCK
    key_pos = jnp.arange(T)
    gather = jax.vmap(lambda a, i: a[i])

    def blocks(a):
        return jnp.moveaxis(a.reshape((B, nb, Q_BLOCK) + a.shape[2:]), 1, 0)

    def attend(args):
        qb, qib, wb, t0 = args
        t = t0 + jnp.arange(Q_BLOCK)
        rel = jax.nn.relu(jnp.einsum('bqhd,bsd->bqhs', qib, k_idx))
        score = jnp.einsum('bqhs,bqh->bqs', rel, wb).astype(jnp.float32)
        causal = key_pos[None, :] <= t[:, None]
        score = jnp.where(causal[None], score, -jnp.inf)
        _, idx = lax.top_k(score, top_k)
        valid = idx <= t[None, :, None]
        kg = gather(k, idx)
        vg = gather(v, idx)
        logits = jnp.einsum('bqgnd,bqkgd->bqgnk', qb, kg).astype(jnp.float32) * D_HEAD ** -0.5
        logits = jnp.where(valid[:, :, None, None, :], logits, -jnp.inf)
        p = jax.nn.softmax(logits, axis=-1).astype(vg.dtype)
        return jnp.einsum('bqgnk,bqkgd->bqgnd', p, vg)

    out = lax.map(attend, (blocks(q), blocks(q_idx), blocks(w_idx), jnp.arange(nb) * Q_BLOCK))
    return jnp.moveaxis(out, 0, 1).reshape(B, T, H_ATT * D_HEAD)


def setup_inputs(seed: int = 0) -> dict:
    key = jax.random.key(seed)
    ks = jax.random.split(key, 16)

    def dense(k, shape, fan_in):
        return jax.random.normal(k, shape, jnp.float32) * fan_in ** -0.5

    def gain(k, shape):
        return 1.0 + 0.02 * jax.random.normal(k, shape, jnp.float32)

    return {
        'x': jax.random.normal(ks[0], (BATCH, SEQ, D_MODEL), jnp.float32),
        'positions': jnp.broadcast_to(jnp.arange(SEQ, dtype=jnp.int32), (BATCH, SEQ)),
        'ffn1_norm': gain(ks[1], (DEPTH, D_MODEL)),
        'ffn1_w_gate': dense(ks[2], (DEPTH, D_MODEL, D_FF), D_MODEL),
        'ffn1_w_up': dense(ks[3], (DEPTH, D_MODEL, D_FF), D_MODEL),
        'ffn1_w_down': dense(ks[4], (DEPTH, D_FF, D_MODEL), D_FF),
        'mix_norm': gain(ks[5], (DEPTH, D_MODEL)),
        'w_in': dense(ks[6], (DEPTH, D_MODEL, IN_COLS), D_MODEL),
        'ret_norm': gain(ks[7], (DEPTH, RET_W)),
        'w_out': dense(ks[8], (DEPTH, MIX_W, D_MODEL), MIX_W),
        'ffn2_norm': gain(ks[9], (DEPTH, D_MODEL)),
        'ffn2_w_gate': dense(ks[10], (DEPTH, D_MODEL, D_FF), D_MODEL),
        'ffn2_w_up': dense(ks[11], (DEPTH, D_MODEL, D_FF), D_MODEL),
        'ffn2_w_down': dense(ks[12], (DEPTH, D_FF, D_MODEL), D_FF),
        'final_norm': gain(ks[13], (D_MODEL,)),
    }


def reference(x, positions, ffn1_norm, ffn1_w_gate, ffn1_w_up, ffn1_w_down, mix_norm, w_in, ret_norm, w_out, ffn2_norm, ffn2_w_gate, ffn2_w_up, ffn2_w_down, final_norm):
    B, T, _ = x.shape
    splits = np.cumsum(IN_SIZES)[:-1].tolist()
    for l in range(DEPTH):
        x = x + 0.5 * swiglu(rmsnorm(x, ffn1_norm[l]), ffn1_w_gate[l], ffn1_w_up[l], ffn1_w_down[l])
        h = rmsnorm(x, mix_norm[l])
        proj = h @ w_in[l]
        rq, rk, rv, rg, aq, ak, av, iq, ik, iw = jnp.split(proj, splits, axis=-1)
        rq = rope(rq.reshape(B, T, H_RET, DK_RET), positions)
        rk = rope(rk.reshape(B, T, H_RET, DK_RET), positions)
        ro = retention(rq, rk, rv.reshape(B, T, H_RET, DV_RET))
        mu = jnp.mean(ro, axis=-1, keepdims=True)
        var = jnp.mean(jnp.square(ro - mu), axis=-1, keepdims=True)
        ro = (ro - mu) * lax.rsqrt(var + NORM_EPS) * ret_norm[l].astype(jnp.float32).reshape(H_RET, DV_RET)
        ro = ro.reshape(B, T, RET_W).astype(x.dtype) * jax.nn.silu(rg)
        aq = rope(aq.reshape(B, T, H_ATT, D_HEAD), positions).reshape(B, T, H_KV, H_ATT // H_KV, D_HEAD)
        ak = rope(ak.reshape(B, T, H_KV, D_HEAD), positions)
        av = av.reshape(B, T, H_KV, D_HEAD)
        iq = rope(iq.reshape(B, T, H_IDX, D_IDX), positions) * D_IDX ** -0.5
        ik = rope(ik.reshape(B, T, 1, D_IDX), positions)[:, :, 0]
        iw = iw * H_IDX ** -0.5
        ao = sparse_attention(aq, ak, av, iq, ik, iw)
        x = x + jnp.concatenate([ro, ao], axis=-1) @ w_out[l]
        x = x + 0.5 * swiglu(rmsnorm(x, ffn2_norm[l]), ffn2_w_gate[l], ffn2_w_up[l], ffn2_w_down[l])
    return rmsnorm(x, final_norm)
```

```python
import functools

import jax
import jax.numpy as jnp
from jax import lax
from jax.experimental import pallas as pl
from jax.experimental.pallas import tpu as pltpu

D_MODEL = 2048
H_RET = 8
DK_RET = 128
DV_RET = 128
RET_CHUNK = 128
H_ATT = 8
H_KV = 2
D_HEAD = 128
H_IDX = 16
D_IDX = 64
MAX_TOPK = 256
Q_BLOCK = 128
D_FF = 5632
ROPE_THETA = 10000.0
NORM_EPS = 1e-6

RET_W = H_RET * DV_RET
ATT_W = H_ATT * D_HEAD
GROUP = H_ATT // H_KV

LANES = 128
VMEM_LIMIT = 56 * 1024 * 1024

ROPE_COLS = 2 * H_RET * DK_RET + ATT_W + H_KV * D_HEAD
PLAIN_COLS = 2 * RET_W + H_KV * D_HEAD
IDX_Q_COLS = H_IDX * D_IDX
IDX_COLS = IDX_Q_COLS + 2 * LANES

INT_MIN = -2 ** 31
NEG_BIG = -1e30

F32 = jnp.float32
BF16 = jnp.bfloat16


def _dot(a, b):
    return jnp.dot(a, b, preferred_element_type=F32)


def _dot_nt(a, b):
    return lax.dot_general(a, b, (((1,), (1,)), ((), ())), preferred_element_type=F32)


def _dot_tn(a, b):
    return lax.dot_general(a, b, (((0,), (0,)), ((), ())), preferred_element_type=F32)


def _rmsnorm(xf, g):
    ms = jnp.mean(xf * xf, axis=-1, keepdims=True)
    return xf * lax.rsqrt(ms + NORM_EPS) * g


def _params(sem):
    return pltpu.CompilerParams(dimension_semantics=sem, vmem_limit_bytes=VMEM_LIMIT)


def _rope_table_kernel(pos_ref, inv_a_ref, inv_b_ref, sgn_a_ref, sgn_b_ref,
                       cos_a_ref, sin_a_ref, cos_b_ref, sin_b_ref):
    p = pos_ref[...].astype(F32)
    ang_a = p * inv_a_ref[...]
    ang_b = p * inv_b_ref[...]
    cos_a_ref[...] = jnp.cos(ang_a)
    sin_a_ref[...] = jnp.sin(ang_a) * sgn_a_ref[...]
    cos_b_ref[...] = jnp.cos(ang_b)
    sin_b_ref[...] = jnp.sin(ang_b) * sgn_b_ref[...]


def _rope_tables(positions):
    n_tok = positions.size
    tm = 1024
    lane = jnp.arange(LANES)

    def inv_freq(d):
        inv = ROPE_THETA ** (-jnp.arange(0, d, 2, dtype=F32) / d)
        return inv[lane % (d // 2)][None, :]

    def sign(d):
        return jnp.where(lane % d < d // 2, -1.0, 1.0).astype(F32)[None, :]

    row = pl.BlockSpec((1, LANES), lambda i: (0, 0))
    tab = pl.BlockSpec((tm, LANES), lambda i: (i, 0))
    out = jax.ShapeDtypeStruct((n_tok, LANES), F32)
    return pl.pallas_call(
        _rope_table_kernel,
        grid=(n_tok // tm,),
        in_specs=[pl.BlockSpec((tm, 1), lambda i: (i, 0)), row, row, row, row],
        out_specs=[tab, tab, tab, tab],
        out_shape=[out, out, out, out],
        compiler_params=_params(("parallel",)),
        name="rope_tables",
    )(positions.reshape(n_tok, 1), inv_freq(D_HEAD), inv_freq(D_IDX), sign(D_HEAD), sign(D_IDX))


def _ffn_kernel(x_ref, g_ref, wg_ref, wu_ref, wd_ref, g2_ref, *refs, emit_residual):
    if emit_residual:
        res_ref, normed_ref, xn_ref = refs
        acc_ref = res_ref
    else:
        normed_ref, xn_ref = refs
        acc_ref = normed_ref
    f = pl.program_id(1)

    @pl.when(f == 0)
    def _():
        xn_ref[...] = _rmsnorm(x_ref[...], g_ref[...]).astype(BF16)
        acc_ref[...] = jnp.zeros_like(acc_ref)

    xn = xn_ref[...]
    a = _dot(xn, wg_ref[...])
    b = _dot(xn, wu_ref[...])
    hidden = (a * jax.nn.sigmoid(a) * b).astype(BF16)
    acc_ref[...] += _dot(hidden, wd_ref[...])

    @pl.when(f == pl.num_programs(1) - 1)
    def _():
        y = x_ref[...] + 0.5 * acc_ref[...]
        if emit_residual:
            res_ref[...] = y
        normed_ref[...] = _rmsnorm(y, g2_ref[...]).astype(normed_ref.dtype)


def _ffn(x2d, g, wg, wu, wd, g2, *, emit_residual, tm=512, tf=512):
    n_tok = x2d.shape[0]
    tile = pl.BlockSpec((tm, D_MODEL), lambda i, f: (i, 0))
    row = pl.BlockSpec((1, D_MODEL), lambda i, f: (0, 0))
    if emit_residual:
        out_specs = [tile, tile]
        out_shape = [jax.ShapeDtypeStruct((n_tok, D_MODEL), F32),
                     jax.ShapeDtypeStruct((n_tok, D_MODEL), BF16)]
    else:
        out_specs = tile
        out_shape = jax.ShapeDtypeStruct((n_tok, D_MODEL), F32)
    return pl.pallas_call(
        functools.partial(_ffn_kernel, emit_residual=emit_residual),
        grid=(n_tok // tm, D_FF // tf),
        in_specs=[tile, row,
                  pl.BlockSpec((D_MODEL, tf), lambda i, f: (0, f)),
                  pl.BlockSpec((D_MODEL, tf), lambda i, f: (0, f)),
                  pl.BlockSpec((tf, D_MODEL), lambda i, f: (f, 0)),
                  row],
        out_specs=out_specs,
        out_shape=out_shape,
        scratch_shapes=[pltpu.VMEM((tm, D_MODEL), BF16)],
        compiler_params=_params(("parallel", "arbitrary")),
        name="ffn_residual" if emit_residual else "ffn_final",
    )(x2d, g, wg, wu, wd, g2)


def _rope128(y, cos, sin):
    return y * cos + pltpu.roll(y, D_HEAD // 2, axis=1) * sin


def _rope64(y, cos, sin):
    lane = lax.broadcasted_iota(jnp.int32, y.shape, 1)
    first_half = (lane & (D_IDX - 1)) < D_IDX // 2
    partner = jnp.where(first_half,
                        pltpu.roll(y, LANES - D_IDX // 2, axis=1),
                        pltpu.roll(y, D_IDX // 2, axis=1))
    return y * cos + partner * sin


def _proj_kernel(h_ref, w_ref, cos_ref, sin_ref, o_ref, *, mode):
    y = _dot(h_ref[...], w_ref[...])
    n_chunks = y.shape[1] // LANES
    if mode == "plain":
        o_ref[...] = y.astype(o_ref.dtype)
        return
    cos = cos_ref[...]
    sin = sin_ref[...]
    for c in range(n_chunks):
        yc = y[:, c * LANES:(c + 1) * LANES]
        if mode == "rope128":
            yc = _rope128(yc, cos, sin)
        elif c < IDX_Q_COLS // LANES:
            yc = _rope64(yc, cos, sin) * (D_IDX ** -0.5)
        elif c == IDX_Q_COLS // LANES:
            yc = _rope64(yc, cos, sin)
        else:
            yc = yc * (H_IDX ** -0.5)
        o_ref[:, c * LANES:(c + 1) * LANES] = yc.astype(o_ref.dtype)


def _proj(h, w, cos, sin, *, mode, tm, tn, out_dtype):
    n_tok, n_cols = h.shape[0], w.shape[1]
    tab = pl.BlockSpec((tm, LANES), lambda i, j: (i, 0))
    return pl.pallas_call(
        functools.partial(_proj_kernel, mode=mode),
        grid=(n_tok // tm, n_cols // tn),
        in_specs=[pl.BlockSpec((tm, D_MODEL), lambda i, j: (i, 0)),
                  pl.BlockSpec((D_MODEL, tn), lambda i, j: (0, j)),
                  tab, tab],
        out_specs=pl.BlockSpec((tm, tn), lambda i, j: (i, j)),
        out_shape=jax.ShapeDtypeStruct((n_tok, n_cols), out_dtype),
        compiler_params=_params(("parallel", "parallel")),
        name="proj_" + mode,
    )(h, w, cos, sin)


def _retention_kernel(lg_ref, q_ref, k_ref, v_ref, g_ref, rn_ref, o_ref):
    C = RET_CHUNK
    n_chunks = q_ref.shape[0] // C
    lg = lg_ref[...]
    row = lax.broadcasted_iota(jnp.int32, (C, C), 0).astype(F32)
    col = lax.broadcasted_iota(jnp.int32, (C, C), 1).astype(F32)
    diff = row - col
    scale = DK_RET ** -0.5
    decay = jnp.where(diff >= 0, jnp.exp(jnp.maximum(diff, 0.0) * lg), 0.0) * scale
    k_dec = jnp.exp((C - 1 - row) * lg) * scale
    q_dec = jnp.exp((row + 1) * lg)
    g_chunk = jnp.exp(C * lg)
    rn = rn_ref[...]

    def body(n, state):
        sl = pl.ds(pl.multiple_of(n * C, C), C)
        qc = q_ref[sl, :]
        kc = k_ref[sl, :]
        vc = v_ref[sl, :]
        s = _dot_nt(qc, kc) * decay
        intra = _dot(s.astype(BF16), vc)
        cross = _dot(qc, state.astype(BF16)) * q_dec
        kv = _dot_tn((kc.astype(F32) * k_dec).astype(BF16), vc)
        o = intra + cross
        mu = jnp.mean(o, axis=-1, keepdims=True)
        d = o - mu
        var = jnp.mean(d * d, axis=-1, keepdims=True)
        y = d * lax.rsqrt(var + NORM_EPS) * rn
        gate = g_ref[sl, :].astype(F32)
        o_ref[sl, :] = (y * (gate * jax.nn.sigmoid(gate))).astype(o_ref.dtype)
        return state * g_chunk + kv

    lax.fori_loop(0, n_chunks, body, jnp.zeros((DK_RET, DV_RET), F32))


def _retention(rope_out, plain_out, ret_norm, batch, seq):
    n_tok = batch * seq
    lg = jnp.log1p(-jnp.exp2(-5.0 - jnp.arange(H_RET, dtype=F32)))
    lg = jnp.broadcast_to(lg[:, None, None], (H_RET, 1, LANES))
    head = lambda off: pl.BlockSpec((seq, LANES), lambda b, h: (b, off + h))
    return pl.pallas_call(
        _retention_kernel,
        grid=(batch, H_RET),
        in_specs=[pl.BlockSpec((None, 1, LANES), lambda b, h: (h, 0, 0)),
                  head(0), head(H_RET),
                  head(0), head(H_RET),
                  pl.BlockSpec((1, LANES), lambda b, h: (0, h))],
        out_specs=head(0),
        out_shape=jax.ShapeDtypeStruct((n_tok, RET_W), BF16),
        compiler_params=_params(("parallel", "parallel")),
        name="retention",
    )(lg, rope_out, rope_out, plain_out, plain_out, ret_norm)


KEY_TILE = 512


def _sortable_key(score):
    bits = pltpu.bitcast(score, jnp.int32)
    return bits ^ ((bits >> 31) & jnp.int32(0x7FFFFFFF))


def _attn_kernel(iq_ref, ik_ref, iw_ref, aq_ref, ak_ref, av_ref, o_ref,
                 key_ref, wb_ref, *, top_k):
    qb = pl.program_id(1)
    t0 = qb * Q_BLOCK
    n_tiles = (t0 + Q_BLOCK + KEY_TILE - 1) // KEY_TILE
    q_pos = t0 + lax.broadcasted_iota(jnp.int32, (Q_BLOCK, KEY_TILE), 0)
    col_iota = lax.broadcasted_iota(jnp.int32, (Q_BLOCK, KEY_TILE), 1)

    iw = iw_ref[...].astype(F32)
    for h in range(H_IDX):
        wb_ref[h] = jnp.broadcast_to(iw[:, h:h + 1], (Q_BLOCK, LANES))

    lane = lax.broadcasted_iota(jnp.int32, (Q_BLOCK, LANES), 1)
    low = lane < D_IDX
    q_heads = []
    for c in range(IDX_Q_COLS // LANES):
        qc = iq_ref[:, c * LANES:(c + 1) * LANES]
        zero = jnp.zeros_like(qc)
        q_heads.append(jnp.where(low, qc, zero))
        q_heads.append(jnp.where(low, zero, qc))

    def score_tile(kt, carry):
        base = pl.multiple_of(kt * KEY_TILE, KEY_TILE)
        kk = ik_ref[pl.ds(base, KEY_TILE), :]
        acc = jnp.zeros((Q_BLOCK, KEY_TILE), F32)
        for h in range(H_IDX):
            z = _dot_nt(q_heads[h], kk)
            w = jnp.concatenate([wb_ref[h]] * (KEY_TILE // LANES), axis=1)
            acc = acc + jnp.maximum(z, 0.0) * w
        causal = base + col_iota <= q_pos
        key_ref[:, pl.ds(base, KEY_TILE)] = jnp.where(causal, _sortable_key(acc), INT_MIN)
        return carry

    lax.fori_loop(0, n_tiles, score_tile, 0)

    def search_bit(it, tau_u):
        bit = lax.shift_left(jnp.int32(1), 31 - it)
        cand_u = tau_u | bit
        cand = cand_u ^ INT_MIN

        def count_tile(kt, cnt):
            base = pl.multiple_of(kt * KEY_TILE, KEY_TILE)
            ge = key_ref[:, pl.ds(base, KEY_TILE)] >= cand
            return cnt + jnp.sum(jnp.where(ge, 1.0, 0.0), axis=1, keepdims=True)

        cnt = lax.fori_loop(0, n_tiles, count_tile, jnp.zeros((Q_BLOCK, 1), F32))
        return jnp.where(cnt >= top_k, cand_u, tau_u)

    n_bits = jnp.where(t0 + Q_BLOCK > top_k, 32, 0)
    tau_u = lax.fori_loop(0, n_bits, search_bit, jnp.zeros((Q_BLOCK, 1), jnp.int32))
    tau = jnp.maximum(tau_u ^ INT_MIN, INT_MIN + 1)

    rows = GROUP * Q_BLOCK
    q_groups = []
    for g in range(H_KV):
        q_groups.append(jnp.concatenate(
            [aq_ref[:, (g * GROUP + j) * D_HEAD:(g * GROUP + j + 1) * D_HEAD] for j in range(GROUP)],
            axis=0))

    def attend_tile(kt, carry):
        base = pl.multiple_of(kt * KEY_TILE, KEY_TILE)
        sel = key_ref[:, pl.ds(base, KEY_TILE)] >= tau
        keep = jnp.where(sel, 1.0, 0.0)[None]
        bias = jnp.where(sel, 0.0, NEG_BIG)[None]
        new = []
        for g in range(H_KV):
            m_old, l_old, acc_old = carry[g]
            kt_g = ak_ref[pl.ds(base, KEY_TILE), g * D_HEAD:(g + 1) * D_HEAD]
            vt_g = av_ref[pl.ds(base, KEY_TILE), g * D_HEAD:(g + 1) * D_HEAD]
            logits = _dot_nt(q_groups[g], kt_g) * (D_HEAD ** -0.5)
            logits = logits.reshape(GROUP, Q_BLOCK, KEY_TILE) + bias
            m_new = jnp.maximum(m_old, jnp.max(logits, axis=2, keepdims=True))
            alpha = jnp.exp(m_old - m_new)
            p = jnp.exp(logits - m_new) * keep
            l_new = alpha * l_old + jnp.sum(p, axis=2, keepdims=True)
            pv = _dot(p.reshape(rows, KEY_TILE).astype(BF16), vt_g)
            acc_new = alpha * acc_old + pv.reshape(GROUP, Q_BLOCK, D_HEAD)
            new.append((m_new, l_new, acc_new))
        return tuple(new)

    init = tuple((jnp.full((GROUP, Q_BLOCK, 1), NEG_BIG, F32), jnp.zeros((GROUP, Q_BLOCK, 1), F32),
                  jnp.zeros((GROUP, Q_BLOCK, D_HEAD), F32)) for _ in range(H_KV))
    final = lax.fori_loop(0, n_tiles, attend_tile, init)
    for g in range(H_KV):
        _, l_fin, acc_fin = final[g]
        out = acc_fin / l_fin
        for j in range(GROUP):
            hcol = (g * GROUP + j) * D_HEAD
            o_ref[:, hcol:hcol + D_HEAD] = out[j].astype(o_ref.dtype)


def _sparse_attention(idx_out, rope_out, plain_out, batch, seq):
    n_tok = batch * seq
    nb = seq // Q_BLOCK
    top_k = min(MAX_TOPK, seq // 4)
    kv_w = H_KV * D_HEAD
    return pl.pallas_call(
        functools.partial(_attn_kernel, top_k=top_k),
        grid=(batch, nb),
        in_specs=[
            pl.BlockSpec((Q_BLOCK, IDX_Q_COLS), lambda b, q: (b * nb + q, 0)),
            pl.BlockSpec((seq, LANES), lambda b, q: (b, IDX_Q_COLS // LANES)),
            pl.BlockSpec((Q_BLOCK, LANES), lambda b, q: (b * nb + q, IDX_Q_COLS // LANES + 1)),
            pl.BlockSpec((Q_BLOCK, ATT_W), lambda b, q: (b * nb + q, 2 * RET_W // ATT_W)),
            pl.BlockSpec((seq, kv_w), lambda b, q: (b, (2 * RET_W + ATT_W) // kv_w)),
            pl.BlockSpec((seq, kv_w), lambda b, q: (b, 2 * RET_W // kv_w)),
        ],
        out_specs=pl.BlockSpec((Q_BLOCK, ATT_W), lambda b, q: (b * nb + q, 0)),
        out_shape=jax.ShapeDtypeStruct((n_tok, ATT_W), BF16),
        scratch_shapes=[pltpu.VMEM((Q_BLOCK, seq), jnp.int32),
                        pltpu.VMEM((H_IDX, Q_BLOCK, LANES), F32)],
        compiler_params=_params(("parallel", "arbitrary")),
        name="sparse_attention",
    )(idx_out, idx_out, idx_out, rope_out, rope_out, plain_out)


def _out_proj_kernel(x_ref, ro_ref, ao_ref, wr_ref, wa_ref, o_ref):
    o_ref[...] = x_ref[...] + _dot(ro_ref[...], wr_ref[...]) + _dot(ao_ref[...], wa_ref[...])


def _out_proj(x2d, ro, ao, w_ret, w_att, tm=512):
    n_tok = x2d.shape[0]
    tile = pl.BlockSpec((tm, D_MODEL), lambda i: (i, 0))
    half = pl.BlockSpec((tm, RET_W), lambda i: (i, 0))
    wspec = pl.BlockSpec((RET_W, D_MODEL), lambda i: (0, 0))
    return pl.pallas_call(
        _out_proj_kernel,
        grid=(n_tok // tm,),
        in_specs=[tile, half, half, wspec, wspec],
        out_specs=tile,
        out_shape=jax.ShapeDtypeStruct((n_tok, D_MODEL), F32),
        compiler_params=_params(("parallel",)),
        name="out_proj",
    )(x2d, ro, ao, w_ret, w_att)


def _layer(x2d, tables, batch, seq, ffn1_norm, ffn1_w_gate, ffn1_w_up, ffn1_w_down, mix_norm,
           w_in, ret_norm, w_out, ffn2_norm, ffn2_w_gate, ffn2_w_up, ffn2_w_down, final_norm, last):
    cos_a, sin_a, cos_b, sin_b = tables
    bf = lambda w: w.astype(BF16)
    row = lambda g: g.reshape(1, -1).astype(F32)

    rq, rk, rv, rg, aq, ak, av, iq, ik, iw = jnp.split(
        w_in, [1024, 2048, 3072, 4096, 5120, 5376, 5632, 6656, 6720], axis=1)
    w_rope = bf(jnp.concatenate([rq, rk, aq, ak], axis=1))
    w_plain = bf(jnp.concatenate([rv, rg, av], axis=1))
    pad = jnp.zeros((D_MODEL, LANES - H_IDX), w_in.dtype)
    w_idx = bf(jnp.concatenate([iq, ik, ik, iw, pad], axis=1))

    x1, h = _ffn(x2d, row(ffn1_norm), bf(ffn1_w_gate), bf(ffn1_w_up), bf(ffn1_w_down),
                 row(mix_norm), emit_residual=True)
    rope_out = _proj(h, w_rope, cos_a, sin_a, mode="rope128", tm=512, tn=ROPE_COLS // 2, out_dtype=BF16)
    plain_out = _proj(h, w_plain, cos_a, sin_a, mode="plain", tm=512, tn=PLAIN_COLS // 2, out_dtype=BF16)
    idx_out = _proj(h, w_idx, cos_b, sin_b, mode="idx", tm=512, tn=IDX_COLS, out_dtype=BF16)
    ro = _retention(rope_out, plain_out, row(ret_norm), batch, seq)
    ao = _sparse_attention(idx_out, rope_out, plain_out, batch, seq)
    x2 = _out_proj(x1, ro, ao, bf(w_out[:RET_W]), bf(w_out[RET_W:]))
    out = _ffn(x2, row(ffn2_norm), bf(ffn2_w_gate), bf(ffn2_w_up), bf(ffn2_w_down),
               row(final_norm), emit_residual=not last)
    return out if last else out[0]


def kernel(x, positions, ffn1_norm, ffn1_w_gate, ffn1_w_up, ffn1_w_down, mix_norm, w_in, ret_norm,
           w_out, ffn2_norm, ffn2_w_gate, ffn2_w_up, ffn2_w_down, final_norm):
    batch, seq, _ = x.shape
    depth = w_in.shape[0]
    tables = _rope_tables(positions)
    x2d = x.reshape(batch * seq, D_MODEL)
    for l in range(depth):
        last = l == depth - 1
        x2d = _layer(x2d, tables, batch, seq, ffn1_norm[l], ffn1_w_gate[l], ffn1_w_up[l], ffn1_w_down[l],
                     mix_norm[l], w_in[l], ret_norm[l], w_out[l], ffn2_norm[l], ffn2_w_gate[l],
                     ffn2_w_up[l], ffn2_w_down[l], final_norm, last)
    return x2d.reshape(batch, seq, D_MODEL)
```

```python
import functools

import jax
import jax.numpy as jnp
from jax import lax
from jax.experimental import pallas as pl
from jax.experimental.pallas import tpu as pltpu

D_MODEL = 2048
H_RET = 8
DK_RET = 128
DV_RET = 128
RET_CHUNK = 128
H_ATT = 8
H_KV = 2
D_HEAD = 128
H_IDX = 16
D_IDX = 64
MAX_TOPK = 256
Q_BLOCK = 128
D_FF = 5632
ROPE_THETA = 10000.0
NORM_EPS = 1e-6

RET_W = H_RET * DV_RET
ATT_W = H_ATT * D_HEAD
GROUP = H_ATT // H_KV

LANES = 128
SUBLANES = 8
VMEM_LIMIT = 56 * 1024 * 1024

ROPE_COLS = 2 * H_RET * DK_RET + ATT_W + H_KV * D_HEAD
PLAIN_COLS = 2 * RET_W + H_KV * D_HEAD
IDX_Q_COLS = H_IDX * D_IDX
IDX_COLS = IDX_Q_COLS + 2 * LANES
AQ_CHUNK0 = 2 * H_RET * DK_RET // LANES

INT_MIN = -2 ** 31
NEG_BIG = -1e30

F32 = jnp.float32
BF16 = jnp.bfloat16


def _dot(a, b):
    return jnp.dot(a, b, preferred_element_type=F32)


def _dot_nt(a, b):
    return lax.dot_general(a, b, (((1,), (1,)), ((), ())), preferred_element_type=F32)


def _dot_tn(a, b):
    return lax.dot_general(a, b, (((0,), (0,)), ((), ())), preferred_element_type=F32)


def _rmsnorm(xf, g):
    ms = jnp.mean(xf * xf, axis=-1, keepdims=True)
    return xf * lax.rsqrt(ms + NORM_EPS) * g


def _params(sem):
    return pltpu.CompilerParams(dimension_semantics=sem, vmem_limit_bytes=VMEM_LIMIT)


def _rope_table_kernel(pos_ref, inv_a_ref, inv_b_ref, sgn_a_ref, sgn_b_ref,
                       cos_a_ref, sin_a_ref, cos_b_ref, sin_b_ref):
    p = pos_ref[...].astype(F32)
    ang_a = p * inv_a_ref[...]
    ang_b = p * inv_b_ref[...]
    cos_a_ref[...] = jnp.cos(ang_a)
    sin_a_ref[...] = jnp.sin(ang_a) * sgn_a_ref[...]
    cos_b_ref[...] = jnp.cos(ang_b)
    sin_b_ref[...] = jnp.sin(ang_b) * sgn_b_ref[...]


def _rope_tables(positions):
    n_tok = positions.size
    tm = 1024
    lane = jnp.arange(LANES)

    def inv_freq(d):
        inv = ROPE_THETA ** (-jnp.arange(0, d, 2, dtype=F32) / d)
        return inv[lane % (d // 2)][None, :]

    def sign(d):
        return jnp.where(lane % d < d // 2, -1.0, 1.0).astype(F32)[None, :]

    row = pl.BlockSpec((1, LANES), lambda i: (0, 0))
    tab = pl.BlockSpec((tm, LANES), lambda i: (i, 0))
    out = jax.ShapeDtypeStruct((n_tok, LANES), F32)
    return pl.pallas_call(
        _rope_table_kernel,
        grid=(n_tok // tm,),
        in_specs=[pl.BlockSpec((tm, 1), lambda i: (i, 0)), row, row, row, row],
        out_specs=[tab, tab, tab, tab],
        out_shape=[out, out, out, out],
        compiler_params=_params(("parallel",)),
        name="rope_tables",
    )(positions.reshape(n_tok, 1), inv_freq(D_HEAD), inv_freq(D_IDX), sign(D_HEAD), sign(D_IDX))


def _ffn_kernel(x_ref, g_ref, wg_ref, wu_ref, wd_ref, g2_ref, *refs, emit_residual):
    if emit_residual:
        res_ref, normed_ref, xn_ref = refs
        acc_ref = res_ref
    else:
        normed_ref, xn_ref = refs
        acc_ref = normed_ref
    f = pl.program_id(1)

    @pl.when(f == 0)
    def _():
        xn_ref[...] = _rmsnorm(x_ref[...], g_ref[...]).astype(BF16)
        acc_ref[...] = jnp.zeros_like(acc_ref)

    xn = xn_ref[...]
    a = _dot(xn, wg_ref[...])
    b = _dot(xn, wu_ref[...])
    hidden = (a * jax.nn.sigmoid(a) * b).astype(BF16)
    acc_ref[...] += _dot(hidden, wd_ref[...])

    @pl.when(f == pl.num_programs(1) - 1)
    def _():
        y = x_ref[...] + 0.5 * acc_ref[...]
        if emit_residual:
            res_ref[...] = y
        normed_ref[...] = _rmsnorm(y, g2_ref[...]).astype(normed_ref.dtype)


def _ffn(x2d, g, wg, wu, wd, g2, *, emit_residual, tm=512, tf=512):
    n_tok = x2d.shape[0]
    tile = pl.BlockSpec((tm, D_MODEL), lambda i, f: (i, 0))
    row = pl.BlockSpec((1, D_MODEL), lambda i, f: (0, 0))
    if emit_residual:
        out_specs = [tile, tile]
        out_shape = [jax.ShapeDtypeStruct((n_tok, D_MODEL), F32),
                     jax.ShapeDtypeStruct((n_tok, D_MODEL), BF16)]
    else:
        out_specs = tile
        out_shape = jax.ShapeDtypeStruct((n_tok, D_MODEL), F32)
    return pl.pallas_call(
        functools.partial(_ffn_kernel, emit_residual=emit_residual),
        grid=(n_tok // tm, D_FF // tf),
        in_specs=[tile, row,
                  pl.BlockSpec((D_MODEL, tf), lambda i, f: (0, f)),
                  pl.BlockSpec((D_MODEL, tf), lambda i, f: (0, f)),
                  pl.BlockSpec((tf, D_MODEL), lambda i, f: (f, 0)),
                  row],
        out_specs=out_specs,
        out_shape=out_shape,
        scratch_shapes=[pltpu.VMEM((tm, D_MODEL), BF16)],
        compiler_params=_params(("parallel", "arbitrary")),
        name="ffn_residual" if emit_residual else "ffn_final",
    )(x2d, g, wg, wu, wd, g2)


def _rope128(y, cos, sin):
    return y * cos + pltpu.roll(y, D_HEAD // 2, axis=1) * sin


def _rope64(y, cos, sin):
    lane = lax.broadcasted_iota(jnp.int32, y.shape, 1)
    first_half = (lane & (D_IDX - 1)) < D_IDX // 2
    partner = jnp.where(first_half,
                        pltpu.roll(y, LANES - D_IDX // 2, axis=1),
                        pltpu.roll(y, D_IDX // 2, axis=1))
    return y * cos + partner * sin


def _proj_kernel(h_ref, w_ref, cos_ref, sin_ref, o_ref, *, mode):
    y = _dot(h_ref[...], w_ref[...])
    n_chunks = y.shape[1] // LANES
    if mode == "plain":
        o_ref[...] = y.astype(o_ref.dtype)
        return
    cos = cos_ref[...]
    sin = sin_ref[...]
    chunk0 = pl.program_id(1) * n_chunks
    for c in range(n_chunks):
        yc = y[:, c * LANES:(c + 1) * LANES]
        if mode == "rope128":
            is_aq = (chunk0 + c >= AQ_CHUNK0) & (chunk0 + c < AQ_CHUNK0 + H_ATT)
            yc = _rope128(yc, cos, sin) * jnp.where(is_aq, D_HEAD ** -0.5, 1.0)
        elif c < IDX_Q_COLS // LANES:
            yc = _rope64(yc, cos, sin) * (D_IDX ** -0.5)
        elif c == IDX_Q_COLS // LANES:
            yc = _rope64(yc, cos, sin)
        else:
            yc = yc * (H_IDX ** -0.5)
        o_ref[:, c * LANES:(c + 1) * LANES] = yc.astype(o_ref.dtype)


def _proj(h, w, cos, sin, *, mode, tm, tn, out_dtype):
    n_tok, n_cols = h.shape[0], w.shape[1]
    tab = pl.BlockSpec((tm, LANES), lambda i, j: (i, 0))
    return pl.pallas_call(
        functools.partial(_proj_kernel, mode=mode),
        grid=(n_tok // tm, n_cols // tn),
        in_specs=[pl.BlockSpec((tm, D_MODEL), lambda i, j: (i, 0)),
                  pl.BlockSpec((D_MODEL, tn), lambda i, j: (0, j)),
                  tab, tab],
        out_specs=pl.BlockSpec((tm, tn), lambda i, j: (i, j)),
        out_shape=jax.ShapeDtypeStruct((n_tok, n_cols), out_dtype),
        compiler_params=_params(("parallel", "parallel")),
        name="proj_" + mode,
    )(h, w, cos, sin)


def _retention_kernel(lg_ref, q_ref, k_ref, v_ref, g_ref, rn_ref, o_ref):
    C = RET_CHUNK
    n_chunks = q_ref.shape[0] // C
    lg = lg_ref[...]
    row = lax.broadcasted_iota(jnp.int32, (C, C), 0).astype(F32)
    col = lax.broadcasted_iota(jnp.int32, (C, C), 1).astype(F32)
    diff = row - col
    scale = DK_RET ** -0.5
    decay = jnp.where(diff >= 0, jnp.exp(jnp.maximum(diff, 0.0) * lg), 0.0) * scale
    k_dec = jnp.exp((C - 1 - row) * lg) * scale
    q_dec = jnp.exp((row + 1) * lg)
    g_chunk = jnp.exp(C * lg)
    rn = rn_ref[...]

    def body(n, state):
        sl = pl.ds(pl.multiple_of(n * C, C), C)
        qc = q_ref[sl, :]
        kc = k_ref[sl, :]
        vc = v_ref[sl, :]
        s = _dot_nt(qc, kc) * decay
        intra = _dot(s.astype(BF16), vc)
        cross = _dot(qc, state.astype(BF16)) * q_dec
        kv = _dot_tn((kc.astype(F32) * k_dec).astype(BF16), vc)
        o = intra + cross
        mu = jnp.mean(o, axis=-1, keepdims=True)
        d = o - mu
        var = jnp.mean(d * d, axis=-1, keepdims=True)
        y = d * lax.rsqrt(var + NORM_EPS) * rn
        gate = g_ref[sl, :].astype(F32)
        o_ref[sl, :] = (y * (gate * jax.nn.sigmoid(gate))).astype(o_ref.dtype)
        return state * g_chunk + kv

    lax.fori_loop(0, n_chunks, body, jnp.zeros((DK_RET, DV_RET), F32))


def _retention(rope_out, plain_out, ret_norm, batch, seq):
    n_tok = batch * seq
    lg = jnp.log1p(-jnp.exp2(-5.0 - jnp.arange(H_RET, dtype=F32)))
    lg = jnp.broadcast_to(lg[:, None, None], (H_RET, 1, LANES))
    head = lambda off: pl.BlockSpec((seq, LANES), lambda b, h: (b, off + h))
    return pl.pallas_call(
        _retention_kernel,
        grid=(batch, H_RET),
        in_specs=[pl.BlockSpec((None, 1, LANES), lambda b, h: (h, 0, 0)),
                  head(0), head(H_RET),
                  head(0), head(H_RET),
                  pl.BlockSpec((1, LANES), lambda b, h: (0, h))],
        out_specs=head(0),
        out_shape=jax.ShapeDtypeStruct((n_tok, RET_W), BF16),
        compiler_params=_params(("parallel", "parallel")),
        name="retention",
    )(lg, rope_out, rope_out, plain_out, plain_out, ret_norm)


KEY_TILE = 512
SCORE_ROWS = 128
FLT_MAX_KEY = 0x00800000


def _order_key_to_float(u):
    s = u ^ INT_MIN
    return pltpu.bitcast(s ^ ((s >> 31) & jnp.int32(0x7FFFFFFF)), F32)


def _attn_kernel(iq_ref, ik_ref, iw_ref, aq_ref, ak_ref, av_ref, o_ref, sc_ref, *, top_k):
    qb = pl.program_id(1)
    t0 = qb * Q_BLOCK
    n_tiles = (t0 + Q_BLOCK + KEY_TILE - 1) // KEY_TILE

    w_t = iw_ref[...].astype(F32).T
    w_rows = [w_t[h:h + 1, :] for h in range(H_IDX)]

    lane = lax.broadcasted_iota(jnp.int32, (Q_BLOCK, LANES), 1)
    low = lane < D_IDX
    q_pairs = []
    for c in range(IDX_Q_COLS // LANES):
        qc = iq_ref[:, c * LANES:(c + 1) * LANES]
        zero = jnp.zeros_like(qc)
        q_pairs.append(jnp.concatenate([jnp.where(low, qc, zero), jnp.where(low, zero, qc)], axis=0))

    q_pos = t0 + lax.broadcasted_iota(jnp.int32, (SCORE_ROWS, Q_BLOCK), 1)
    key_off = lax.broadcasted_iota(jnp.int32, (SCORE_ROWS, Q_BLOCK), 0)

    def score_tile(kt, carry):
        for s in range(KEY_TILE // SCORE_ROWS):
            base = pl.multiple_of(kt * KEY_TILE + s * SCORE_ROWS, SCORE_ROWS)
            kk = ik_ref[pl.ds(base, SCORE_ROWS), :]
            acc = jnp.zeros((SCORE_ROWS, Q_BLOCK), F32)
            for c, qp in enumerate(q_pairs):
                z = _dot_nt(kk, qp)
                acc = acc + jnp.maximum(z[:, :Q_BLOCK], 0.0) * w_rows[2 * c]
                acc = acc + jnp.maximum(z[:, Q_BLOCK:], 0.0) * w_rows[2 * c + 1]
            causal = base + key_off <= q_pos
            sc_ref[pl.ds(base, SCORE_ROWS), :] = jnp.where(causal, acc, -jnp.inf)
        return carry

    lax.fori_loop(0, n_tiles, score_tile, 0)

    def search_bit(it, tau_u):
        cand_u = tau_u | lax.shift_left(jnp.int32(1), 31 - it)
        cand = _order_key_to_float(cand_u)

        def count_tile(kt, cnt):
            base = pl.multiple_of(kt * KEY_TILE, KEY_TILE)
            ge = sc_ref[pl.ds(base, KEY_TILE), :] >= cand
            part = jnp.where(ge, 1.0, 0.0).reshape(KEY_TILE // SUBLANES, SUBLANES, Q_BLOCK)
            while part.shape[0] > 1:
                half = part.shape[0] // 2
                part = part[:half] + part[half:]
            return cnt + part[0]

        cnt = lax.fori_loop(0, n_tiles, count_tile, jnp.zeros((SUBLANES, Q_BLOCK), F32))
        total = jnp.sum(cnt, axis=0, keepdims=True)
        return jnp.where(total >= top_k, cand_u, tau_u)

    searched = t0 + Q_BLOCK > top_k
    n_bits = jnp.where(searched, 32, 0)
    tau_u = lax.fori_loop(0, n_bits, search_bit, jnp.zeros((1, Q_BLOCK), jnp.int32))
    tau = _order_key_to_float(jnp.where(searched, tau_u, FLT_MAX_KEY))

    q_groups = []
    for g in range(H_KV):
        q_groups.append(jnp.concatenate(
            [aq_ref[:, (g * GROUP + j) * D_HEAD:(g * GROUP + j + 1) * D_HEAD] for j in range(GROUP)],
            axis=0))
    width = GROUP * Q_BLOCK

    def attend_tile(kt, carry):
        base = pl.multiple_of(kt * KEY_TILE, KEY_TILE)
        bias = jnp.where(sc_ref[pl.ds(base, KEY_TILE), :] >= tau, 0.0, NEG_BIG)
        new = []
        for g in range(H_KV):
            m_old, l_old, acc_old = carry[g]
            kt_g = ak_ref[pl.ds(base, KEY_TILE), g * D_HEAD:(g + 1) * D_HEAD]
            vt_g = av_ref[pl.ds(base, KEY_TILE), g * D_HEAD:(g + 1) * D_HEAD]
            logits = _dot_nt(kt_g, q_groups[g])
            logits = jnp.concatenate(
                [logits[:, j * Q_BLOCK:(j + 1) * Q_BLOCK] + bias for j in range(GROUP)], axis=1)
            m_new = jnp.maximum(m_old, jnp.max(logits, axis=0, keepdims=True))
            alpha = jnp.exp(m_old - m_new)
            p = jnp.exp(logits - m_new)
            l_new = alpha * l_old + jnp.sum(p, axis=0, keepdims=True)
            acc_new = alpha * acc_old + _dot_tn(vt_g, p.astype(BF16))
            new.append((m_new, l_new, acc_new))
        return tuple(new)

    init = tuple((jnp.full((1, width), NEG_BIG, F32), jnp.zeros((1, width), F32),
                  jnp.zeros((D_HEAD, width), F32)) for _ in range(H_KV))
    final = lax.fori_loop(0, n_tiles, attend_tile, init)
    for g in range(H_KV):
        _, l_fin, acc_fin = final[g]
        out_t = acc_fin / l_fin
        for j in range(GROUP):
            hcol = (g * GROUP + j) * D_HEAD
            o_ref[:, hcol:hcol + D_HEAD] = out_t[:, j * Q_BLOCK:(j + 1) * Q_BLOCK].T.astype(o_ref.dtype)


def _sparse_attention(idx_out, rope_out, plain_out, batch, seq):
    n_tok = batch * seq
    nb = seq // Q_BLOCK
    top_k = min(MAX_TOPK, seq // 4)
    kv_w = H_KV * D_HEAD
    return pl.pallas_call(
        functools.partial(_attn_kernel, top_k=top_k),
        grid=(batch, nb),
        in_specs=[
            pl.BlockSpec((Q_BLOCK, IDX_Q_COLS), lambda b, q: (b * nb + q, 0)),
            pl.BlockSpec((seq, LANES), lambda b, q: (b, IDX_Q_COLS // LANES)),
            pl.BlockSpec((Q_BLOCK, LANES), lambda b, q: (b * nb + q, IDX_Q_COLS // LANES + 1)),
            pl.BlockSpec((Q_BLOCK, ATT_W), lambda b, q: (b * nb + q, 2 * RET_W // ATT_W)),
            pl.BlockSpec((seq, kv_w), lambda b, q: (b, (2 * RET_W + ATT_W) // kv_w)),
            pl.BlockSpec((seq, kv_w), lambda b, q: (b, 2 * RET_W // kv_w)),
        ],
        out_specs=pl.BlockSpec((Q_BLOCK, ATT_W), lambda b, q: (b * nb + q, 0)),
        out_shape=jax.ShapeDtypeStruct((n_tok, ATT_W), BF16),
        scratch_shapes=[pltpu.VMEM((seq, Q_BLOCK), F32)],
        compiler_params=_params(("parallel", "arbitrary")),
        name="sparse_attention",
    )(idx_out, idx_out, idx_out, rope_out, rope_out, plain_out)


def _out_proj_kernel(x_ref, ro_ref, ao_ref, wr_ref, wa_ref, o_ref):
    o_ref[...] = x_ref[...] + _dot(ro_ref[...], wr_ref[...]) + _dot(ao_ref[...], wa_ref[...])


def _out_proj(x2d, ro, ao, w_ret, w_att, tm=512):
    n_tok = x2d.shape[0]
    tile = pl.BlockSpec((tm, D_MODEL), lambda i: (i, 0))
    half = pl.BlockSpec((tm, RET_W), lambda i: (i, 0))
    wspec = pl.BlockSpec((RET_W, D_MODEL), lambda i: (0, 0))
    return pl.pallas_call(
        _out_proj_kernel,
        grid=(n_tok // tm,),
        in_specs=[tile, half, half, wspec, wspec],
        out_specs=tile,
        out_shape=jax.ShapeDtypeStruct((n_tok, D_MODEL), F32),
        compiler_params=_params(("parallel",)),
        name="out_proj",
    )(x2d, ro, ao, w_ret, w_att)


def _layer(x2d, tables, batch, seq, ffn1_norm, ffn1_w_gate, ffn1_w_up, ffn1_w_down, mix_norm,
           w_in, ret_norm, w_out, ffn2_norm, ffn2_w_gate, ffn2_w_up, ffn2_w_down, final_norm, last):
    cos_a, sin_a, cos_b, sin_b = tables
    bf = lambda w: w.astype(BF16)
    row = lambda g: g.reshape(1, -1).astype(F32)

    rq, rk, rv, rg, aq, ak, av, iq, ik, iw = jnp.split(
        w_in, [1024, 2048, 3072, 4096, 5120, 5376, 5632, 6656, 6720], axis=1)
    w_rope = bf(jnp.concatenate([rq, rk, aq, ak], axis=1))
    w_plain = bf(jnp.concatenate([rv, rg, av], axis=1))
    pad = jnp.zeros((D_MODEL, LANES - H_IDX), w_in.dtype)
    w_idx = bf(jnp.concatenate([iq, ik, ik, iw, pad], axis=1))

    x1, h = _ffn(x2d, row(ffn1_norm), bf(ffn1_w_gate), bf(ffn1_w_up), bf(ffn1_w_down),
                 row(mix_norm), emit_residual=True)
    rope_out = _proj(h, w_rope, cos_a, sin_a, mode="rope128", tm=512, tn=ROPE_COLS // 2, out_dtype=BF16)
    plain_out = _proj(h, w_plain, cos_a, sin_a, mode="plain", tm=512, tn=PLAIN_COLS // 2, out_dtype=BF16)
    idx_out = _proj(h, w_idx, cos_b, sin_b, mode="idx", tm=512, tn=IDX_COLS, out_dtype=BF16)
    ro = _retention(rope_out, plain_out, row(ret_norm), batch, seq)
    ao = _sparse_attention(idx_out, rope_out, plain_out, batch, seq)
    x2 = _out_proj(x1, ro, ao, bf(w_out[:RET_W]), bf(w_out[RET_W:]))
    out = _ffn(x2, row(ffn2_norm), bf(ffn2_w_gate), bf(ffn2_w_up), bf(ffn2_w_down),
               row(final_norm), emit_residual=not last)
    return out if last else out[0]


def kernel(x, positions, ffn1_norm, ffn1_w_gate, ffn1_w_up, ffn1_w_down, mix_norm, w_in, ret_norm,
           w_out, ffn2_norm, ffn2_w_gate, ffn2_w_up, ffn2_w_down, final_norm):
    batch, seq, _ = x.shape
    depth = w_in.shape[0]
    tables = _rope_tables(positions)
    x2d = x.reshape(batch * seq, D_MODEL)
    for l in range(depth):
        last = l == depth - 1
        x2d = _layer(x2d, tables, batch, seq, ffn1_norm[l], ffn1_w_gate[l], ffn1_w_up[l], ffn1_w_down[l],
                     mix_norm[l], w_in[l], ret_norm[l], w_out[l], ffn2_norm[l], ffn2_w_gate[l],
                     ffn2_w_up[l], ffn2_w_down[l], final_norm, last)
    return x2d.reshape(batch, seq, D_MODEL)
```

```python
import functools

import jax
import jax.numpy as jnp
from jax import lax
from jax.experimental import pallas as pl
from jax.experimental.pallas import tpu as pltpu

D_MODEL = 2048
H_RET = 8
DK_RET = 128
DV_RET = 128
RET_CHUNK = 128
H_ATT = 8
H_KV = 2
D_HEAD = 128
H_IDX = 16
D_IDX = 64
MAX_TOPK = 256
Q_BLOCK = 128
D_FF = 5632
ROPE_THETA = 10000.0
NORM_EPS = 1e-6

RET_W = H_RET * DV_RET
ATT_W = H_ATT * D_HEAD
GROUP = H_ATT // H_KV

LANES = 128
SUBLANES = 8
VMEM_LIMIT = 56 * 1024 * 1024

ROPE_COLS = 2 * H_RET * DK_RET + ATT_W + H_KV * D_HEAD
PLAIN_COLS = 2 * RET_W + H_KV * D_HEAD
IDX_Q_COLS = H_IDX * D_IDX
IDX_COLS = IDX_Q_COLS + 2 * LANES
AQ_CHUNK0 = 2 * H_RET * DK_RET // LANES

INT_MIN = -2 ** 31
NEG_BIG = -1e30

F32 = jnp.float32
BF16 = jnp.bfloat16


def _dot(a, b):
    return jnp.dot(a, b, preferred_element_type=F32)


def _dot_nt(a, b):
    return lax.dot_general(a, b, (((1,), (1,)), ((), ())), preferred_element_type=F32)


def _dot_tn(a, b):
    return lax.dot_general(a, b, (((0,), (0,)), ((), ())), preferred_element_type=F32)


def _rmsnorm(xf, g):
    ms = jnp.mean(xf * xf, axis=-1, keepdims=True)
    return xf * lax.rsqrt(ms + NORM_EPS) * g


def _params(sem):
    return pltpu.CompilerParams(dimension_semantics=sem, vmem_limit_bytes=VMEM_LIMIT)


def _rope_table_kernel(pos_ref, inv_a_ref, inv_b_ref, sgn_a_ref, sgn_b_ref,
                       cos_a_ref, sin_a_ref, cos_b_ref, sin_b_ref):
    p = pos_ref[...].astype(F32)
    ang_a = p * inv_a_ref[...]
    ang_b = p * inv_b_ref[...]
    cos_a_ref[...] = jnp.cos(ang_a)
    sin_a_ref[...] = jnp.sin(ang_a) * sgn_a_ref[...]
    cos_b_ref[...] = jnp.cos(ang_b)
    sin_b_ref[...] = jnp.sin(ang_b) * sgn_b_ref[...]


def _rope_tables(positions):
    n_tok = positions.size
    tm = 1024
    lane = jnp.arange(LANES)

    def inv_freq(d):
        inv = ROPE_THETA ** (-jnp.arange(0, d, 2, dtype=F32) / d)
        return inv[lane % (d // 2)][None, :]

    def sign(d):
        return jnp.where(lane % d < d // 2, -1.0, 1.0).astype(F32)[None, :]

    row = pl.BlockSpec((1, LANES), lambda i: (0, 0))
    tab = pl.BlockSpec((tm, LANES), lambda i: (i, 0))
    out = jax.ShapeDtypeStruct((n_tok, LANES), F32)
    return pl.pallas_call(
        _rope_table_kernel,
        grid=(n_tok // tm,),
        in_specs=[pl.BlockSpec((tm, 1), lambda i: (i, 0)), row, row, row, row],
        out_specs=[tab, tab, tab, tab],
        out_shape=[out, out, out, out],
        compiler_params=_params(("parallel",)),
        name="rope_tables",
    )(positions.reshape(n_tok, 1), inv_freq(D_HEAD), inv_freq(D_IDX), sign(D_HEAD), sign(D_IDX))


def _ffn_kernel(x_ref, g_ref, wg_ref, wu_ref, wd_ref, g2_ref, *refs, emit_residual):
    if emit_residual:
        res_ref, normed_ref, xn_ref = refs
        acc_ref = res_ref
    else:
        normed_ref, xn_ref = refs
        acc_ref = normed_ref
    f = pl.program_id(1)

    @pl.when(f == 0)
    def _():
        xn_ref[...] = _rmsnorm(x_ref[...], g_ref[...]).astype(BF16)
        acc_ref[...] = jnp.zeros_like(acc_ref)

    xn = xn_ref[...]
    a = _dot(xn, wg_ref[...].astype(BF16))
    b = _dot(xn, wu_ref[...].astype(BF16))
    hidden = (a * jax.nn.sigmoid(a) * b).astype(BF16)
    acc_ref[...] += _dot(hidden, wd_ref[...].astype(BF16))

    @pl.when(f == pl.num_programs(1) - 1)
    def _():
        y = x_ref[...] + 0.5 * acc_ref[...]
        if emit_residual:
            res_ref[...] = y
        normed_ref[...] = _rmsnorm(y, g2_ref[...]).astype(normed_ref.dtype)


def _ffn(x2d, g, wg, wu, wd, g2, *, emit_residual, tm=1024, tf=256):
    n_tok = x2d.shape[0]
    tile = pl.BlockSpec((tm, D_MODEL), lambda i, f: (i, 0), pipeline_mode=pl.Buffered(1))
    row = pl.BlockSpec((1, D_MODEL), lambda i, f: (0, 0))
    if emit_residual:
        out_specs = [tile, tile]
        out_shape = [jax.ShapeDtypeStruct((n_tok, D_MODEL), F32),
                     jax.ShapeDtypeStruct((n_tok, D_MODEL), BF16)]
    else:
        out_specs = tile
        out_shape = jax.ShapeDtypeStruct((n_tok, D_MODEL), F32)
    return pl.pallas_call(
        functools.partial(_ffn_kernel, emit_residual=emit_residual),
        grid=(n_tok // tm, D_FF // tf),
        in_specs=[tile, row,
                  pl.BlockSpec((D_MODEL, tf), lambda i, f: (0, f)),
                  pl.BlockSpec((D_MODEL, tf), lambda i, f: (0, f)),
                  pl.BlockSpec((tf, D_MODEL), lambda i, f: (f, 0)),
                  row],
        out_specs=out_specs,
        out_shape=out_shape,
        scratch_shapes=[pltpu.VMEM((tm, D_MODEL), BF16)],
        compiler_params=_params(("parallel", "arbitrary")),
        name="ffn_residual" if emit_residual else "ffn_final",
    )(x2d, g, wg, wu, wd, g2)


def _rope128(y, cos, sin):
    return y * cos + pltpu.roll(y, D_HEAD // 2, axis=1) * sin


def _rope64(y, cos, sin):
    lane = lax.broadcasted_iota(jnp.int32, y.shape, 1)
    first_half = (lane & (D_IDX - 1)) < D_IDX // 2
    partner = jnp.where(first_half,
                        pltpu.roll(y, LANES - D_IDX // 2, axis=1),
                        pltpu.roll(y, D_IDX // 2, axis=1))
    return y * cos + partner * sin


def _proj_kernel(h_ref, w_ref, cos_ref, sin_ref, o_ref, *, mode):
    y = _dot(h_ref[...], w_ref[...])
    n_chunks = y.shape[1] // LANES
    if mode == "plain":
        o_ref[...] = y.astype(o_ref.dtype)
        return
    cos = cos_ref[...]
    sin = sin_ref[...]
    chunk0 = pl.program_id(1) * n_chunks
    for c in range(n_chunks):
        yc = y[:, c * LANES:(c + 1) * LANES]
        if mode == "rope128":
            is_aq = (chunk0 + c >= AQ_CHUNK0) & (chunk0 + c < AQ_CHUNK0 + H_ATT)
            yc = _rope128(yc, cos, sin) * jnp.where(is_aq, D_HEAD ** -0.5, 1.0)
        elif c < IDX_Q_COLS // LANES:
            yc = _rope64(yc, cos, sin) * (D_IDX ** -0.5)
        elif c == IDX_Q_COLS // LANES:
            yc = _rope64(yc, cos, sin)
        else:
            yc = yc * (H_IDX ** -0.5)
        o_ref[:, c * LANES:(c + 1) * LANES] = yc.astype(o_ref.dtype)


def _proj(h, w, cos, sin, *, mode, tm, tn, out_dtype):
    n_tok, n_cols = h.shape[0], w.shape[1]
    tab = pl.BlockSpec((tm, LANES), lambda i, j: (i, 0))
    return pl.pallas_call(
        functools.partial(_proj_kernel, mode=mode),
        grid=(n_tok // tm, n_cols // tn),
        in_specs=[pl.BlockSpec((tm, D_MODEL), lambda i, j: (i, 0)),
                  pl.BlockSpec((D_MODEL, tn), lambda i, j: (0, j)),
                  tab, tab],
        out_specs=pl.BlockSpec((tm, tn), lambda i, j: (i, j)),
        out_shape=jax.ShapeDtypeStruct((n_tok, n_cols), out_dtype),
        compiler_params=_params(("parallel", "parallel")),
        name="proj_" + mode,
    )(h, w, cos, sin)


def _retention_kernel(lg_ref, q_ref, k_ref, v_ref, g_ref, rn_ref, o_ref):
    C = RET_CHUNK
    n_chunks = q_ref.shape[0] // C
    lg = lg_ref[...]
    row = lax.broadcasted_iota(jnp.int32, (C, C), 0).astype(F32)
    col = lax.broadcasted_iota(jnp.int32, (C, C), 1).astype(F32)
    diff = row - col
    scale = DK_RET ** -0.5
    decay = jnp.where(diff >= 0, jnp.exp(jnp.maximum(diff, 0.0) * lg), 0.0) * scale
    k_dec = jnp.exp((C - 1 - row) * lg) * scale
    q_dec = jnp.exp((row + 1) * lg)
    g_chunk = jnp.exp(C * lg)
    rn = rn_ref[...]

    def body(n, state):
        sl = pl.ds(pl.multiple_of(n * C, C), C)
        qc = q_ref[sl, :]
        kc = k_ref[sl, :]
        vc = v_ref[sl, :]
        s = _dot_nt(qc, kc) * decay
        intra = _dot(s.astype(BF16), vc)
        cross = _dot(qc, state.astype(BF16)) * q_dec
        kv = _dot_tn((kc.astype(F32) * k_dec).astype(BF16), vc)
        o = intra + cross
        mu = jnp.mean(o, axis=-1, keepdims=True)
        d = o - mu
        var = jnp.mean(d * d, axis=-1, keepdims=True)
        y = d * lax.rsqrt(var + NORM_EPS) * rn
        gate = g_ref[sl, :].astype(F32)
        o_ref[sl, :] = (y * (gate * jax.nn.sigmoid(gate))).astype(o_ref.dtype)
        return state * g_chunk + kv

    lax.fori_loop(0, n_chunks, body, jnp.zeros((DK_RET, DV_RET), F32))


def _retention(rope_out, plain_out, ret_norm, batch, seq):
    n_tok = batch * seq
    lg = jnp.log1p(-jnp.exp2(-5.0 - jnp.arange(H_RET, dtype=F32)))
    lg = jnp.broadcast_to(lg[:, None, None], (H_RET, 1, LANES))
    head = lambda off: pl.BlockSpec((seq, LANES), lambda b, h: (b, off + h))
    return pl.pallas_call(
        _retention_kernel,
        grid=(batch, H_RET),
        in_specs=[pl.BlockSpec((None, 1, LANES), lambda b, h: (h, 0, 0)),
                  head(0), head(H_RET),
                  head(0), head(H_RET),
                  pl.BlockSpec((1, LANES), lambda b, h: (0, h))],
        out_specs=head(0),
        out_shape=jax.ShapeDtypeStruct((n_tok, RET_W), BF16),
        compiler_params=_params(("parallel", "parallel")),
        name="retention",
    )(lg, rope_out, rope_out, plain_out, plain_out, ret_norm)


KEY_TILE = 512
SCORE_ROWS = 128
FLT_MAX_KEY = 0x00800000


def _order_key_to_float(u):
    s = u ^ INT_MIN
    return pltpu.bitcast(s ^ ((s >> 31) & jnp.int32(0x7FFFFFFF)), F32)


def _attn_kernel(iq_ref, ik_ref, iw_ref, aq_ref, ak_ref, av_ref, o_ref, sc_ref, *, top_k):
    qb = pl.program_id(1)
    t0 = qb * Q_BLOCK
    n_tiles = (t0 + Q_BLOCK + KEY_TILE - 1) // KEY_TILE

    w_t = iw_ref[...].astype(F32).T
    w_rows = [w_t[h:h + 1, :] for h in range(H_IDX)]

    lane = lax.broadcasted_iota(jnp.int32, (Q_BLOCK, LANES), 1)
    low = lane < D_IDX
    q_pairs = []
    for c in range(IDX_Q_COLS // LANES):
        qc = iq_ref[:, c * LANES:(c + 1) * LANES]
        zero = jnp.zeros_like(qc)
        q_pairs.append(jnp.concatenate([jnp.where(low, qc, zero), jnp.where(low, zero, qc)], axis=0))

    q_pos = t0 + lax.broadcasted_iota(jnp.int32, (SCORE_ROWS, Q_BLOCK), 1)
    key_off = lax.broadcasted_iota(jnp.int32, (SCORE_ROWS, Q_BLOCK), 0)

    def score_tile(kt, carry):
        for s in range(KEY_TILE // SCORE_ROWS):
            base = pl.multiple_of(kt * KEY_TILE + s * SCORE_ROWS, SCORE_ROWS)
            kk = ik_ref[pl.ds(base, SCORE_ROWS), :]
            acc = jnp.zeros((SCORE_ROWS, Q_BLOCK), F32)
            for c, qp in enumerate(q_pairs):
                z = _dot_nt(kk, qp)
                acc = acc + jnp.maximum(z[:, :Q_BLOCK], 0.0) * w_rows[2 * c]
                acc = acc + jnp.maximum(z[:, Q_BLOCK:], 0.0) * w_rows[2 * c + 1]
            causal = base + key_off <= q_pos
            sc_ref[pl.ds(base, SCORE_ROWS), :] = jnp.where(causal, acc, -jnp.inf)
        return carry

    lax.fori_loop(0, n_tiles, score_tile, 0)

    def search_bit(it, tau_u):
        cand_u = tau_u | lax.shift_left(jnp.int32(1), 31 - it)
        cand = _order_key_to_float(cand_u)

        def count_tile(kt, cnt):
            base = pl.multiple_of(kt * KEY_TILE, KEY_TILE)
            ge = sc_ref[pl.ds(base, KEY_TILE), :] >= cand
            part = jnp.where(ge, 1.0, 0.0).reshape(KEY_TILE // SUBLANES, SUBLANES, Q_BLOCK)
            while part.shape[0] > 1:
                half = part.shape[0] // 2
                part = part[:half] + part[half:]
            return cnt + part[0]

        cnt = lax.fori_loop(0, n_tiles, count_tile, jnp.zeros((SUBLANES, Q_BLOCK), F32))
        total = jnp.sum(cnt, axis=0, keepdims=True)
        return jnp.where(total >= top_k, cand_u, tau_u)

    searched = t0 + Q_BLOCK > top_k
    n_bits = jnp.where(searched, 32, 0)
    tau_u = lax.fori_loop(0, n_bits, search_bit, jnp.zeros((1, Q_BLOCK), jnp.int32))
    tau = _order_key_to_float(jnp.where(searched, tau_u, FLT_MAX_KEY))

    q_groups = []
    for g in range(H_KV):
        q_groups.append(jnp.concatenate(
            [aq_ref[:, (g * GROUP + j) * D_HEAD:(g * GROUP + j + 1) * D_HEAD] for j in range(GROUP)],
            axis=0))
    width = GROUP * Q_BLOCK

    def attend_tile(kt, carry):
        base = pl.multiple_of(kt * KEY_TILE, KEY_TILE)
        bias = jnp.where(sc_ref[pl.ds(base, KEY_TILE), :] >= tau, 0.0, NEG_BIG)
        new = []
        for g in range(H_KV):
            m_old, l_old, acc_old = carry[g]
            kt_g = ak_ref[pl.ds(base, KEY_TILE), g * D_HEAD:(g + 1) * D_HEAD]
            vt_g = av_ref[pl.ds(base, KEY_TILE), g * D_HEAD:(g + 1) * D_HEAD]
            logits = _dot_nt(kt_g, q_groups[g])
            logits = jnp.concatenate(
                [logits[:, j * Q_BLOCK:(j + 1) * Q_BLOCK] + bias for j in range(GROUP)], axis=1)
            m_new = jnp.maximum(m_old, jnp.max(logits, axis=0, keepdims=True))
            alpha = jnp.exp(m_old - m_new)
            p = jnp.exp(logits - m_new)
            l_new = alpha * l_old + jnp.sum(p, axis=0, keepdims=True)
            acc_new = alpha * acc_old + _dot_tn(vt_g, p.astype(BF16))
            new.append((m_new, l_new, acc_new))
        return tuple(new)

    init = tuple((jnp.full((1, width), NEG_BIG, F32), jnp.zeros((1, width), F32),
                  jnp.zeros((D_HEAD, width), F32)) for _ in range(H_KV))
    final = lax.fori_loop(0, n_tiles, attend_tile, init)
    for g in range(H_KV):
        _, l_fin, acc_fin = final[g]
        out_t = acc_fin / l_fin
        for j in range(GROUP):
            hcol = (g * GROUP + j) * D_HEAD
            o_ref[:, hcol:hcol + D_HEAD] = out_t[:, j * Q_BLOCK:(j + 1) * Q_BLOCK].T.astype(o_ref.dtype)


def _sparse_attention(idx_out, rope_out, plain_out, batch, seq):
    n_tok = batch * seq
    nb = seq // Q_BLOCK
    top_k = min(MAX_TOPK, seq // 4)
    kv_w = H_KV * D_HEAD
    return pl.pallas_call(
        functools.partial(_attn_kernel, top_k=top_k),
        grid=(batch, nb),
        in_specs=[
            pl.BlockSpec((Q_BLOCK, IDX_Q_COLS), lambda b, q: (b * nb + q, 0)),
            pl.BlockSpec((seq, LANES), lambda b, q: (b, IDX_Q_COLS // LANES)),
            pl.BlockSpec((Q_BLOCK, LANES), lambda b, q: (b * nb + q, IDX_Q_COLS // LANES + 1)),
            pl.BlockSpec((Q_BLOCK, ATT_W), lambda b, q: (b * nb + q, 2 * RET_W // ATT_W)),
            pl.BlockSpec((seq, kv_w), lambda b, q: (b, (2 * RET_W + ATT_W) // kv_w)),
            pl.BlockSpec((seq, kv_w), lambda b, q: (b, 2 * RET_W // kv_w)),
        ],
        out_specs=pl.BlockSpec((Q_BLOCK, ATT_W), lambda b, q: (b * nb + q, 0)),
        out_shape=jax.ShapeDtypeStruct((n_tok, ATT_W), BF16),
        scratch_shapes=[pltpu.VMEM((seq, Q_BLOCK), F32)],
        compiler_params=_params(("parallel", "arbitrary")),
        name="sparse_attention",
    )(idx_out, idx_out, idx_out, rope_out, rope_out, plain_out)


def _out_proj_kernel(x_ref, ro_ref, ao_ref, wr_ref, wa_ref, o_ref):
    o_ref[...] = x_ref[...] + _dot(ro_ref[...], wr_ref[...]) + _dot(ao_ref[...], wa_ref[...])


def _out_proj(x2d, ro, ao, w_ret, w_att, tm=512):
    n_tok = x2d.shape[0]
    tile = pl.BlockSpec((tm, D_MODEL), lambda i: (i, 0))
    half = pl.BlockSpec((tm, RET_W), lambda i: (i, 0))
    wspec = pl.BlockSpec((RET_W, D_MODEL), lambda i: (0, 0))
    return pl.pallas_call(
        _out_proj_kernel,
        grid=(n_tok // tm,),
        in_specs=[tile, half, half, wspec, wspec],
        out_specs=tile,
        out_shape=jax.ShapeDtypeStruct((n_tok, D_MODEL), F32),
        compiler_params=_params(("parallel",)),
        name="out_proj",
    )(x2d, ro, ao, w_ret, w_att)


def _layer(x2d, tables, batch, seq, ffn1_norm, ffn1_w_gate, ffn1_w_up, ffn1_w_down, mix_norm,
           w_in, ret_norm, w_out, ffn2_norm, ffn2_w_gate, ffn2_w_up, ffn2_w_down, final_norm, last):
    cos_a, sin_a, cos_b, sin_b = tables
    bf = lambda w: w.astype(BF16)
    row = lambda g: g.reshape(1, -1).astype(F32)

    rq, rk, rv, rg, aq, ak, av, iq, ik, iw = jnp.split(
        w_in, [1024, 2048, 3072, 4096, 5120, 5376, 5632, 6656, 6720], axis=1)
    w_rope = bf(jnp.concatenate([rq, rk, aq, ak], axis=1))
    w_plain = bf(jnp.concatenate([rv, rg, av], axis=1))
    pad = jnp.zeros((D_MODEL, LANES - H_IDX), w_in.dtype)
    w_idx = bf(jnp.concatenate([iq, ik, ik, iw, pad], axis=1))

    x1, h = _ffn(x2d, row(ffn1_norm), ffn1_w_gate, ffn1_w_up, ffn1_w_down,
                 row(mix_norm), emit_residual=True)
    rope_out = _proj(h, w_rope, cos_a, sin_a, mode="rope128", tm=512, tn=ROPE_COLS // 2, out_dtype=BF16)
    plain_out = _proj(h, w_plain, cos_a, sin_a, mode="plain", tm=512, tn=PLAIN_COLS // 2, out_dtype=BF16)
    idx_out = _proj(h, w_idx, cos_b, sin_b, mode="idx", tm=512, tn=IDX_COLS, out_dtype=BF16)
    ro = _retention(rope_out, plain_out, row(ret_norm), batch, seq)
    ao = _sparse_attention(idx_out, rope_out, plain_out, batch, seq)
    x2 = _out_proj(x1, ro, ao, bf(w_out[:RET_W]), bf(w_out[RET_W:]))
    out = _ffn(x2, row(ffn2_norm), ffn2_w_gate, ffn2_w_up, ffn2_w_down,
               row(final_norm), emit_residual=not last)
    return out if last else out[0]


def kernel(x, positions, ffn1_norm, ffn1_w_gate, ffn1_w_up, ffn1_w_down, mix_norm, w_in, ret_norm,
           w_out, ffn2_norm, ffn2_w_gate, ffn2_w_up, ffn2_w_down, final_norm):
    batch, seq, _ = x.shape
    depth = w_in.shape[0]
    tables = _rope_tables(positions)
    x2d = x.reshape(batch * seq, D_MODEL)
    for l in range(depth):
        last = l == depth - 1
        x2d = _layer(x2d, tables, batch, seq, ffn1_norm[l], ffn1_w_gate[l], ffn1_w_up[l], ffn1_w_down[l],
                     mix_norm[l], w_in[l], ret_norm[l], w_out[l], ffn2_norm[l], ffn2_w_gate[l],
                     ffn2_w_up[l], ffn2_w_down[l], final_norm, last)
    return x2d.reshape(batch, seq, D_MODEL)
```

```python
import functools

import jax
import jax.numpy as jnp
from jax import lax
from jax.experimental import pallas as pl
from jax.experimental.pallas import tpu as pltpu

D_MODEL = 2048
H_RET = 8
DK_RET = 128
DV_RET = 128
RET_CHUNK = 128
H_ATT = 8
H_KV = 2
D_HEAD = 128
H_IDX = 16
D_IDX = 64
MAX_TOPK = 256
Q_BLOCK = 256
D_FF = 5632
ROPE_THETA = 10000.0
NORM_EPS = 1e-6

RET_W = H_RET * DV_RET
ATT_W = H_ATT * D_HEAD
GROUP = H_ATT // H_KV

LANES = 128
SUBLANES = 8
VMEM_LIMIT = 56 * 1024 * 1024

ROPE_COLS = 2 * H_RET * DK_RET + ATT_W + H_KV * D_HEAD
PLAIN_COLS = 2 * RET_W + H_KV * D_HEAD
IDX_Q_COLS = H_IDX * D_IDX
IDX_COLS = IDX_Q_COLS + 2 * LANES
AQ_CHUNK0 = 2 * H_RET * DK_RET // LANES

INT_MIN = -2 ** 31
NEG_BIG = -1e30

F32 = jnp.float32
BF16 = jnp.bfloat16


def _dot(a, b):
    return jnp.dot(a, b, preferred_element_type=F32)


def _dot_nt(a, b):
    return lax.dot_general(a, b, (((1,), (1,)), ((), ())), preferred_element_type=F32)


def _dot_tn(a, b):
    return lax.dot_general(a, b, (((0,), (0,)), ((), ())), preferred_element_type=F32)


def _rmsnorm(xf, g):
    ms = jnp.mean(xf * xf, axis=-1, keepdims=True)
    return xf * lax.rsqrt(ms + NORM_EPS) * g


def _params(sem):
    return pltpu.CompilerParams(dimension_semantics=sem, vmem_limit_bytes=VMEM_LIMIT)


def _rope_table_kernel(pos_ref, inv_a_ref, inv_b_ref, sgn_a_ref, sgn_b_ref,
                       cos_a_ref, sin_a_ref, cos_b_ref, sin_b_ref):
    p = pos_ref[...].astype(F32)
    ang_a = p * inv_a_ref[...]
    ang_b = p * inv_b_ref[...]
    cos_a_ref[...] = jnp.cos(ang_a)
    sin_a_ref[...] = jnp.sin(ang_a) * sgn_a_ref[...]
    cos_b_ref[...] = jnp.cos(ang_b)
    sin_b_ref[...] = jnp.sin(ang_b) * sgn_b_ref[...]


def _rope_tables(positions):
    n_tok = positions.size
    tm = 1024
    lane = jnp.arange(LANES)

    def inv_freq(d):
        inv = ROPE_THETA ** (-jnp.arange(0, d, 2, dtype=F32) / d)
        return inv[lane % (d // 2)][None, :]

    def sign(d):
        return jnp.where(lane % d < d // 2, -1.0, 1.0).astype(F32)[None, :]

    row = pl.BlockSpec((1, LANES), lambda i: (0, 0))
    tab = pl.BlockSpec((tm, LANES), lambda i: (i, 0))
    out = jax.ShapeDtypeStruct((n_tok, LANES), F32)
    return pl.pallas_call(
        _rope_table_kernel,
        grid=(n_tok // tm,),
        in_specs=[pl.BlockSpec((tm, 1), lambda i: (i, 0)), row, row, row, row],
        out_specs=[tab, tab, tab, tab],
        out_shape=[out, out, out, out],
        compiler_params=_params(("parallel",)),
        name="rope_tables",
    )(positions.reshape(n_tok, 1), inv_freq(D_HEAD), inv_freq(D_IDX), sign(D_HEAD), sign(D_IDX))


FFN_ROW_CHUNK = 128


def _ffn_kernel(x_ref, g_ref, wg_ref, wu_ref, wd_ref, g2_ref, *refs, emit_residual):
    if emit_residual:
        res_ref, normed_ref, xn_ref = refs
        acc_ref = res_ref
    else:
        normed_ref, xn_ref = refs
        acc_ref = normed_ref
    f = pl.program_id(1)
    n_row_chunks = x_ref.shape[0] // FFN_ROW_CHUNK

    def rows_of(r):
        return pl.ds(pl.multiple_of(r * FFN_ROW_CHUNK, FFN_ROW_CHUNK), FFN_ROW_CHUNK)

    @pl.when(f == 0)
    def _():
        def prologue(r, carry):
            xf = x_ref[rows_of(r), :]
            xn_ref[rows_of(r), :] = _rmsnorm(xf, g_ref[...]).astype(BF16)
            acc_ref[rows_of(r), :] = 2.0 * xf
            return carry
        lax.fori_loop(0, n_row_chunks, prologue, 0)

    xn = xn_ref[...]
    a = _dot(xn, wg_ref[...].astype(BF16))
    b = _dot(xn, wu_ref[...].astype(BF16))
    hidden = (a * jax.nn.sigmoid(a) * b).astype(BF16)
    acc_ref[...] += _dot(hidden, wd_ref[...].astype(BF16))

    @pl.when(f == pl.num_programs(1) - 1)
    def _():
        def epilogue(r, carry):
            y = 0.5 * acc_ref[rows_of(r), :]
            if emit_residual:
                res_ref[rows_of(r), :] = y
            normed_ref[rows_of(r), :] = _rmsnorm(y, g2_ref[...]).astype(normed_ref.dtype)
            return carry
        lax.fori_loop(0, n_row_chunks, epilogue, 0)


def _ffn(x2d, g, wg, wu, wd, g2, *, emit_residual, tm=1024, tf=256):
    n_tok = x2d.shape[0]
    tile_map = lambda i, f: (i, 0)
    x_tile = pl.BlockSpec((tm, D_MODEL), tile_map, pipeline_mode=pl.Buffered(1))
    out_tile = pl.BlockSpec((tm, D_MODEL), tile_map)
    row = pl.BlockSpec((1, D_MODEL), lambda i, f: (0, 0))
    if emit_residual:
        out_specs = [out_tile, out_tile]
        out_shape = [jax.ShapeDtypeStruct((n_tok, D_MODEL), F32),
                     jax.ShapeDtypeStruct((n_tok, D_MODEL), BF16)]
    else:
        out_specs = out_tile
        out_shape = jax.ShapeDtypeStruct((n_tok, D_MODEL), F32)
    return pl.pallas_call(
        functools.partial(_ffn_kernel, emit_residual=emit_residual),
        grid=(n_tok // tm, D_FF // tf),
        in_specs=[x_tile, row,
                  pl.BlockSpec((D_MODEL, tf), lambda i, f: (0, f)),
                  pl.BlockSpec((D_MODEL, tf), lambda i, f: (0, f)),
                  pl.BlockSpec((tf, D_MODEL), lambda i, f: (f, 0)),
                  row],
        out_specs=out_specs,
        out_shape=out_shape,
        scratch_shapes=[pltpu.VMEM((tm, D_MODEL), BF16)],
        compiler_params=_params(("parallel", "arbitrary")),
        name="ffn_residual" if emit_residual else "ffn_final",
    )(x2d, g, wg, wu, wd, g2)


def _rope128(y, cos, sin):
    return y * cos + pltpu.roll(y, D_HEAD // 2, axis=1) * sin


def _rope64(y, cos, sin):
    lane = lax.broadcasted_iota(jnp.int32, y.shape, 1)
    first_half = (lane & (D_IDX - 1)) < D_IDX // 2
    partner = jnp.where(first_half,
                        pltpu.roll(y, LANES - D_IDX // 2, axis=1),
                        pltpu.roll(y, D_IDX // 2, axis=1))
    return y * cos + partner * sin


def _proj_kernel(h_ref, w_ref, cos_ref, sin_ref, o_ref, *, mode):
    y = _dot(h_ref[...], w_ref[...])
    n_chunks = y.shape[1] // LANES
    if mode == "plain":
        o_ref[...] = y.astype(o_ref.dtype)
        return
    cos = cos_ref[...]
    sin = sin_ref[...]
    chunk0 = pl.program_id(1) * n_chunks
    for c in range(n_chunks):
        yc = y[:, c * LANES:(c + 1) * LANES]
        if mode == "rope128":
            is_aq = (chunk0 + c >= AQ_CHUNK0) & (chunk0 + c < AQ_CHUNK0 + H_ATT)
            yc = _rope128(yc, cos, sin) * jnp.where(is_aq, LOG2_E * D_HEAD ** -0.5, 1.0)
        elif c < IDX_Q_COLS // LANES:
            yc = _rope64(yc, cos, sin) * (D_IDX ** -0.5)
        elif c == IDX_Q_COLS // LANES:
            yc = _rope64(yc, cos, sin)
        else:
            yc = yc * (H_IDX ** -0.5)
        o_ref[:, c * LANES:(c + 1) * LANES] = yc.astype(o_ref.dtype)


def _proj(h, w, cos, sin, *, mode, tm, tn, out_dtype):
    n_tok, n_cols = h.shape[0], w.shape[1]
    tab = pl.BlockSpec((tm, LANES), lambda i, j: (i, 0))
    return pl.pallas_call(
        functools.partial(_proj_kernel, mode=mode),
        grid=(n_tok // tm, n_cols // tn),
        in_specs=[pl.BlockSpec((tm, D_MODEL), lambda i, j: (i, 0)),
                  pl.BlockSpec((D_MODEL, tn), lambda i, j: (0, j)),
                  tab, tab],
        out_specs=pl.BlockSpec((tm, tn), lambda i, j: (i, j)),
        out_shape=jax.ShapeDtypeStruct((n_tok, n_cols), out_dtype),
        compiler_params=_params(("parallel", "parallel")),
        name="proj_" + mode,
    )(h, w, cos, sin)


RET_UNROLL = 8


def _retention_kernel(lg_ref, q_ref, k_ref, v_ref, g_ref, rn_ref, o_ref):
    C = RET_CHUNK
    n_chunks = q_ref.shape[0] // C
    lg = lg_ref[...]
    row = lax.broadcasted_iota(jnp.int32, (C, C), 0).astype(F32)
    col = lax.broadcasted_iota(jnp.int32, (C, C), 1).astype(F32)
    diff = row - col
    scale = DK_RET ** -0.5
    decay = jnp.where(diff >= 0, jnp.exp(jnp.maximum(diff, 0.0) * lg), 0.0) * scale
    k_dec = jnp.exp((C - 1 - row) * lg) * scale
    q_dec = jnp.exp((row + 1) * lg)
    g_chunk = jnp.exp(C * lg)
    rn = rn_ref[...]

    def body(n, state):
        sl = pl.ds(pl.multiple_of(n * C, C), C)
        qc = q_ref[sl, :]
        kc = k_ref[sl, :]
        vc = v_ref[sl, :]
        s = _dot_nt(qc, kc) * decay
        intra = _dot(s.astype(BF16), vc)
        cross = _dot(qc, state.astype(BF16)) * q_dec
        kv = _dot_tn((kc.astype(F32) * k_dec).astype(BF16), vc)
        o = intra + cross
        mu = jnp.mean(o, axis=-1, keepdims=True)
        d = o - mu
        var = jnp.mean(d * d, axis=-1, keepdims=True)
        y = d * lax.rsqrt(var + NORM_EPS) * rn
        gate = g_ref[sl, :].astype(F32)
        o_ref[sl, :] = (y * (gate * jax.nn.sigmoid(gate))).astype(o_ref.dtype)
        return state * g_chunk + kv

    lax.fori_loop(0, n_chunks, body, jnp.zeros((DK_RET, DV_RET), F32), unroll=RET_UNROLL)


def _retention(rope_out, plain_out, ret_norm, batch, seq):
    n_tok = batch * seq
    lg = jnp.log1p(-jnp.exp2(-5.0 - jnp.arange(H_RET, dtype=F32)))
    lg = jnp.broadcast_to(lg[:, None, None], (H_RET, 1, LANES))
    head = lambda off: pl.BlockSpec((seq, LANES), lambda b, h: (b, off + h))
    return pl.pallas_call(
        _retention_kernel,
        grid=(batch, H_RET),
        in_specs=[pl.BlockSpec((None, 1, LANES), lambda b, h: (h, 0, 0)),
                  head(0), head(H_RET),
                  head(0), head(H_RET),
                  pl.BlockSpec((1, LANES), lambda b, h: (0, h))],
        out_specs=head(0),
        out_shape=jax.ShapeDtypeStruct((n_tok, RET_W), BF16),
        compiler_params=_params(("parallel", "parallel")),
        name="retention",
    )(lg, rope_out, rope_out, plain_out, plain_out, ret_norm)


KEY_TILE = 512
SCORE_ROWS = 128
FLT_MAX_KEY = 0x00800000
ATT_HEADS_PER_CHAIN = 2
LOG2_E = 1.4426950408889634


def _order_key_to_float(u):
    s = u ^ INT_MIN
    return pltpu.bitcast(s ^ ((s >> 31) & jnp.int32(0x7FFFFFFF)), F32)


def _attn_kernel(iq_ref, ik_ref, iw_ref, aq_ref, ak_ref, av_ref, o_ref, sc_ref, *, top_k):
    qb = pl.program_id(1)
    t0 = qb * Q_BLOCK
    n_tiles = (t0 + Q_BLOCK + KEY_TILE - 1) // KEY_TILE

    w_t = iw_ref[...].astype(F32).T
    w_rows = [w_t[h:h + 1, :] for h in range(H_IDX)]

    lane = lax.broadcasted_iota(jnp.int32, (Q_BLOCK, LANES), 1)
    low = lane < D_IDX
    q_pairs = []
    for c in range(IDX_Q_COLS // LANES):
        qc = iq_ref[:, c * LANES:(c + 1) * LANES]
        zero = jnp.zeros_like(qc)
        q_pairs.append(jnp.concatenate([jnp.where(low, qc, zero), jnp.where(low, zero, qc)], axis=0))

    q_pos = t0 + lax.broadcasted_iota(jnp.int32, (SCORE_ROWS, Q_BLOCK), 1)
    key_off = lax.broadcasted_iota(jnp.int32, (SCORE_ROWS, Q_BLOCK), 0)

    def score_tile(kt, carry):
        for s in range(KEY_TILE // SCORE_ROWS):
            base = pl.multiple_of(kt * KEY_TILE + s * SCORE_ROWS, SCORE_ROWS)
            kk = ik_ref[pl.ds(base, SCORE_ROWS), :]
            acc = jnp.zeros((SCORE_ROWS, Q_BLOCK), F32)
            for c, qp in enumerate(q_pairs):
                z = _dot_nt(kk, qp)
                acc = acc + jnp.maximum(z[:, :Q_BLOCK], 0.0) * w_rows[2 * c]
                acc = acc + jnp.maximum(z[:, Q_BLOCK:], 0.0) * w_rows[2 * c + 1]
            causal = base + key_off <= q_pos
            sc_ref[pl.ds(base, SCORE_ROWS), :] = jnp.where(causal, acc, -jnp.inf)
        return carry

    lax.fori_loop(0, n_tiles, score_tile, 0)

    def search_bit(it, tau_u):
        cand_u = tau_u | lax.shift_left(jnp.int32(1), 31 - it)
        cand = _order_key_to_float(cand_u)

        def count_tile(kt, cnt):
            base = pl.multiple_of(kt * KEY_TILE, KEY_TILE)
            ge = sc_ref[pl.ds(base, KEY_TILE), :] >= cand
            part = jnp.where(ge, 1.0, 0.0).reshape(KEY_TILE // SUBLANES, SUBLANES, Q_BLOCK)
            while part.shape[0] > 1:
                half = part.shape[0] // 2
                part = part[:half] + part[half:]
            return cnt + part[0]

        cnt = lax.fori_loop(0, n_tiles, count_tile, jnp.zeros((SUBLANES, Q_BLOCK), F32))
        total = jnp.sum(cnt, axis=0, keepdims=True)
        return jnp.where(total >= top_k, cand_u, tau_u)

    searched = t0 + Q_BLOCK > top_k
    n_bits = jnp.where(searched, 32, 0)
    tau_u = lax.fori_loop(0, n_bits, search_bit, jnp.zeros((1, Q_BLOCK), jnp.int32))
    tau = _order_key_to_float(jnp.where(searched, tau_u, FLT_MAX_KEY))

    n_chains = H_ATT // ATT_HEADS_PER_CHAIN
    width = ATT_HEADS_PER_CHAIN * Q_BLOCK
    q_chains = []
    for c in range(n_chains):
        heads = range(c * ATT_HEADS_PER_CHAIN, (c + 1) * ATT_HEADS_PER_CHAIN)
        q_chains.append(jnp.concatenate([aq_ref[:, h * D_HEAD:(h + 1) * D_HEAD] for h in heads], axis=0))

    def attend_tile(kt, carry):
        base = pl.multiple_of(kt * KEY_TILE, KEY_TILE)
        bias = jnp.where(sc_ref[pl.ds(base, KEY_TILE), :] >= tau, 0.0, NEG_BIG)
        bias = jnp.concatenate([bias] * ATT_HEADS_PER_CHAIN, axis=1)
        kv_of = lambda c: c * ATT_HEADS_PER_CHAIN // GROUP
        k_tiles = [ak_ref[pl.ds(base, KEY_TILE), g * D_HEAD:(g + 1) * D_HEAD] for g in range(H_KV)]
        v_tiles = [av_ref[pl.ds(base, KEY_TILE), g * D_HEAD:(g + 1) * D_HEAD] for g in range(H_KV)]
        logits = [_dot_nt(k_tiles[kv_of(c)], q_chains[c]) + bias for c in range(n_chains)]
        stats = []
        for c in range(n_chains):
            m_old, l_old, _ = carry[c]
            m_new = jnp.maximum(m_old, jnp.max(logits[c], axis=0, keepdims=True))
            alpha = jnp.exp2(m_old - m_new)
            p = jnp.exp2(logits[c] - m_new)
            l_new = alpha * l_old + jnp.sum(p, axis=0, keepdims=True)
            stats.append((m_new, l_new, alpha, p.astype(BF16)))
        new = []
        for c in range(n_chains):
            m_new, l_new, alpha, p = stats[c]
            acc_new = alpha * carry[c][2] + _dot_tn(v_tiles[kv_of(c)], p)
            new.append((m_new, l_new, acc_new))
        return tuple(new)

    init = tuple((jnp.full((1, width), NEG_BIG, F32), jnp.zeros((1, width), F32),
                  jnp.zeros((D_HEAD, width), F32)) for _ in range(n_chains))
    final = lax.fori_loop(0, n_tiles, attend_tile, init)
    for c in range(n_chains):
        _, l_fin, acc_fin = final[c]
        out_t = acc_fin / l_fin
        for j in range(ATT_HEADS_PER_CHAIN):
            hcol = (c * ATT_HEADS_PER_CHAIN + j) * D_HEAD
            o_ref[:, hcol:hcol + D_HEAD] = out_t[:, j * Q_BLOCK:(j + 1) * Q_BLOCK].T.astype(o_ref.dtype)


def _sparse_attention(idx_out, rope_out, plain_out, batch, seq):
    n_tok = batch * seq
    nb = seq // Q_BLOCK
    top_k = min(MAX_TOPK, seq // 4)
    kv_w = H_KV * D_HEAD
    return pl.pallas_call(
        functools.partial(_attn_kernel, top_k=top_k),
        grid=(batch, nb),
        in_specs=[
            pl.BlockSpec((Q_BLOCK, IDX_Q_COLS), lambda b, q: (b * nb + q, 0)),
            pl.BlockSpec((seq, LANES), lambda b, q: (b, IDX_Q_COLS // LANES)),
            pl.BlockSpec((Q_BLOCK, LANES), lambda b, q: (b * nb + q, IDX_Q_COLS // LANES + 1)),
            pl.BlockSpec((Q_BLOCK, ATT_W), lambda b, q: (b * nb + q, 2 * RET_W // ATT_W)),
            pl.BlockSpec((seq, kv_w), lambda b, q: (b, (2 * RET_W + ATT_W) // kv_w)),
            pl.BlockSpec((seq, kv_w), lambda b, q: (b, 2 * RET_W // kv_w)),
        ],
        out_specs=pl.BlockSpec((Q_BLOCK, ATT_W), lambda b, q: (b * nb + q, 0)),
        out_shape=jax.ShapeDtypeStruct((n_tok, ATT_W), BF16),
        scratch_shapes=[pltpu.VMEM((seq, Q_BLOCK), F32)],
        compiler_params=_params(("parallel", "arbitrary")),
        name="sparse_attention",
    )(idx_out, idx_out, idx_out, rope_out, rope_out, plain_out)


def _out_proj_kernel(x_ref, ro_ref, ao_ref, wr_ref, wa_ref, o_ref):
    o_ref[...] = x_ref[...] + _dot(ro_ref[...], wr_ref[...]) + _dot(ao_ref[...], wa_ref[...])


def _out_proj(x2d, ro, ao, w_ret, w_att, tm=512):
    n_tok = x2d.shape[0]
    tile = pl.BlockSpec((tm, D_MODEL), lambda i: (i, 0))
    half = pl.BlockSpec((tm, RET_W), lambda i: (i, 0))
    wspec = pl.BlockSpec((RET_W, D_MODEL), lambda i: (0, 0))
    return pl.pallas_call(
        _out_proj_kernel,
        grid=(n_tok // tm,),
        in_specs=[tile, half, half, wspec, wspec],
        out_specs=tile,
        out_shape=jax.ShapeDtypeStruct((n_tok, D_MODEL), F32),
        compiler_params=_params(("parallel",)),
        name="out_proj",
    )(x2d, ro, ao, w_ret, w_att)


def _layer(x2d, tables, batch, seq, ffn1_norm, ffn1_w_gate, ffn1_w_up, ffn1_w_down, mix_norm,
           w_in, ret_norm, w_out, ffn2_norm, ffn2_w_gate, ffn2_w_up, ffn2_w_down, final_norm, last):
    cos_a, sin_a, cos_b, sin_b = tables
    bf = lambda w: w.astype(BF16)
    row = lambda g: g.reshape(1, -1).astype(F32)

    rq, rk, rv, rg, aq, ak, av, iq, ik, iw = jnp.split(
        w_in, [1024, 2048, 3072, 4096, 5120, 5376, 5632, 6656, 6720], axis=1)
    w_rope = bf(jnp.concatenate([rq, rk, aq, ak], axis=1))
    w_plain = bf(jnp.concatenate([rv, rg, av], axis=1))
    pad = jnp.zeros((D_MODEL, LANES - H_IDX), w_in.dtype)
    w_idx = bf(jnp.concatenate([iq, ik, ik, iw, pad], axis=1))

    x1, h = _ffn(x2d, row(ffn1_norm), ffn1_w_gate, ffn1_w_up, ffn1_w_down,
                 row(mix_norm), emit_residual=True)
    rope_out = _proj(h, w_rope, cos_a, sin_a, mode="rope128", tm=512, tn=ROPE_COLS // 2, out_dtype=BF16)
    plain_out = _proj(h, w_plain, cos_a, sin_a, mode="plain", tm=512, tn=PLAIN_COLS // 2, out_dtype=BF16)
    idx_out = _proj(h, w_idx, cos_b, sin_b, mode="idx", tm=512, tn=IDX_COLS, out_dtype=BF16)
    ro = _retention(rope_out, plain_out, row(ret_norm), batch, seq)
    ao = _sparse_attention(idx_out, rope_out, plain_out, batch, seq)
    x2 = _out_proj(x1, ro, ao, bf(w_out[:RET_W]), bf(w_out[RET_W:]))
    out = _ffn(x2, row(ffn2_norm), ffn2_w_gate, ffn2_w_up, ffn2_w_down,
               row(final_norm), emit_residual=not last)
    return out if last else out[0]


def kernel(x, positions, ffn1_norm, ffn1_w_gate, ffn1_w_up, ffn1_w_down, mix_norm, w_in, ret_norm,
           w_out, ffn2_norm, ffn2_w_gate, ffn2_w_up, ffn2_w_down, final_norm):
    batch, seq, _ = x.shape
    depth = w_in.shape[0]
    tables = _rope_tables(positions)
    x2d = x.reshape(batch * seq, D_MODEL)
    for l in range(depth):
        last = l == depth - 1
        x2d = _layer(x2d, tables, batch, seq, ffn1_norm[l], ffn1_w_gate[l], ffn1_w_up[l], ffn1_w_down[l],
                     mix_norm[l], w_in[l], ret_norm[l], w_out[l], ffn2_norm[l], ffn2_w_gate[l],
                     ffn2_w_up[l], ffn2_w_down[l], final_norm, last)
    return x2d.reshape(batch, seq, D_MODEL)
```

```python
import functools

import jax
import jax.numpy as jnp
from jax import lax
from jax.experimental import pallas as pl
from jax.experimental.pallas import tpu as pltpu

D_MODEL = 2048
H_RET = 8
DK_RET = 128
DV_RET = 128
RET_CHUNK = 128
H_ATT = 8
H_KV = 2
D_HEAD = 128
H_IDX = 16
D_IDX = 64
MAX_TOPK = 256
Q_BLOCK = 256
D_FF = 5632
ROPE_THETA = 10000.0
NORM_EPS = 1e-6

RET_W = H_RET * DV_RET
ATT_W = H_ATT * D_HEAD
GROUP = H_ATT // H_KV

LANES = 128
SUBLANES = 8
VMEM_LIMIT = 56 * 1024 * 1024

ROPE_COLS = 2 * H_RET * DK_RET + ATT_W + H_KV * D_HEAD
PLAIN_COLS = 2 * RET_W + H_KV * D_HEAD
IDX_Q_COLS = H_IDX * D_IDX
IDX_COLS = IDX_Q_COLS + 2 * LANES
AQ_CHUNK0 = 2 * H_RET * DK_RET // LANES

INT_MIN = -2 ** 31
NEG_BIG = -1e30

F32 = jnp.float32
BF16 = jnp.bfloat16


def _dot(a, b):
    return jnp.dot(a, b, preferred_element_type=F32)


def _dot_nt(a, b):
    return lax.dot_general(a, b, (((1,), (1,)), ((), ())), preferred_element_type=F32)


def _dot_tn(a, b):
    return lax.dot_general(a, b, (((0,), (0,)), ((), ())), preferred_element_type=F32)


def _rmsnorm(xf, g):
    ms = jnp.mean(xf * xf, axis=-1, keepdims=True)
    return xf * lax.rsqrt(ms + NORM_EPS) * g


def _params(sem):
    return pltpu.CompilerParams(dimension_semantics=sem, vmem_limit_bytes=VMEM_LIMIT)


W_IN_SPLITS = (0, 1024, 2048, 3072, 4096, 5120, 5376, 5632, 6656, 6720, 6736)


def _prep_kernel(pos_ref, inv_a_ref, inv_b_ref, sgn_a_ref, sgn_b_ref, w_in_ref,
                 cos_a_ref, sin_a_ref, cos_b_ref, sin_b_ref, w_rope_ref, w_plain_ref, w_idx_ref):
    p = pos_ref[...].astype(F32)
    ang_a = p * inv_a_ref[...]
    ang_b = p * inv_b_ref[...]
    cos_a_ref[...] = jnp.cos(ang_a)
    sin_a_ref[...] = jnp.sin(ang_a) * sgn_a_ref[...]
    cos_b_ref[...] = jnp.cos(ang_b)
    sin_b_ref[...] = jnp.sin(ang_b) * sgn_b_ref[...]

    rq, rk, rv, rg, aq, ak, av, iq, ik, iw, end = W_IN_SPLITS
    cols = lambda lo, hi: w_in_ref[:, lo:hi].astype(BF16)
    w_rope_ref[:, :rv - rq] = cols(rq, rv)
    w_rope_ref[:, rv - rq:] = cols(aq, av)
    w_plain_ref[:, :aq - rv] = cols(rv, aq)
    w_plain_ref[:, aq - rv:] = cols(av, iq)
    w_idx_ref[:, :ik - iq] = cols(iq, ik)
    key = cols(ik, iw)
    mix = cols(iw, end)
    w_idx_ref[:, ik - iq:ik - iq + LANES] = jnp.concatenate([key, key], axis=1)
    w_idx_ref[:, ik - iq + LANES:] = jnp.concatenate(
        [mix, jnp.zeros((mix.shape[0], LANES - H_IDX), BF16)], axis=1)


def _prepare(positions, w_in2d):
    n_tok = positions.size
    tm = 1024
    n_steps = n_tok // tm
    w_rows = w_in2d.shape[0] // n_steps
    lane = jnp.arange(LANES)

    def inv_freq(d):
        inv = ROPE_THETA ** (-jnp.arange(0, d, 2, dtype=F32) / d)
        return inv[lane % (d // 2)][None, :]

    def sign(d):
        return jnp.where(lane % d < d // 2, -1.0, 1.0).astype(F32)[None, :]

    row = pl.BlockSpec((1, LANES), lambda i: (0, 0))
    tab = pl.BlockSpec((tm, LANES), lambda i: (i, 0))
    out = jax.ShapeDtypeStruct((n_tok, LANES), F32)
    slab = lambda n_cols: pl.BlockSpec((w_rows, n_cols), lambda i: (i, 0))
    w_out = lambda n_cols: jax.ShapeDtypeStruct((w_in2d.shape[0], n_cols), BF16)
    outs = pl.pallas_call(
        _prep_kernel,
        grid=(n_steps,),
        in_specs=[pl.BlockSpec((tm, 1), lambda i: (i, 0)), row, row, row, row, slab(w_in2d.shape[1])],
        out_specs=[tab, tab, tab, tab, slab(ROPE_COLS), slab(PLAIN_COLS), slab(IDX_COLS)],
        out_shape=[out, out, out, out, w_out(ROPE_COLS), w_out(PLAIN_COLS), w_out(IDX_COLS)],
        compiler_params=_params(("parallel",)),
        name="prepare",
    )(positions.reshape(n_tok, 1), inv_freq(D_HEAD), inv_freq(D_IDX), sign(D_HEAD), sign(D_IDX), w_in2d)
    return outs[:4], outs[4:]


FFN_ROW_CHUNK = 128


def _ffn_kernel(x_ref, g_ref, wg_ref, wu_ref, wd_ref, g2_ref, *refs, emit_residual):
    if emit_residual:
        res_ref, normed_ref, xn_ref = refs
        acc_ref = res_ref
    else:
        normed_ref, xn_ref = refs
        acc_ref = normed_ref
    f = pl.program_id(1)
    n_row_chunks = x_ref.shape[0] // FFN_ROW_CHUNK

    def rows_of(r):
        return pl.ds(pl.multiple_of(r * FFN_ROW_CHUNK, FFN_ROW_CHUNK), FFN_ROW_CHUNK)

    @pl.when(f == 0)
    def _():
        def prologue(r, carry):
            xf = x_ref[rows_of(r), :]
            xn_ref[rows_of(r), :] = _rmsnorm(xf, g_ref[...]).astype(BF16)
            acc_ref[rows_of(r), :] = 2.0 * xf
            return carry
        lax.fori_loop(0, n_row_chunks, prologue, 0)

    xn = xn_ref[...]
    a = _dot(xn, wg_ref[...].astype(BF16))
    b = _dot(xn, wu_ref[...].astype(BF16))
    hidden = (a * jax.nn.sigmoid(a) * b).astype(BF16)
    acc_ref[...] += _dot(hidden, wd_ref[...].astype(BF16))

    @pl.when(f == pl.num_programs(1) - 1)
    def _():
        def epilogue(r, carry):
            y = 0.5 * acc_ref[rows_of(r), :]
            if emit_residual:
                res_ref[rows_of(r), :] = y
            normed_ref[rows_of(r), :] = _rmsnorm(y, g2_ref[...]).astype(normed_ref.dtype)
            return carry
        lax.fori_loop(0, n_row_chunks, epilogue, 0)


FFN_TM = 1024
FFN_TF_F32 = 256
FFN_TF_BF16 = 512


def _ffn(x2d, g, wg, wu, wd, g2, *, emit_residual, tf, tm=FFN_TM):
    n_tok = x2d.shape[0]
    tile_map = lambda i, f: (i, 0)
    x_tile = pl.BlockSpec((tm, D_MODEL), tile_map, pipeline_mode=pl.Buffered(1))
    out_tile = pl.BlockSpec((tm, D_MODEL), tile_map)
    row = pl.BlockSpec((1, D_MODEL), lambda i, f: (0, 0))
    if emit_residual:
        out_specs = [out_tile, out_tile]
        out_shape = [jax.ShapeDtypeStruct((n_tok, D_MODEL), F32),
                     jax.ShapeDtypeStruct((n_tok, D_MODEL), BF16)]
    else:
        out_specs = out_tile
        out_shape = jax.ShapeDtypeStruct((n_tok, D_MODEL), F32)
    return pl.pallas_call(
        functools.partial(_ffn_kernel, emit_residual=emit_residual),
        grid=(n_tok // tm, D_FF // tf),
        in_specs=[x_tile, row,
                  pl.BlockSpec((D_MODEL, tf), lambda i, f: (0, f)),
                  pl.BlockSpec((D_MODEL, tf), lambda i, f: (0, f)),
                  pl.BlockSpec((tf, D_MODEL), lambda i, f: (f, 0)),
                  row],
        out_specs=out_specs,
        out_shape=out_shape,
        scratch_shapes=[pltpu.VMEM((tm, D_MODEL), BF16)],
        compiler_params=_params(("parallel", "arbitrary")),
        name="ffn_residual" if emit_residual else "ffn_final",
    )(x2d, g, wg, wu, wd, g2)


def _rope128(y, cos, sin):
    return y * cos + pltpu.roll(y, D_HEAD // 2, axis=1) * sin


def _rope64(y, cos, sin):
    lane = lax.broadcasted_iota(jnp.int32, y.shape, 1)
    first_half = (lane & (D_IDX - 1)) < D_IDX // 2
    partner = jnp.where(first_half,
                        pltpu.roll(y, LANES - D_IDX // 2, axis=1),
                        pltpu.roll(y, D_IDX // 2, axis=1))
    return y * cos + partner * sin


def _proj_kernel(h_ref, w_ref, cos_ref, sin_ref, o_ref, *, mode):
    y = _dot(h_ref[...], w_ref[...])
    n_chunks = y.shape[1] // LANES
    if mode == "plain":
        o_ref[...] = y.astype(o_ref.dtype)
        return
    cos = cos_ref[...]
    sin = sin_ref[...]
    chunk0 = pl.program_id(0) * n_chunks
    for c in range(n_chunks):
        yc = y[:, c * LANES:(c + 1) * LANES]
        if mode == "rope128":
            is_aq = (chunk0 + c >= AQ_CHUNK0) & (chunk0 + c < AQ_CHUNK0 + H_ATT)
            yc = _rope128(yc, cos, sin) * jnp.where(is_aq, LOG2_E * D_HEAD ** -0.5, 1.0)
        elif c < IDX_Q_COLS // LANES:
            yc = _rope64(yc, cos, sin) * (D_IDX ** -0.5)
        elif c == IDX_Q_COLS // LANES:
            yc = _rope64(yc, cos, sin)
        else:
            yc = yc * (H_IDX ** -0.5)
        o_ref[:, c * LANES:(c + 1) * LANES] = yc.astype(o_ref.dtype)


PROJ_TM = 1024


def _proj(h, w, cos, sin, *, mode, tm, tn, out_dtype):
    n_tok, n_cols = h.shape[0], w.shape[1]
    tab = pl.BlockSpec((tm, LANES), lambda j, i: (i, 0))
    return pl.pallas_call(
        functools.partial(_proj_kernel, mode=mode),
        grid=(n_cols // tn, n_tok // tm),
        in_specs=[pl.BlockSpec((tm, D_MODEL), lambda j, i: (i, 0)),
                  pl.BlockSpec((D_MODEL, tn), lambda j, i: (0, j)),
                  tab, tab],
        out_specs=pl.BlockSpec((tm, tn), lambda j, i: (i, j)),
        out_shape=jax.ShapeDtypeStruct((n_tok, n_cols), out_dtype),
        compiler_params=_params(("parallel", "parallel")),
        name="proj_" + mode,
    )(h, w, cos, sin)


RET_UNROLL = 8


def _retention_kernel(lg_ref, q_ref, k_ref, v_ref, g_ref, rn_ref, o_ref):
    C = RET_CHUNK
    n_chunks = q_ref.shape[0] // C
    lg = lg_ref[...]
    row = lax.broadcasted_iota(jnp.int32, (C, C), 0).astype(F32)
    col = lax.broadcasted_iota(jnp.int32, (C, C), 1).astype(F32)
    diff = row - col
    scale = DK_RET ** -0.5
    decay = jnp.where(diff >= 0, jnp.exp(jnp.maximum(diff, 0.0) * lg), 0.0) * scale
    k_dec = jnp.exp((C - 1 - row) * lg) * scale
    q_dec = jnp.exp((row + 1) * lg)
    g_chunk = jnp.exp(C * lg)
    rn = rn_ref[...]

    def body(n, state):
        sl = pl.ds(pl.multiple_of(n * C, C), C)
        qc = q_ref[sl, :]
        kc = k_ref[sl, :]
        vc = v_ref[sl, :]
        s = _dot_nt(qc, kc) * decay
        intra = _dot(s.astype(BF16), vc)
        cross = _dot(qc, state.astype(BF16)) * q_dec
        kv = _dot_tn((kc.astype(F32) * k_dec).astype(BF16), vc)
        o = intra + cross
        mu = jnp.mean(o, axis=-1, keepdims=True)
        d = o - mu
        var = jnp.mean(d * d, axis=-1, keepdims=True)
        y = d * lax.rsqrt(var + NORM_EPS) * rn
        gate = g_ref[sl, :].astype(F32)
        o_ref[sl, :] = (y * (gate * jax.nn.sigmoid(gate))).astype(o_ref.dtype)
        return state * g_chunk + kv

    lax.fori_loop(0, n_chunks, body, jnp.zeros((DK_RET, DV_RET), F32), unroll=RET_UNROLL)


def _retention(rope_out, plain_out, ret_norm, batch, seq):
    n_tok = batch * seq
    lg = jnp.log1p(-jnp.exp2(-5.0 - jnp.arange(H_RET, dtype=F32)))
    lg = jnp.broadcast_to(lg[:, None, None], (H_RET, 1, LANES))
    head = lambda off: pl.BlockSpec((seq, LANES), lambda b, h: (b, off + h))
    return pl.pallas_call(
        _retention_kernel,
        grid=(batch, H_RET),
        in_specs=[pl.BlockSpec((None, 1, LANES), lambda b, h: (h, 0, 0)),
                  head(0), head(H_RET),
                  head(0), head(H_RET),
                  pl.BlockSpec((1, LANES), lambda b, h: (0, h))],
        out_specs=head(0),
        out_shape=jax.ShapeDtypeStruct((n_tok, RET_W), BF16),
        compiler_params=_params(("parallel", "parallel")),
        name="retention",
    )(lg, rope_out, rope_out, plain_out, plain_out, ret_norm)


KEY_TILE = 512
SCORE_ROWS = 128
FLT_MAX_KEY = 0x00800000
ATT_HEADS_PER_CHAIN = 2
LOG2_E = 1.4426950408889634


def _order_key_to_float(u):
    s = u ^ INT_MIN
    return pltpu.bitcast(s ^ ((s >> 31) & jnp.int32(0x7FFFFFFF)), F32)


def _attn_kernel(iq_ref, ik_ref, iw_ref, aq_ref, ak_ref, av_ref, *rest, top_k, n_cast):
    o_ref, sc_ref = rest[n_cast], rest[-1]
    for w32_ref, w16_ref in zip(rest[:n_cast], rest[n_cast + 1:-1]):
        w16_ref[...] = w32_ref[...].astype(BF16)

    qb = pl.program_id(1)
    t0 = qb * Q_BLOCK
    n_tiles = (t0 + Q_BLOCK + KEY_TILE - 1) // KEY_TILE

    w_t = iw_ref[...].astype(F32).T
    w_rows = [w_t[h:h + 1, :] for h in range(H_IDX)]

    lane = lax.broadcasted_iota(jnp.int32, (Q_BLOCK, LANES), 1)
    low = lane < D_IDX
    q_pairs = []
    for c in range(IDX_Q_COLS // LANES):
        qc = iq_ref[:, c * LANES:(c + 1) * LANES]
        zero = jnp.zeros_like(qc)
        q_pairs.append(jnp.concatenate([jnp.where(low, qc, zero), jnp.where(low, zero, qc)], axis=0))

    q_pos = t0 + lax.broadcasted_iota(jnp.int32, (SCORE_ROWS, Q_BLOCK), 1)
    key_off = lax.broadcasted_iota(jnp.int32, (SCORE_ROWS, Q_BLOCK), 0)

    def score_tile(kt, carry):
        for s in range(KEY_TILE // SCORE_ROWS):
            base = pl.multiple_of(kt * KEY_TILE + s * SCORE_ROWS, SCORE_ROWS)
            kk = ik_ref[pl.ds(base, SCORE_ROWS), :]
            acc = jnp.zeros((SCORE_ROWS, Q_BLOCK), F32)
            for c, qp in enumerate(q_pairs):
                z = _dot_nt(kk, qp)
                acc = acc + jnp.maximum(z[:, :Q_BLOCK], 0.0) * w_rows[2 * c]
                acc = acc + jnp.maximum(z[:, Q_BLOCK:], 0.0) * w_rows[2 * c + 1]
            causal = base + key_off <= q_pos
            sc_ref[pl.ds(base, SCORE_ROWS), :] = jnp.where(causal, acc, -jnp.inf)
        return carry

    lax.fori_loop(0, n_tiles, score_tile, 0)

    def search_bit(it, tau_u):
        cand_u = tau_u | lax.shift_left(jnp.int32(1), 31 - it)
        cand = _order_key_to_float(cand_u)

        def count_tile(kt, cnt):
            base = pl.multiple_of(kt * KEY_TILE, KEY_TILE)
            ge = sc_ref[pl.ds(base, KEY_TILE), :] >= cand
            part = jnp.where(ge, 1.0, 0.0).reshape(KEY_TILE // SUBLANES, SUBLANES, Q_BLOCK)
            while part.shape[0] > 1:
                half = part.shape[0] // 2
                part = part[:half] + part[half:]
            return cnt + part[0]

        cnt = lax.fori_loop(0, n_tiles, count_tile, jnp.zeros((SUBLANES, Q_BLOCK), F32))
        total = jnp.sum(cnt, axis=0, keepdims=True)
        return jnp.where(total >= top_k, cand_u, tau_u)

    searched = t0 + Q_BLOCK > top_k
    n_bits = jnp.where(searched, 32, 0)
    tau_u = lax.fori_loop(0, n_bits, search_bit, jnp.zeros((1, Q_BLOCK), jnp.int32))
    tau = _order_key_to_float(jnp.where(searched, tau_u, FLT_MAX_KEY))

    n_chains = H_ATT // ATT_HEADS_PER_CHAIN
    width = ATT_HEADS_PER_CHAIN * Q_BLOCK
    q_chains = []
    for c in range(n_chains):
        heads = range(c * ATT_HEADS_PER_CHAIN, (c + 1) * ATT_HEADS_PER_CHAIN)
        q_chains.append(jnp.concatenate([aq_ref[:, h * D_HEAD:(h + 1) * D_HEAD] for h in heads], axis=0))

    def attend_tile(kt, carry):
        base = pl.multiple_of(kt * KEY_TILE, KEY_TILE)
        bias = jnp.where(sc_ref[pl.ds(base, KEY_TILE), :] >= tau, 0.0, NEG_BIG)
        bias = jnp.concatenate([bias] * ATT_HEADS_PER_CHAIN, axis=1)
        kv_of = lambda c: c * ATT_HEADS_PER_CHAIN // GROUP
        k_tiles = [ak_ref[pl.ds(base, KEY_TILE), g * D_HEAD:(g + 1) * D_HEAD] for g in range(H_KV)]
        v_tiles = [av_ref[pl.ds(base, KEY_TILE), g * D_HEAD:(g + 1) * D_HEAD] for g in range(H_KV)]
        logits = [_dot_nt(k_tiles[kv_of(c)], q_chains[c]) + bias for c in range(n_chains)]
        stats = []
        for c in range(n_chains):
            m_old, l_old, _ = carry[c]
            m_new = jnp.maximum(m_old, jnp.max(logits[c], axis=0, keepdims=True))
            alpha = jnp.exp2(m_old - m_new)
            p = jnp.exp2(logits[c] - m_new)
            l_new = alpha * l_old + jnp.sum(p, axis=0, keepdims=True)
            stats.append((m_new, l_new, alpha, p.astype(BF16)))
        new = []
        for c in range(n_chains):
            m_new, l_new, alpha, p = stats[c]
            acc_new = alpha * carry[c][2] + _dot_tn(v_tiles[kv_of(c)], p)
            new.append((m_new, l_new, acc_new))
        return tuple(new)

    init = tuple((jnp.full((1, width), NEG_BIG, F32), jnp.zeros((1, width), F32),
                  jnp.zeros((D_HEAD, width), F32)) for _ in range(n_chains))
    final = lax.fori_loop(0, n_tiles, attend_tile, init)
    for c in range(n_chains):
        _, l_fin, acc_fin = final[c]
        out_t = acc_fin / l_fin
        for j in range(ATT_HEADS_PER_CHAIN):
            hcol = (c * ATT_HEADS_PER_CHAIN + j) * D_HEAD
            o_ref[:, hcol:hcol + D_HEAD] = out_t[:, j * Q_BLOCK:(j + 1) * Q_BLOCK].T.astype(o_ref.dtype)


def _sparse_attention(idx_out, rope_out, plain_out, batch, seq, weights_to_cast):
    n_tok = batch * seq
    nb = seq // Q_BLOCK
    n_steps = batch * nb
    top_k = min(MAX_TOPK, seq // 4)
    kv_w = H_KV * D_HEAD
    step = lambda b, q: (b * nb + q, 0)
    slabs = [pl.BlockSpec((w.shape[0] // n_steps, w.shape[1]), step) for w in weights_to_cast]
    outs = pl.pallas_call(
        functools.partial(_attn_kernel, top_k=top_k, n_cast=len(weights_to_cast)),
        grid=(batch, nb),
        in_specs=[
            pl.BlockSpec((Q_BLOCK, IDX_Q_COLS), step),
            pl.BlockSpec((seq, LANES), lambda b, q: (b, IDX_Q_COLS // LANES)),
            pl.BlockSpec((Q_BLOCK, LANES), lambda b, q: (b * nb + q, IDX_Q_COLS // LANES + 1)),
            pl.BlockSpec((Q_BLOCK, ATT_W), lambda b, q: (b * nb + q, 2 * RET_W // ATT_W)),
            pl.BlockSpec((seq, kv_w), lambda b, q: (b, (2 * RET_W + ATT_W) // kv_w)),
            pl.BlockSpec((seq, kv_w), lambda b, q: (b, 2 * RET_W // kv_w)),
        ] + slabs,
        out_specs=[pl.BlockSpec((Q_BLOCK, ATT_W), step)] + slabs,
        out_shape=[jax.ShapeDtypeStruct((n_tok, ATT_W), BF16)]
                  + [jax.ShapeDtypeStruct(w.shape, BF16) for w in weights_to_cast],
        scratch_shapes=[pltpu.VMEM((seq, Q_BLOCK), F32)],
        compiler_params=_params(("parallel", "arbitrary")),
        name="sparse_attention",
    )(idx_out, idx_out, idx_out, rope_out, rope_out, plain_out, *weights_to_cast)
    return outs[0], outs[1:]


def _out_proj_kernel(x_ref, ro_ref, ao_ref, wr_ref, wa_ref, o_ref):
    o_ref[...] = x_ref[...] + _dot(ro_ref[...], wr_ref[...]) + _dot(ao_ref[...], wa_ref[...])


def _out_proj(x2d, ro, ao, w_out, tm=512):
    n_tok = x2d.shape[0]
    tile = pl.BlockSpec((tm, D_MODEL), lambda i: (i, 0))
    half = pl.BlockSpec((tm, RET_W), lambda i: (i, 0))
    w_ret = pl.BlockSpec((RET_W, D_MODEL), lambda i: (0, 0))
    w_att = pl.BlockSpec((ATT_W, D_MODEL), lambda i: (RET_W // ATT_W, 0))
    return pl.pallas_call(
        _out_proj_kernel,
        grid=(n_tok // tm,),
        in_specs=[tile, half, half, w_ret, w_att],
        out_specs=tile,
        out_shape=jax.ShapeDtypeStruct((n_tok, D_MODEL), F32),
        compiler_params=_params(("parallel",)),
        name="out_proj",
    )(x2d, ro, ao, w_out, w_out)


def _layer(x2d, tables, batch, seq, ffn1_norm, ffn1_w_gate, ffn1_w_up, ffn1_w_down, mix_norm,
           w_in_groups, ret_norm, w_out, ffn2_norm, ffn2_w_gate, ffn2_w_up, ffn2_w_down, final_norm, last):
    cos_a, sin_a, cos_b, sin_b = tables
    w_rope, w_plain, w_idx = w_in_groups
    row = lambda g: g.reshape(1, -1).astype(F32)

    x1, h = _ffn(x2d, row(ffn1_norm), ffn1_w_gate, ffn1_w_up, ffn1_w_down,
                 row(mix_norm), emit_residual=True, tf=FFN_TF_F32)
    rope_out = _proj(h, w_rope, cos_a, sin_a, mode="rope128", tm=PROJ_TM, tn=ROPE_COLS // 2, out_dtype=BF16)
    plain_out = _proj(h, w_plain, cos_a, sin_a, mode="plain", tm=PROJ_TM, tn=PLAIN_COLS // 2, out_dtype=BF16)
    idx_out = _proj(h, w_idx, cos_b, sin_b, mode="idx", tm=PROJ_TM, tn=IDX_COLS, out_dtype=BF16)
    ro = _retention(rope_out, plain_out, row(ret_norm), batch, seq)
    ao, (w_out16, wg16, wu16, wd16) = _sparse_attention(
        idx_out, rope_out, plain_out, batch, seq, [w_out, ffn2_w_gate, ffn2_w_up, ffn2_w_down])
    x2 = _out_proj(x1, ro, ao, w_out16)
    out = _ffn(x2, row(ffn2_norm), wg16, wu16, wd16, row(final_norm), emit_residual=not last,
               tf=FFN_TF_BF16)
    return out if last else out[0]


def kernel(x, positions, ffn1_norm, ffn1_w_gate, ffn1_w_up, ffn1_w_down, mix_norm, w_in, ret_norm,
           w_out, ffn2_norm, ffn2_w_gate, ffn2_w_up, ffn2_w_down, final_norm):
    batch, seq, _ = x.shape
    depth = w_in.shape[0]
    tables, w_in_groups = _prepare(positions, w_in.reshape(depth * D_MODEL, w_in.shape[2]))
    x2d = x.reshape(batch * seq, D_MODEL)
    for l in range(depth):
        last = l == depth - 1
        groups_l = [w[l * D_MODEL:(l + 1) * D_MODEL] for w in w_in_groups]
        x2d = _layer(x2d, tables, batch, seq, ffn1_norm[l], ffn1_w_gate[l], ffn1_w_up[l], ffn1_w_down[l],
                     mix_norm[l], groups_l, ret_norm[l], w_out[l], ffn2_norm[l], ffn2_w_gate[l],
                     ffn2_w_up[l], ffn2_w_down[l], final_norm, last)
    return x2d.reshape(batch, seq, D_MODEL)
```

```python
import functools

import jax
import jax.numpy as jnp
from jax import lax
from jax.experimental import pallas as pl
from jax.experimental.pallas import tpu as pltpu

D_MODEL = 2048
H_RET = 8
DK_RET = 128
DV_RET = 128
RET_CHUNK = 128
H_ATT = 8
H_KV = 2
D_HEAD = 128
H_IDX = 16
D_IDX = 64
MAX_TOPK = 256
Q_BLOCK = 256
D_FF = 5632
ROPE_THETA = 10000.0
NORM_EPS = 1e-6

RET_W = H_RET * DV_RET
ATT_W = H_ATT * D_HEAD
GROUP = H_ATT // H_KV

LANES = 128
SUBLANES = 8
VMEM_LIMIT = 56 * 1024 * 1024

ROPE_COLS = 2 * H_RET * DK_RET + ATT_W + H_KV * D_HEAD
PLAIN_COLS = 2 * RET_W + H_KV * D_HEAD
IDX_Q_COLS = H_IDX * D_IDX
IDX_COLS = IDX_Q_COLS + 2 * LANES
AQ_CHUNK0 = 2 * H_RET * DK_RET // LANES

INT_MIN = -2 ** 31
NEG_BIG = -1e30

F32 = jnp.float32
BF16 = jnp.bfloat16


def _dot(a, b):
    return jnp.dot(a, b, preferred_element_type=F32)


def _dot_nt(a, b):
    return lax.dot_general(a, b, (((1,), (1,)), ((), ())), preferred_element_type=F32)


def _dot_tn(a, b):
    return lax.dot_general(a, b, (((0,), (0,)), ((), ())), preferred_element_type=F32)


def _rmsnorm(xf, g):
    ms = jnp.mean(xf * xf, axis=-1, keepdims=True)
    return xf * lax.rsqrt(ms + NORM_EPS) * g


def _params(sem):
    return pltpu.CompilerParams(dimension_semantics=sem, vmem_limit_bytes=VMEM_LIMIT)


W_IN_SPLITS = (0, 1024, 2048, 3072, 4096, 5120, 5376, 5632, 6656, 6720, 6736)


def _prep_kernel(pos_ref, inv_a_ref, inv_b_ref, sgn_a_ref, sgn_b_ref, w_in_ref,
                 cos_a_ref, sin_a_ref, cos_b_ref, sin_b_ref, w_rope_ref, w_plain_ref, w_idx_ref):
    p = pos_ref[...].astype(F32)
    ang_a = p * inv_a_ref[...]
    ang_b = p * inv_b_ref[...]
    cos_a_ref[...] = jnp.cos(ang_a)
    sin_a_ref[...] = jnp.sin(ang_a) * sgn_a_ref[...]
    cos_b_ref[...] = jnp.cos(ang_b)
    sin_b_ref[...] = jnp.sin(ang_b) * sgn_b_ref[...]

    rq, rk, rv, rg, aq, ak, av, iq, ik, iw, end = W_IN_SPLITS
    for layer in range(w_in_ref.shape[0]):
        feats = lambda lo, hi: w_in_ref[layer, lo:hi, :].astype(BF16)
        w_rope_ref[layer, :rv - rq, :] = feats(rq, rv)
        w_rope_ref[layer, rv - rq:, :] = feats(aq, av)
        w_plain_ref[layer, :aq - rv, :] = feats(rv, aq)
        w_plain_ref[layer, aq - rv:, :] = feats(av, iq)
        w_idx_ref[layer, :ik - iq, :] = feats(iq, ik)
        key = feats(ik, iw)
        w_idx_ref[layer, ik - iq:ik - iq + D_IDX, :] = key
        w_idx_ref[layer, ik - iq + D_IDX:ik - iq + LANES, :] = key
        w_idx_ref[layer, ik - iq + LANES:ik - iq + LANES + H_IDX, :] = feats(iw, end)
        w_idx_ref[layer, ik - iq + LANES + H_IDX:, :] = jnp.zeros(
            (LANES - H_IDX, w_idx_ref.shape[2]), BF16)


def _prepare(positions, w_in_t):
    n_tok = positions.size
    tm = 1024
    n_steps = n_tok // tm
    depth, _, d_model = w_in_t.shape
    w_cols = d_model // n_steps
    lane = jnp.arange(LANES)

    def inv_freq(d):
        inv = ROPE_THETA ** (-jnp.arange(0, d, 2, dtype=F32) / d)
        return inv[lane % (d // 2)][None, :]

    def sign(d):
        return jnp.where(lane % d < d // 2, -1.0, 1.0).astype(F32)[None, :]

    row = pl.BlockSpec((1, LANES), lambda i: (0, 0))
    tab = pl.BlockSpec((tm, LANES), lambda i: (i, 0))
    out = jax.ShapeDtypeStruct((n_tok, LANES), F32)
    slab = lambda n_feats: pl.BlockSpec((depth, n_feats, w_cols), lambda i: (0, 0, i))
    w_out = lambda n_feats: jax.ShapeDtypeStruct((depth, n_feats, d_model), BF16)
    outs = pl.pallas_call(
        _prep_kernel,
        grid=(n_steps,),
        in_specs=[pl.BlockSpec((tm, 1), lambda i: (i, 0)), row, row, row, row, slab(w_in_t.shape[1])],
        out_specs=[tab, tab, tab, tab, slab(ROPE_COLS), slab(PLAIN_COLS), slab(IDX_COLS)],
        out_shape=[out, out, out, out, w_out(ROPE_COLS), w_out(PLAIN_COLS), w_out(IDX_COLS)],
        compiler_params=_params(("parallel",)),
        name="prepare",
    )(positions.reshape(n_tok, 1), inv_freq(D_HEAD), inv_freq(D_IDX), sign(D_HEAD), sign(D_IDX), w_in_t)
    return outs[:4], outs[4:]


FFN_ROW_CHUNK = 128


def _ffn_kernel(x_ref, g_ref, wg_ref, wu_ref, wd_ref, g2_ref, *refs, emit_residual):
    if emit_residual:
        res_ref, normed_ref, xn_ref = refs
        acc_ref = res_ref
    else:
        normed_ref, xn_ref = refs
        acc_ref = normed_ref
    f = pl.program_id(1)
    n_row_chunks = x_ref.shape[0] // FFN_ROW_CHUNK

    def rows_of(r):
        return pl.ds(pl.multiple_of(r * FFN_ROW_CHUNK, FFN_ROW_CHUNK), FFN_ROW_CHUNK)

    @pl.when(f == 0)
    def _():
        def prologue(r, carry):
            xf = x_ref[rows_of(r), :]
            xn_ref[rows_of(r), :] = _rmsnorm(xf, g_ref[...]).astype(BF16)
            acc_ref[rows_of(r), :] = 2.0 * xf
            return carry
        lax.fori_loop(0, n_row_chunks, prologue, 0)

    xn = xn_ref[...]
    a = _dot(xn, wg_ref[...].astype(BF16))
    b = _dot(xn, wu_ref[...].astype(BF16))
    hidden = (a * jax.nn.sigmoid(a) * b).astype(BF16)
    acc_ref[...] += _dot(hidden, wd_ref[...].astype(BF16))

    @pl.when(f == pl.num_programs(1) - 1)
    def _():
        def epilogue(r, carry):
            y = 0.5 * acc_ref[rows_of(r), :]
            if emit_residual:
                res_ref[rows_of(r), :] = y
            normed_ref[rows_of(r), :] = _rmsnorm(y, g2_ref[...]).astype(normed_ref.dtype)
            return carry
        lax.fori_loop(0, n_row_chunks, epilogue, 0)


FFN_TM = 1024
FFN_TF_F32 = 256
FFN_TF_BF16 = 512


def _ffn(x2d, g, wg, wu, wd, g2, *, emit_residual, tf, tm=FFN_TM):
    n_tok = x2d.shape[0]
    tile_map = lambda i, f: (i, 0)
    x_buffers = 2 if wg.dtype == BF16 else 1
    x_tile = pl.BlockSpec((tm, D_MODEL), tile_map, pipeline_mode=pl.Buffered(x_buffers))
    out_tile = pl.BlockSpec((tm, D_MODEL), tile_map)
    row = pl.BlockSpec((1, D_MODEL), lambda i, f: (0, 0))
    if emit_residual:
        out_specs = [out_tile, out_tile]
        out_shape = [jax.ShapeDtypeStruct((n_tok, D_MODEL), F32),
                     jax.ShapeDtypeStruct((n_tok, D_MODEL), BF16)]
    else:
        out_specs = out_tile
        out_shape = jax.ShapeDtypeStruct((n_tok, D_MODEL), F32)
    return pl.pallas_call(
        functools.partial(_ffn_kernel, emit_residual=emit_residual),
        grid=(n_tok // tm, D_FF // tf),
        in_specs=[x_tile, row,
                  pl.BlockSpec((D_MODEL, tf), lambda i, f: (0, f)),
                  pl.BlockSpec((D_MODEL, tf), lambda i, f: (0, f)),
                  pl.BlockSpec((tf, D_MODEL), lambda i, f: (f, 0)),
                  row],
        out_specs=out_specs,
        out_shape=out_shape,
        scratch_shapes=[pltpu.VMEM((tm, D_MODEL), BF16)],
        compiler_params=_params(("parallel", "arbitrary")),
        name="ffn_residual" if emit_residual else "ffn_final",
    )(x2d, g, wg, wu, wd, g2)


def _rope128(y, cos, sin):
    return y * cos + pltpu.roll(y, D_HEAD // 2, axis=1) * sin


def _rope64(y, cos, sin):
    lane = lax.broadcasted_iota(jnp.int32, y.shape, 1)
    first_half = (lane & (D_IDX - 1)) < D_IDX // 2
    partner = jnp.where(first_half,
                        pltpu.roll(y, LANES - D_IDX // 2, axis=1),
                        pltpu.roll(y, D_IDX // 2, axis=1))
    return y * cos + partner * sin


def _proj_kernel(h_ref, w_ref, cos_ref, sin_ref, o_ref, *, mode):
    y = _dot_nt(h_ref[...], w_ref[...])
    n_chunks = y.shape[1] // LANES
    if mode == "plain":
        o_ref[...] = y.astype(o_ref.dtype)
        return
    cos = cos_ref[...]
    sin = sin_ref[...]
    chunk0 = pl.program_id(0) * n_chunks
    for c in range(n_chunks):
        yc = y[:, c * LANES:(c + 1) * LANES]
        if mode == "rope128":
            is_aq = (chunk0 + c >= AQ_CHUNK0) & (chunk0 + c < AQ_CHUNK0 + H_ATT)
            yc = _rope128(yc, cos, sin) * jnp.where(is_aq, LOG2_E * D_HEAD ** -0.5, 1.0)
        elif c < IDX_Q_COLS // LANES:
            yc = _rope64(yc, cos, sin) * (D_IDX ** -0.5)
        elif c == IDX_Q_COLS // LANES:
            yc = _rope64(yc, cos, sin)
        else:
            yc = yc * (H_IDX ** -0.5)
        o_ref[:, c * LANES:(c + 1) * LANES] = yc.astype(o_ref.dtype)


PROJ_TM = 1024


def _proj(h, w, cos, sin, *, mode, tm, tn, out_dtype):
    n_tok, n_cols = h.shape[0], w.shape[0]
    tab = pl.BlockSpec((tm, LANES), lambda j, i: (i, 0))
    return pl.pallas_call(
        functools.partial(_proj_kernel, mode=mode),
        grid=(n_cols // tn, n_tok // tm),
        in_specs=[pl.BlockSpec((tm, D_MODEL), lambda j, i: (i, 0)),
                  pl.BlockSpec((tn, D_MODEL), lambda j, i: (j, 0)),
                  tab, tab],
        out_specs=pl.BlockSpec((tm, tn), lambda j, i: (i, j)),
        out_shape=jax.ShapeDtypeStruct((n_tok, n_cols), out_dtype),
        compiler_params=_params(("parallel", "parallel")),
        name="proj_" + mode,
    )(h, w, cos, sin)


RET_UNROLL = 8


def _retention_kernel(lg_ref, q_ref, k_ref, v_ref, g_ref, rn_ref, o_ref):
    C = RET_CHUNK
    n_chunks = q_ref.shape[0] // C
    lg = lg_ref[...]
    row = lax.broadcasted_iota(jnp.int32, (C, C), 0).astype(F32)
    col = lax.broadcasted_iota(jnp.int32, (C, C), 1).astype(F32)
    diff = row - col
    scale = DK_RET ** -0.5
    decay = jnp.where(diff >= 0, jnp.exp(jnp.maximum(diff, 0.0) * lg), 0.0) * scale
    k_dec = jnp.exp((C - 1 - row) * lg) * scale
    q_dec = jnp.exp((row + 1) * lg)
    g_chunk = jnp.exp(C * lg)
    rn = rn_ref[...]

    def body(n, state):
        sl = pl.ds(pl.multiple_of(n * C, C), C)
        qc = q_ref[sl, :]
        kc = k_ref[sl, :]
        vc = v_ref[sl, :]
        s = _dot_nt(qc, kc) * decay
        intra = _dot(s.astype(BF16), vc)
        cross = _dot(qc, state.astype(BF16)) * q_dec
        kv = _dot_tn((kc.astype(F32) * k_dec).astype(BF16), vc)
        o = intra + cross
        mu = jnp.mean(o, axis=-1, keepdims=True)
        d = o - mu
        var = jnp.mean(d * d, axis=-1, keepdims=True)
        y = d * lax.rsqrt(var + NORM_EPS) * rn
        gate = g_ref[sl, :].astype(F32)
        o_ref[sl, :] = (y * (gate * jax.nn.sigmoid(gate))).astype(o_ref.dtype)
        return state * g_chunk + kv

    lax.fori_loop(0, n_chunks, body, jnp.zeros((DK_RET, DV_RET), F32), unroll=RET_UNROLL)


def _retention(rope_out, plain_out, ret_norm, batch, seq):
    n_tok = batch * seq
    lg = jnp.log1p(-jnp.exp2(-5.0 - jnp.arange(H_RET, dtype=F32)))
    lg = jnp.broadcast_to(lg[:, None, None], (H_RET, 1, LANES))
    head = lambda off: pl.BlockSpec((seq, LANES), lambda b, h: (b, off + h))
    return pl.pallas_call(
        _retention_kernel,
        grid=(batch, H_RET),
        in_specs=[pl.BlockSpec((None, 1, LANES), lambda b, h: (h, 0, 0)),
                  head(0), head(H_RET),
                  head(0), head(H_RET),
                  pl.BlockSpec((1, LANES), lambda b, h: (0, h))],
        out_specs=head(0),
        out_shape=jax.ShapeDtypeStruct((n_tok, RET_W), BF16),
        compiler_params=_params(("parallel", "parallel")),
        name="retention",
    )(lg, rope_out, rope_out, plain_out, plain_out, ret_norm)


KEY_TILE = 512
SCORE_ROWS = 128
FLT_MAX_KEY = 0x00800000
ATT_HEADS_PER_CHAIN = 2
LOG2_E = 1.4426950408889634


def _order_key_to_float(u):
    s = u ^ INT_MIN
    return pltpu.bitcast(s ^ ((s >> 31) & jnp.int32(0x7FFFFFFF)), F32)


def _attn_kernel(iq_ref, ik_ref, iw_ref, aq_ref, ak_ref, av_ref, *rest, top_k, n_cast):
    o_ref, sc_ref = rest[n_cast], rest[-1]
    for w32_ref, w16_ref in zip(rest[:n_cast], rest[n_cast + 1:-1]):
        w16_ref[...] = w32_ref[...].astype(BF16)

    qb = pl.program_id(1)
    t0 = qb * Q_BLOCK
    n_tiles = (t0 + Q_BLOCK + KEY_TILE - 1) // KEY_TILE

    w_t = iw_ref[...].astype(F32).T
    w_rows = [w_t[h:h + 1, :] for h in range(H_IDX)]

    lane = lax.broadcasted_iota(jnp.int32, (Q_BLOCK, LANES), 1)
    low = lane < D_IDX
    q_pairs = []
    for c in range(IDX_Q_COLS // LANES):
        qc = iq_ref[:, c * LANES:(c + 1) * LANES]
        zero = jnp.zeros_like(qc)
        q_pairs.append(jnp.concatenate([jnp.where(low, qc, zero), jnp.where(low, zero, qc)], axis=0))

    q_pos = t0 + lax.broadcasted_iota(jnp.int32, (SCORE_ROWS, Q_BLOCK), 1)
    key_off = lax.broadcasted_iota(jnp.int32, (SCORE_ROWS, Q_BLOCK), 0)

    def score_tile(kt, carry):
        for s in range(KEY_TILE // SCORE_ROWS):
            base = pl.multiple_of(kt * KEY_TILE + s * SCORE_ROWS, SCORE_ROWS)
            kk = ik_ref[pl.ds(base, SCORE_ROWS), :]
            acc = jnp.zeros((SCORE_ROWS, Q_BLOCK), F32)
            for c, qp in enumerate(q_pairs):
                z = _dot_nt(kk, qp)
                acc = acc + jnp.maximum(z[:, :Q_BLOCK], 0.0) * w_rows[2 * c]
                acc = acc + jnp.maximum(z[:, Q_BLOCK:], 0.0) * w_rows[2 * c + 1]
            causal = base + key_off <= q_pos
            sc_ref[pl.ds(base, SCORE_ROWS), :] = jnp.where(causal, acc, -jnp.inf)
        return carry

    lax.fori_loop(0, n_tiles, score_tile, 0)

    def search_bit(it, tau_u):
        cand_u = tau_u | lax.shift_left(jnp.int32(1), 31 - it)
        cand = _order_key_to_float(cand_u)

        def count_tile(kt, cnt):
            base = pl.multiple_of(kt * KEY_TILE, KEY_TILE)
            ge = sc_ref[pl.ds(base, KEY_TILE), :] >= cand
            part = jnp.where(ge, 1.0, 0.0).reshape(KEY_TILE // SUBLANES, SUBLANES, Q_BLOCK)
            while part.shape[0] > 1:
                half = part.shape[0] // 2
                part = part[:half] + part[half:]
            return cnt + part[0]

        cnt = lax.fori_loop(0, n_tiles, count_tile, jnp.zeros((SUBLANES, Q_BLOCK), F32))
        total = jnp.sum(cnt, axis=0, keepdims=True)
        return jnp.where(total >= top_k, cand_u, tau_u)

    searched = t0 + Q_BLOCK > top_k
    n_bits = jnp.where(searched, 32, 0)
    tau_u = lax.fori_loop(0, n_bits, search_bit, jnp.zeros((1, Q_BLOCK), jnp.int32))
    tau = _order_key_to_float(jnp.where(searched, tau_u, FLT_MAX_KEY))

    n_chains = H_ATT // ATT_HEADS_PER_CHAIN
    width = ATT_HEADS_PER_CHAIN * Q_BLOCK
    q_chains = []
    for c in range(n_chains):
        heads = range(c * ATT_HEADS_PER_CHAIN, (c + 1) * ATT_HEADS_PER_CHAIN)
        q_chains.append(jnp.concatenate([aq_ref[:, h * D_HEAD:(h + 1) * D_HEAD] for h in heads], axis=0))

    def attend_tile(kt, carry):
        base = pl.multiple_of(kt * KEY_TILE, KEY_TILE)
        bias = jnp.where(sc_ref[pl.ds(base, KEY_TILE), :] >= tau, 0.0, NEG_BIG)
        bias = jnp.concatenate([bias] * ATT_HEADS_PER_CHAIN, axis=1)
        kv_of = lambda c: c * ATT_HEADS_PER_CHAIN // GROUP
        k_tiles = [ak_ref[pl.ds(base, KEY_TILE), g * D_HEAD:(g + 1) * D_HEAD] for g in range(H_KV)]
        v_tiles = [av_ref[pl.ds(base, KEY_TILE), g * D_HEAD:(g + 1) * D_HEAD] for g in range(H_KV)]
        logits = [_dot_nt(k_tiles[kv_of(c)], q_chains[c]) + bias for c in range(n_chains)]
        stats = []
        for c in range(n_chains):
            m_old, l_old, _ = carry[c]
            m_new = jnp.maximum(m_old, jnp.max(logits[c], axis=0, keepdims=True))
            alpha = jnp.exp2(m_old - m_new)
            p = jnp.exp2(logits[c] - m_new)
            l_new = alpha * l_old + jnp.sum(p, axis=0, keepdims=True)
            stats.append((m_new, l_new, alpha, p.astype(BF16)))
        new = []
        for c in range(n_chains):
            m_new, l_new, alpha, p = stats[c]
            acc_new = alpha * carry[c][2] + _dot_tn(v_tiles[kv_of(c)], p)
            new.append((m_new, l_new, acc_new))
        return tuple(new)

    init = tuple((jnp.full((1, width), NEG_BIG, F32), jnp.zeros((1, width), F32),
                  jnp.zeros((D_HEAD, width), F32)) for _ in range(n_chains))
    final = lax.fori_loop(0, n_tiles, attend_tile, init)
    for c in range(n_chains):
        _, l_fin, acc_fin = final[c]
        out_t = acc_fin / l_fin
        for j in range(ATT_HEADS_PER_CHAIN):
            hcol = (c * ATT_HEADS_PER_CHAIN + j) * D_HEAD
            o_ref[:, hcol:hcol + D_HEAD] = out_t[:, j * Q_BLOCK:(j + 1) * Q_BLOCK].T.astype(o_ref.dtype)


def _sparse_attention(idx_out, rope_out, plain_out, batch, seq, weights_to_cast):
    n_tok = batch * seq
    nb = seq // Q_BLOCK
    n_steps = batch * nb
    top_k = min(MAX_TOPK, seq // 4)
    kv_w = H_KV * D_HEAD
    step = lambda b, q: (b * nb + q, 0)
    slabs = [pl.BlockSpec((w.shape[0] // n_steps, w.shape[1]), step) for w in weights_to_cast]
    outs = pl.pallas_call(
        functools.partial(_attn_kernel, top_k=top_k, n_cast=len(weights_to_cast)),
        grid=(batch, nb),
        in_specs=[
            pl.BlockSpec((Q_BLOCK, IDX_Q_COLS), step),
            pl.BlockSpec((seq, LANES), lambda b, q: (b, IDX_Q_COLS // LANES)),
            pl.BlockSpec((Q_BLOCK, LANES), lambda b, q: (b * nb + q, IDX_Q_COLS // LANES + 1)),
            pl.BlockSpec((Q_BLOCK, ATT_W), lambda b, q: (b * nb + q, 2 * RET_W // ATT_W)),
            pl.BlockSpec((seq, kv_w), lambda b, q: (b, (2 * RET_W + ATT_W) // kv_w)),
            pl.BlockSpec((seq, kv_w), lambda b, q: (b, 2 * RET_W // kv_w)),
        ] + slabs,
        out_specs=[pl.BlockSpec((Q_BLOCK, ATT_W), step)] + slabs,
        out_shape=[jax.ShapeDtypeStruct((n_tok, ATT_W), BF16)]
                  + [jax.ShapeDtypeStruct(w.shape, BF16) for w in weights_to_cast],
        scratch_shapes=[pltpu.VMEM((seq, Q_BLOCK), F32)],
        compiler_params=_params(("parallel", "arbitrary")),
        name="sparse_attention",
    )(idx_out, idx_out, idx_out, rope_out, rope_out, plain_out, *weights_to_cast)
    return outs[0], outs[1:]


def _out_proj_kernel(x_ref, ro_ref, ao_ref, wr_ref, wa_ref, o_ref):
    o_ref[...] = x_ref[...] + _dot(ro_ref[...], wr_ref[...]) + _dot(ao_ref[...], wa_ref[...])


def _out_proj(x2d, ro, ao, w_out, tm=512):
    n_tok = x2d.shape[0]
    tile = pl.BlockSpec((tm, D_MODEL), lambda i: (i, 0))
    half = pl.BlockSpec((tm, RET_W), lambda i: (i, 0))
    w_ret = pl.BlockSpec((RET_W, D_MODEL), lambda i: (0, 0))
    w_att = pl.BlockSpec((ATT_W, D_MODEL), lambda i: (RET_W // ATT_W, 0))
    return pl.pallas_call(
        _out_proj_kernel,
        grid=(n_tok // tm,),
        in_specs=[tile, half, half, w_ret, w_att],
        out_specs=tile,
        out_shape=jax.ShapeDtypeStruct((n_tok, D_MODEL), F32),
        compiler_params=_params(("parallel",)),
        name="out_proj",
    )(x2d, ro, ao, w_out, w_out)


def _layer(x2d, tables, batch, seq, ffn1_norm, ffn1_w_gate, ffn1_w_up, ffn1_w_down, mix_norm,
           w_in_groups, ret_norm, w_out, ffn2_norm, ffn2_w_gate, ffn2_w_up, ffn2_w_down, final_norm, last):
    cos_a, sin_a, cos_b, sin_b = tables
    w_rope, w_plain, w_idx = w_in_groups
    row = lambda g: g.reshape(1, -1).astype(F32)

    x1, h = _ffn(x2d, row(ffn1_norm), ffn1_w_gate, ffn1_w_up, ffn1_w_down,
                 row(mix_norm), emit_residual=True, tf=FFN_TF_F32)
    rope_out = _proj(h, w_rope, cos_a, sin_a, mode="rope128", tm=PROJ_TM, tn=ROPE_COLS // 2, out_dtype=BF16)
    plain_out = _proj(h, w_plain, cos_a, sin_a, mode="plain", tm=PROJ_TM, tn=PLAIN_COLS // 2, out_dtype=BF16)
    idx_out = _proj(h, w_idx, cos_b, sin_b, mode="idx", tm=PROJ_TM, tn=IDX_COLS, out_dtype=BF16)
    ro = _retention(rope_out, plain_out, row(ret_norm), batch, seq)
    ao, (w_out16, wg16, wu16, wd16) = _sparse_attention(
        idx_out, rope_out, plain_out, batch, seq, [w_out, ffn2_w_gate, ffn2_w_up, ffn2_w_down])
    x2 = _out_proj(x1, ro, ao, w_out16)
    out = _ffn(x2, row(ffn2_norm), wg16, wu16, wd16, row(final_norm), emit_residual=not last,
               tf=FFN_TF_BF16)
    return out if last else out[0]


def kernel(x, positions, ffn1_norm, ffn1_w_gate, ffn1_w_up, ffn1_w_down, mix_norm, w_in, ret_norm,
           w_out, ffn2_norm, ffn2_w_gate, ffn2_w_up, ffn2_w_down, final_norm):
    batch, seq, _ = x.shape
    depth = w_in.shape[0]
    tables, w_in_groups = _prepare(positions, jnp.swapaxes(w_in, 1, 2))
    x2d = x.reshape(batch * seq, D_MODEL)
    for l in range(depth):
        last = l == depth - 1
        groups_l = [w[l] for w in w_in_groups]
        x2d = _layer(x2d, tables, batch, seq, ffn1_norm[l], ffn1_w_gate[l], ffn1_w_up[l], ffn1_w_down[l],
                     mix_norm[l], groups_l, ret_norm[l], w_out[l], ffn2_norm[l], ffn2_w_gate[l],
                     ffn2_w_up[l], ffn2_w_down[l], final_norm, last)
    return x2d.reshape(batch, seq, D_MODEL)
```

```python
import functools

import jax
import jax.numpy as jnp
from jax import lax
from jax.experimental import pallas as pl
from jax.experimental.pallas import tpu as pltpu

D_MODEL = 2048
H_RET = 8
DK_RET = 128
DV_RET = 128
RET_CHUNK = 128
H_ATT = 8
H_KV = 2
D_HEAD = 128
H_IDX = 16
D_IDX = 64
MAX_TOPK = 256
Q_BLOCK = 256
D_FF = 5632
ROPE_THETA = 10000.0
NORM_EPS = 1e-6

RET_W = H_RET * DV_RET
ATT_W = H_ATT * D_HEAD
GROUP = H_ATT // H_KV

LANES = 128
SUBLANES = 8
VMEM_LIMIT = 56 * 1024 * 1024

ROPE_COLS = 2 * H_RET * DK_RET + ATT_W + H_KV * D_HEAD
PLAIN_COLS = 2 * RET_W + H_KV * D_HEAD
IDX_Q_COLS = H_IDX * D_IDX
IDX_COLS = IDX_Q_COLS + 2 * LANES
AQ_CHUNK0 = 2 * H_RET * DK_RET // LANES

INT_MIN = -2 ** 31
NEG_BIG = -1e30

F32 = jnp.float32
BF16 = jnp.bfloat16


def _dot(a, b):
    return jnp.dot(a, b, preferred_element_type=F32)


def _dot_nt(a, b):
    return lax.dot_general(a, b, (((1,), (1,)), ((), ())), preferred_element_type=F32)


def _dot_tn(a, b):
    return lax.dot_general(a, b, (((0,), (0,)), ((), ())), preferred_element_type=F32)


def _rmsnorm(xf, g):
    ms = jnp.mean(xf * xf, axis=-1, keepdims=True)
    return xf * lax.rsqrt(ms + NORM_EPS) * g


def _params(sem):
    return pltpu.CompilerParams(dimension_semantics=sem, vmem_limit_bytes=VMEM_LIMIT)


W_IN_SPLITS = (0, 1024, 2048, 3072, 4096, 5120, 5376, 5632, 6656, 6720, 6736)


def _prep_kernel(pos_ref, inv_ref, sgn_a_ref, sgn_b_ref, w_in_ref,
                 cos_a_ref, sin_a_ref, cos_b_ref, sin_b_ref, w_rope_ref, w_plain_ref, w_idx_ref):
    ang = pos_ref[...].astype(F32) * inv_ref[...]
    lane = lax.broadcasted_iota(jnp.int32, ang.shape, 1)
    half, quarter = D_HEAD // 2, D_IDX // 2

    def table_a(t):
        return jnp.where(lane < half, t, pltpu.roll(t, half, axis=1))

    def table_b(t):
        upper = jnp.where(lane < half + quarter, t, pltpu.roll(t, quarter, axis=1))
        return jnp.where(lane >= half, upper, pltpu.roll(upper, half, axis=1))

    cos, sin = jnp.cos(ang), jnp.sin(ang)
    cos_a_ref[...] = table_a(cos)
    sin_a_ref[...] = table_a(sin) * sgn_a_ref[...]
    cos_b_ref[...] = table_b(cos)
    sin_b_ref[...] = table_b(sin) * sgn_b_ref[...]

    rq, rk, rv, rg, aq, ak, av, iq, ik, iw, end = W_IN_SPLITS
    for layer in range(w_in_ref.shape[0]):
        feats = lambda lo, hi: w_in_ref[layer, lo:hi, :].astype(BF16)
        w_rope_ref[layer, :rv - rq, :] = feats(rq, rv)
        w_rope_ref[layer, rv - rq:, :] = feats(aq, av)
        w_plain_ref[layer, :aq - rv, :] = feats(rv, aq)
        w_plain_ref[layer, aq - rv:, :] = feats(av, iq)
        w_idx_ref[layer, :ik - iq, :] = feats(iq, ik)
        key = feats(ik, iw)
        w_idx_ref[layer, ik - iq:ik - iq + D_IDX, :] = key
        w_idx_ref[layer, ik - iq + D_IDX:ik - iq + LANES, :] = key
        w_idx_ref[layer, ik - iq + LANES:ik - iq + LANES + H_IDX, :] = feats(iw, end)
        w_idx_ref[layer, ik - iq + LANES + H_IDX:, :] = jnp.zeros(
            (LANES - H_IDX, w_idx_ref.shape[2]), BF16)


def _prepare(positions, w_in_t):
    n_tok = positions.size
    tm = 1024
    n_steps = n_tok // tm
    depth, _, d_model = w_in_t.shape
    w_cols = d_model // n_steps
    lane = jnp.arange(LANES)

    def inv_freq(d):
        return ROPE_THETA ** (-jnp.arange(0, d, 2, dtype=F32) / d)

    unused = jnp.zeros((LANES - D_HEAD // 2 - D_IDX // 2,), F32)
    inv = jnp.concatenate([inv_freq(D_HEAD), inv_freq(D_IDX), unused])[None, :]

    def sign(d):
        return jnp.where(lane % d < d // 2, -1.0, 1.0).astype(F32)[None, :]

    row = pl.BlockSpec((1, LANES), lambda i: (0, 0))
    tab = pl.BlockSpec((tm, LANES), lambda i: (i, 0))
    out = jax.ShapeDtypeStruct((n_tok, LANES), F32)
    slab = lambda n_feats: pl.BlockSpec((depth, n_feats, w_cols), lambda i: (0, 0, i))
    w_out = lambda n_feats: jax.ShapeDtypeStruct((depth, n_feats, d_model), BF16)
    outs = pl.pallas_call(
        _prep_kernel,
        grid=(n_steps,),
        in_specs=[pl.BlockSpec((tm, 1), lambda i: (i, 0)), row, row, row, slab(w_in_t.shape[1])],
        out_specs=[tab, tab, tab, tab, slab(ROPE_COLS), slab(PLAIN_COLS), slab(IDX_COLS)],
        out_shape=[out, out, out, out, w_out(ROPE_COLS), w_out(PLAIN_COLS), w_out(IDX_COLS)],
        compiler_params=_params(("parallel",)),
        name="prepare",
    )(positions.reshape(n_tok, 1), inv, sign(D_HEAD), sign(D_IDX), w_in_t)
    return outs[:4], outs[4:]


FFN_ROW_CHUNK = 128


def _ffn_kernel(x_ref, g_ref, wg_ref, wu_ref, wd_ref, g2_ref, *refs, emit_residual):
    if emit_residual:
        res_ref, normed_ref, xn_ref = refs
        acc_ref = res_ref
    else:
        normed_ref, xn_ref = refs
        acc_ref = normed_ref
    f = pl.program_id(1)
    n_row_chunks = x_ref.shape[0] // FFN_ROW_CHUNK

    def rows_of(r):
        return pl.ds(pl.multiple_of(r * FFN_ROW_CHUNK, FFN_ROW_CHUNK), FFN_ROW_CHUNK)

    @pl.when(f == 0)
    def _():
        def prologue(r, carry):
            xf = x_ref[rows_of(r), :]
            xn_ref[rows_of(r), :] = _rmsnorm(xf, g_ref[...]).astype(BF16)
            acc_ref[rows_of(r), :] = 2.0 * xf
            return carry
        lax.fori_loop(0, n_row_chunks, prologue, 0)

    xn = xn_ref[...]
    a = _dot(xn, wg_ref[...].astype(BF16))
    b = _dot(xn, wu_ref[...].astype(BF16))
    hidden = (a * jax.nn.sigmoid(a) * b).astype(BF16)
    acc_ref[...] += _dot(hidden, wd_ref[...].astype(BF16))

    @pl.when(f == pl.num_programs(1) - 1)
    def _():
        def epilogue(r, carry):
            y = 0.5 * acc_ref[rows_of(r), :]
            if emit_residual:
                res_ref[rows_of(r), :] = y
            normed_ref[rows_of(r), :] = _rmsnorm(y, g2_ref[...]).astype(normed_ref.dtype)
            return carry
        lax.fori_loop(0, n_row_chunks, epilogue, 0)


FFN_TM = 1024
FFN_TF_F32 = 256
FFN_TF_BF16 = 512


def _ffn(x2d, g, wg, wu, wd, g2, *, emit_residual, tf, tm=FFN_TM):
    n_tok = x2d.shape[0]
    tile_map = lambda i, f: (i, 0)
    x_tile = pl.BlockSpec((tm, D_MODEL), tile_map)
    out_tile = pl.BlockSpec((tm, D_MODEL), tile_map)
    normed_tile = out_tile if wg.dtype == BF16 else pl.BlockSpec(
        (tm, D_MODEL), tile_map, pipeline_mode=pl.Buffered(1))
    row = pl.BlockSpec((1, D_MODEL), lambda i, f: (0, 0))
    if emit_residual:
        out_specs = [out_tile, normed_tile]
        out_shape = [jax.ShapeDtypeStruct((n_tok, D_MODEL), F32),
                     jax.ShapeDtypeStruct((n_tok, D_MODEL), BF16)]
    else:
        out_specs = out_tile
        out_shape = jax.ShapeDtypeStruct((n_tok, D_MODEL), F32)
    return pl.pallas_call(
        functools.partial(_ffn_kernel, emit_residual=emit_residual),
        grid=(n_tok // tm, D_FF // tf),
        in_specs=[x_tile, row,
                  pl.BlockSpec((D_MODEL, tf), lambda i, f: (0, f)),
                  pl.BlockSpec((D_MODEL, tf), lambda i, f: (0, f)),
                  pl.BlockSpec((tf, D_MODEL), lambda i, f: (f, 0)),
                  row],
        out_specs=out_specs,
        out_shape=out_shape,
        scratch_shapes=[pltpu.VMEM((tm, D_MODEL), BF16)],
        compiler_params=_params(("parallel", "arbitrary")),
        name="ffn_residual" if emit_residual else "ffn_final",
    )(x2d, g, wg, wu, wd, g2)


def _rope128(y, cos, sin):
    return y * cos + pltpu.roll(y, D_HEAD // 2, axis=1) * sin


def _rope64(y, cos, sin):
    lane = lax.broadcasted_iota(jnp.int32, y.shape, 1)
    first_half = (lane & (D_IDX - 1)) < D_IDX // 2
    partner = jnp.where(first_half,
                        pltpu.roll(y, LANES - D_IDX // 2, axis=1),
                        pltpu.roll(y, D_IDX // 2, axis=1))
    return y * cos + partner * sin


def _proj_kernel(h_ref, w_ref, cos_ref, sin_ref, o_ref, *, mode):
    y = _dot_nt(h_ref[...], w_ref[...])
    n_chunks = y.shape[1] // LANES
    if mode == "plain":
        o_ref[...] = y.astype(o_ref.dtype)
        return
    cos = cos_ref[...]
    sin = sin_ref[...]
    for c in range(n_chunks):
        yc = y[:, c * LANES:(c + 1) * LANES]
        if mode == "rope128":
            yc = _rope128(yc, cos, sin)
            if AQ_CHUNK0 <= c < AQ_CHUNK0 + H_ATT:
                yc = yc * (LOG2_E * D_HEAD ** -0.5)
        elif c < IDX_Q_COLS // LANES:
            yc = _rope64(yc, cos, sin) * (D_IDX ** -0.5)
        elif c == IDX_Q_COLS // LANES:
            yc = _rope64(yc, cos, sin)
        else:
            yc = yc * (H_IDX ** -0.5)
        o_ref[:, c * LANES:(c + 1) * LANES] = yc.astype(o_ref.dtype)


def _proj(h, w, cos, sin, *, mode, tm, out_dtype):
    n_tok, n_cols = h.shape[0], w.shape[0]
    tab = pl.BlockSpec((tm, LANES), lambda i: (i, 0))
    return pl.pallas_call(
        functools.partial(_proj_kernel, mode=mode),
        grid=(n_tok // tm,),
        in_specs=[pl.BlockSpec((tm, D_MODEL), lambda i: (i, 0)),
                  pl.BlockSpec((n_cols, D_MODEL), lambda i: (0, 0), pipeline_mode=pl.Buffered(1)),
                  tab, tab],
        out_specs=pl.BlockSpec((tm, n_cols), lambda i: (i, 0)),
        out_shape=jax.ShapeDtypeStruct((n_tok, n_cols), out_dtype),
        compiler_params=_params(("parallel",)),
        name="proj_" + mode,
    )(h, w, cos, sin)


RET_UNROLL = 8


def _retention_kernel(lg_ref, q_ref, k_ref, v_ref, g_ref, rn_ref, o_ref):
    C = RET_CHUNK
    n_chunks = q_ref.shape[0] // C
    lg = lg_ref[...]
    row = lax.broadcasted_iota(jnp.int32, (C, C), 0).astype(F32)
    col = lax.broadcasted_iota(jnp.int32, (C, C), 1).astype(F32)
    diff = row - col
    scale = DK_RET ** -0.5
    decay = jnp.where(diff >= 0, jnp.exp(jnp.maximum(diff, 0.0) * lg), 0.0) * scale
    k_dec = jnp.exp((C - 1 - row) * lg) * scale
    q_dec = jnp.exp((row + 1) * lg)
    g_chunk = jnp.exp(C * lg)
    rn = rn_ref[...]

    def body(n, state):
        sl = pl.ds(pl.multiple_of(n * C, C), C)
        qc = q_ref[sl, :]
        kc = k_ref[sl, :]
        vc = v_ref[sl, :]
        s = _dot_nt(qc, kc) * decay
        intra = _dot(s.astype(BF16), vc)
        cross = _dot(qc, state.astype(BF16)) * q_dec
        kv = _dot_tn((kc.astype(F32) * k_dec).astype(BF16), vc)
        o = intra + cross
        mu = jnp.mean(o, axis=-1, keepdims=True)
        d = o - mu
        var = jnp.mean(d * d, axis=-1, keepdims=True)
        y = d * lax.rsqrt(var + NORM_EPS) * rn
        gate = g_ref[sl, :].astype(F32)
        o_ref[sl, :] = (y * (gate * jax.nn.sigmoid(gate))).astype(o_ref.dtype)
        return state * g_chunk + kv

    lax.fori_loop(0, n_chunks, body, jnp.zeros((DK_RET, DV_RET), F32), unroll=RET_UNROLL)


def _retention(rope_out, plain_out, ret_norm, batch, seq):
    n_tok = batch * seq
    lg = jnp.log1p(-jnp.exp2(-5.0 - jnp.arange(H_RET, dtype=F32)))
    lg = jnp.broadcast_to(lg[:, None, None], (H_RET, 1, LANES))
    head = lambda off: pl.BlockSpec((seq, LANES), lambda b, h: (b, off + h))
    return pl.pallas_call(
        _retention_kernel,
        grid=(batch, H_RET),
        in_specs=[pl.BlockSpec((None, 1, LANES), lambda b, h: (h, 0, 0)),
                  head(0), head(H_RET),
                  head(0), head(H_RET),
                  pl.BlockSpec((1, LANES), lambda b, h: (0, h))],
        out_specs=head(0),
        out_shape=jax.ShapeDtypeStruct((n_tok, RET_W), BF16),
        compiler_params=_params(("parallel", "parallel")),
        name="retention",
    )(lg, rope_out, rope_out, plain_out, plain_out, ret_norm)


KEY_TILE = 512
SCORE_ROWS = 128
FLT_MAX_KEY = 0x00800000
KEY_BITS = 32
ATT_HEADS_PER_CHAIN = 2
LOG2_E = 1.4426950408889634


def _order_key_to_float(u):
    s = u ^ INT_MIN
    return pltpu.bitcast(s ^ ((s >> 31) & jnp.int32(0x7FFFFFFF)), F32)


def _attn_kernel(iq_ref, ik_ref, iw_ref, aq_ref, ak_ref, av_ref, *rest, top_k, n_cast):
    o_ref, sc_ref, sc16_ref = rest[n_cast], rest[-2], rest[-1]
    for w32_ref, w16_ref in zip(rest[:n_cast], rest[n_cast + 1:-2]):
        w16_ref[...] = w32_ref[...].astype(BF16)

    qb = pl.program_id(1)
    t0 = qb * Q_BLOCK
    n_tiles = (t0 + Q_BLOCK + KEY_TILE - 1) // KEY_TILE

    w_t = iw_ref[...].astype(F32).T
    w_rows = [w_t[h:h + 1, :] for h in range(H_IDX)]

    lane = lax.broadcasted_iota(jnp.int32, (Q_BLOCK, LANES), 1)
    low = lane < D_IDX
    q_pairs = []
    for c in range(IDX_Q_COLS // LANES):
        qc = iq_ref[:, c * LANES:(c + 1) * LANES]
        zero = jnp.zeros_like(qc)
        q_pairs.append(jnp.concatenate([jnp.where(low, qc, zero), jnp.where(low, zero, qc)], axis=0))

    q_pos = t0 + lax.broadcasted_iota(jnp.int32, (SCORE_ROWS, Q_BLOCK), 1)
    key_off = lax.broadcasted_iota(jnp.int32, (SCORE_ROWS, Q_BLOCK), 0)

    def score_tile(kt, carry):
        for s in range(KEY_TILE // SCORE_ROWS):
            base = pl.multiple_of(kt * KEY_TILE + s * SCORE_ROWS, SCORE_ROWS)
            kk = ik_ref[pl.ds(base, SCORE_ROWS), :]
            acc = jnp.zeros((SCORE_ROWS, Q_BLOCK), F32)
            for c, qp in enumerate(q_pairs):
                z = _dot_nt(kk, qp)
                acc = acc + jnp.maximum(z[:, :Q_BLOCK], 0.0) * w_rows[2 * c]
                acc = acc + jnp.maximum(z[:, Q_BLOCK:], 0.0) * w_rows[2 * c + 1]
            causal = base + key_off <= q_pos
            score = jnp.where(causal, acc, -jnp.inf)
            sc_ref[pl.ds(base, SCORE_ROWS), :] = score
            sc16_ref[pl.ds(base, SCORE_ROWS), :] = score.astype(BF16)
        return carry

    lax.fori_loop(0, n_tiles, score_tile, 0)

    def count_ge(ref, cand):
        rows = SUBLANES * 4 // ref.dtype.itemsize
        one, zero = jnp.ones((), ref.dtype), jnp.zeros((), ref.dtype)

        def count_tile(kt, cnt):
            base = pl.multiple_of(kt * KEY_TILE, KEY_TILE)
            ge = ref[pl.ds(base, KEY_TILE), :] >= cand
            part = jnp.where(ge, one, zero).reshape(KEY_TILE // rows, rows, Q_BLOCK)
            while part.shape[0] > 1:
                half = part.shape[0] // 2
                part = part[:half] + part[half:]
            return cnt + part[0].astype(F32)

        cnt = lax.fori_loop(0, n_tiles, count_tile, jnp.zeros((rows, Q_BLOCK), F32))
        return jnp.sum(cnt, axis=0, keepdims=True)

    def search_bits(ref, tau_u, first_bit, n_bits):
        def search_bit(it, tau_u):
            cand_u = tau_u | lax.shift_left(jnp.int32(1), first_bit - it)
            cand = _order_key_to_float(cand_u).astype(ref.dtype)
            return jnp.where(count_ge(ref, cand) >= top_k, cand_u, tau_u)
        return lax.fori_loop(0, n_bits, search_bit, tau_u)

    searched = t0 + Q_BLOCK > top_k
    half_bits = jnp.where(searched, KEY_BITS // 2, 0)
    rounded = search_bits(sc16_ref, jnp.zeros((1, Q_BLOCK), jnp.int32), KEY_BITS - 1, half_bits)
    fits = count_ge(sc_ref, _order_key_to_float(rounded)) >= top_k
    prefix = jnp.where(fits, rounded, rounded - (1 << KEY_BITS // 2))
    tau_u = search_bits(sc_ref, prefix, KEY_BITS // 2 - 1, half_bits)
    tau = _order_key_to_float(jnp.where(searched, tau_u, FLT_MAX_KEY))

    n_chains = H_ATT // ATT_HEADS_PER_CHAIN
    width = ATT_HEADS_PER_CHAIN * Q_BLOCK
    q_chains = []
    for c in range(n_chains):
        heads = range(c * ATT_HEADS_PER_CHAIN, (c + 1) * ATT_HEADS_PER_CHAIN)
        q_chains.append(jnp.concatenate([aq_ref[:, h * D_HEAD:(h + 1) * D_HEAD] for h in heads], axis=0))

    def attend_tile(kt, carry):
        base = pl.multiple_of(kt * KEY_TILE, KEY_TILE)
        bias = jnp.where(sc_ref[pl.ds(base, KEY_TILE), :] >= tau, 0.0, NEG_BIG)
        bias = jnp.concatenate([bias] * ATT_HEADS_PER_CHAIN, axis=1)
        kv_of = lambda c: c * ATT_HEADS_PER_CHAIN // GROUP
        k_tiles = [ak_ref[pl.ds(base, KEY_TILE), g * D_HEAD:(g + 1) * D_HEAD] for g in range(H_KV)]
        v_tiles = [av_ref[pl.ds(base, KEY_TILE), g * D_HEAD:(g + 1) * D_HEAD] for g in range(H_KV)]
        logits = [_dot_nt(k_tiles[kv_of(c)], q_chains[c]) + bias for c in range(n_chains)]
        stats = []
        for c in range(n_chains):
            m_old, l_old, _ = carry[c]
            m_new = jnp.maximum(m_old, jnp.max(logits[c], axis=0, keepdims=True))
            alpha = jnp.exp2(m_old - m_new)
            p = jnp.exp2(logits[c] - m_new)
            l_new = alpha * l_old + jnp.sum(p, axis=0, keepdims=True)
            stats.append((m_new, l_new, alpha, p.astype(BF16)))
        new = []
        for c in range(n_chains):
            m_new, l_new, alpha, p = stats[c]
            acc_new = alpha * carry[c][2] + _dot_tn(v_tiles[kv_of(c)], p)
            new.append((m_new, l_new, acc_new))
        return tuple(new)

    init = tuple((jnp.full((1, width), NEG_BIG, F32), jnp.zeros((1, width), F32),
                  jnp.zeros((D_HEAD, width), F32)) for _ in range(n_chains))
    final = lax.fori_loop(0, n_tiles, attend_tile, init)
    for c in range(n_chains):
        _, l_fin, acc_fin = final[c]
        out_t = acc_fin / l_fin
        for j in range(ATT_HEADS_PER_CHAIN):
            hcol = (c * ATT_HEADS_PER_CHAIN + j) * D_HEAD
            o_ref[:, hcol:hcol + D_HEAD] = out_t[:, j * Q_BLOCK:(j + 1) * Q_BLOCK].T.astype(o_ref.dtype)


def _sparse_attention(idx_out, rope_out, plain_out, batch, seq, weights_to_cast):
    n_tok = batch * seq
    nb = seq // Q_BLOCK
    n_steps = batch * nb
    top_k = min(MAX_TOPK, seq // 4)
    kv_w = H_KV * D_HEAD
    step = lambda b, q: (b * nb + q, 0)
    slabs = [pl.BlockSpec((w.shape[0] // n_steps, w.shape[1]), step) for w in weights_to_cast]
    outs = pl.pallas_call(
        functools.partial(_attn_kernel, top_k=top_k, n_cast=len(weights_to_cast)),
        grid=(batch, nb),
        in_specs=[
            pl.BlockSpec((Q_BLOCK, IDX_Q_COLS), step),
            pl.BlockSpec((seq, LANES), lambda b, q: (b, IDX_Q_COLS // LANES)),
            pl.BlockSpec((Q_BLOCK, LANES), lambda b, q: (b * nb + q, IDX_Q_COLS // LANES + 1)),
            pl.BlockSpec((Q_BLOCK, ATT_W), lambda b, q: (b * nb + q, 2 * RET_W // ATT_W)),
            pl.BlockSpec((seq, kv_w), lambda b, q: (b, (2 * RET_W + ATT_W) // kv_w)),
            pl.BlockSpec((seq, kv_w), lambda b, q: (b, 2 * RET_W // kv_w)),
        ] + slabs,
        out_specs=[pl.BlockSpec((Q_BLOCK, ATT_W), step)] + slabs,
        out_shape=[jax.ShapeDtypeStruct((n_tok, ATT_W), BF16)]
                  + [jax.ShapeDtypeStruct(w.shape, BF16) for w in weights_to_cast],
        scratch_shapes=[pltpu.VMEM((seq, Q_BLOCK), F32), pltpu.VMEM((seq, Q_BLOCK), BF16)],
        compiler_params=_params(("parallel", "arbitrary")),
        name="sparse_attention",
    )(idx_out, idx_out, idx_out, rope_out, rope_out, plain_out, *weights_to_cast)
    return outs[0], outs[1:]


def _out_proj_kernel(x_ref, ro_ref, ao_ref, wr_ref, wa_ref, o_ref):
    o_ref[...] = x_ref[...] + _dot(ro_ref[...], wr_ref[...]) + _dot(ao_ref[...], wa_ref[...])


def _out_proj(x2d, ro, ao, w_out, tm=512):
    n_tok = x2d.shape[0]
    tile = pl.BlockSpec((tm, D_MODEL), lambda i: (i, 0))
    half = pl.BlockSpec((tm, RET_W), lambda i: (i, 0))
    w_ret = pl.BlockSpec((RET_W, D_MODEL), lambda i: (0, 0))
    w_att = pl.BlockSpec((ATT_W, D_MODEL), lambda i: (RET_W // ATT_W, 0))
    return pl.pallas_call(
        _out_proj_kernel,
        grid=(n_tok // tm,),
        in_specs=[tile, half, half, w_ret, w_att],
        out_specs=tile,
        out_shape=jax.ShapeDtypeStruct((n_tok, D_MODEL), F32),
        compiler_params=_params(("parallel",)),
        name="out_proj",
    )(x2d, ro, ao, w_out, w_out)


def _layer(x2d, tables, batch, seq, ffn1_norm, ffn1_w_gate, ffn1_w_up, ffn1_w_down, mix_norm,
           w_in_groups, ret_norm, w_out, ffn2_norm, ffn2_w_gate, ffn2_w_up, ffn2_w_down, final_norm, last):
    cos_a, sin_a, cos_b, sin_b = tables
    w_rope, w_plain, w_idx = w_in_groups
    row = lambda g: g.reshape(1, -1).astype(F32)

    x1, h = _ffn(x2d, row(ffn1_norm), ffn1_w_gate, ffn1_w_up, ffn1_w_down,
                 row(mix_norm), emit_residual=True, tf=FFN_TF_F32)
    rope_out = _proj(h, w_rope, cos_a, sin_a, mode="rope128", tm=512, out_dtype=BF16)
    plain_out = _proj(h, w_plain, cos_a, sin_a, mode="plain", tm=512, out_dtype=BF16)
    idx_out = _proj(h, w_idx, cos_b, sin_b, mode="idx", tm=1024, out_dtype=BF16)
    ro = _retention(rope_out, plain_out, row(ret_norm), batch, seq)
    ao, (w_out16, wg16, wu16, wd16) = _sparse_attention(
        idx_out, rope_out, plain_out, batch, seq, [w_out, ffn2_w_gate, ffn2_w_up, ffn2_w_down])
    x2 = _out_proj(x1, ro, ao, w_out16)
    out = _ffn(x2, row(ffn2_norm), wg16, wu16, wd16, row(final_norm), emit_residual=not last,
               tf=FFN_TF_BF16)
    return out if last else out[0]


def kernel(x, positions, ffn1_norm, ffn1_w_gate, ffn1_w_up, ffn1_w_down, mix_norm, w_in, ret_norm,
           w_out, ffn2_norm, ffn2_w_gate, ffn2_w_up, ffn2_w_down, final_norm):
    batch, seq, _ = x.shape
    depth = w_in.shape[0]
    tables, w_in_groups = _prepare(positions, jnp.swapaxes(w_in, 1, 2))
    x2d = x.reshape(batch * seq, D_MODEL)
    for l in range(depth):
        last = l == depth - 1
        groups_l = [w[l] for w in w_in_groups]
        x2d = _layer(x2d, tables, batch, seq, ffn1_norm[l], ffn1_w_gate[l], ffn1_w_up[l], ffn1_w_down[l],
                     mix_norm[l], groups_l, ret_norm[l], w_out[l], ffn2_norm[l], ffn2_w_gate[l],
                     ffn2_w_up[l], ffn2_w_down[l], final_norm, last)
    return x2d.reshape(batch, seq, D_MODEL)
```

```python
import functools

import jax
import jax.numpy as jnp
from jax import lax
from jax.experimental import pallas as pl
from jax.experimental.pallas import tpu as pltpu

D_MODEL = 2048
H_RET = 8
DK_RET = 128
DV_RET = 128
RET_CHUNK = 128
H_ATT = 8
H_KV = 2
D_HEAD = 128
H_IDX = 16
D_IDX = 64
MAX_TOPK = 256
Q_BLOCK = 256
D_FF = 5632
ROPE_THETA = 10000.0
NORM_EPS = 1e-6

RET_W = H_RET * DV_RET
ATT_W = H_ATT * D_HEAD
GROUP = H_ATT // H_KV

LANES = 128
SUBLANES = 8
VMEM_LIMIT = 56 * 1024 * 1024

ROPE_COLS = 2 * H_RET * DK_RET + ATT_W + H_KV * D_HEAD
PLAIN_COLS = 2 * RET_W + H_KV * D_HEAD
IDX_Q_COLS = H_IDX * D_IDX
IDX_COLS = IDX_Q_COLS + 2 * LANES
AQ_CHUNK0 = 2 * H_RET * DK_RET // LANES

INT_MIN = -2 ** 31
NEG_BIG = -1e30

F32 = jnp.float32
BF16 = jnp.bfloat16


def _dot(a, b):
    return jnp.dot(a, b, preferred_element_type=F32)


def _dot_nt(a, b):
    return lax.dot_general(a, b, (((1,), (1,)), ((), ())), preferred_element_type=F32)


def _dot_tn(a, b):
    return lax.dot_general(a, b, (((0,), (0,)), ((), ())), preferred_element_type=F32)


def _rmsnorm(xf, g):
    ms = jnp.mean(xf * xf, axis=-1, keepdims=True)
    return xf * lax.rsqrt(ms + NORM_EPS) * g


def _params(sem):
    return pltpu.CompilerParams(dimension_semantics=sem, vmem_limit_bytes=VMEM_LIMIT)


W_IN_SPLITS = (0, 1024, 2048, 3072, 4096, 5120, 5376, 5632, 6656, 6720, 6736)


def _prep_kernel(pos_ref, inv_ref, sgn_a_ref, sgn_b_ref, w_in_ref,
                 cos_a_ref, sin_a_ref, cos_b_ref, sin_b_ref, w_rope_ref, w_plain_ref, w_idx_ref):
    ang = pos_ref[...].astype(F32) * inv_ref[...]
    lane = lax.broadcasted_iota(jnp.int32, ang.shape, 1)
    half, quarter = D_HEAD // 2, D_IDX // 2

    def table_a(t):
        return jnp.where(lane < half, t, pltpu.roll(t, half, axis=1))

    def table_b(t):
        upper = jnp.where(lane < half + quarter, t, pltpu.roll(t, quarter, axis=1))
        return jnp.where(lane >= half, upper, pltpu.roll(upper, half, axis=1))

    cos, sin = jnp.cos(ang), jnp.sin(ang)
    cos_a_ref[...] = table_a(cos)
    sin_a_ref[...] = table_a(sin) * sgn_a_ref[...]
    cos_b_ref[...] = table_b(cos)
    sin_b_ref[...] = table_b(sin) * sgn_b_ref[...]

    rq, rk, rv, rg, aq, ak, av, iq, ik, iw, end = W_IN_SPLITS
    for layer in range(w_in_ref.shape[0]):
        feats = lambda lo, hi: w_in_ref[layer, lo:hi, :].astype(BF16)
        w_rope_ref[layer, :rv - rq, :] = feats(rq, rv)
        w_rope_ref[layer, rv - rq:, :] = feats(aq, av)
        w_plain_ref[layer, :aq - rv, :] = feats(rv, aq)
        w_plain_ref[layer, aq - rv:, :] = feats(av, iq)
        w_idx_ref[layer, :ik - iq, :] = feats(iq, ik)
        key = feats(ik, iw)
        w_idx_ref[layer, ik - iq:ik - iq + D_IDX, :] = key
        w_idx_ref[layer, ik - iq + D_IDX:ik - iq + LANES, :] = key
        w_idx_ref[layer, ik - iq + LANES:ik - iq + LANES + H_IDX, :] = feats(iw, end)
        w_idx_ref[layer, ik - iq + LANES + H_IDX:, :] = jnp.zeros(
            (LANES - H_IDX, w_idx_ref.shape[2]), BF16)


def _prepare(positions, w_in_t):
    n_tok = positions.size
    tm = 1024
    n_steps = n_tok // tm
    depth, _, d_model = w_in_t.shape
    w_cols = d_model // n_steps
    lane = jnp.arange(LANES)

    def inv_freq(d):
        return ROPE_THETA ** (-jnp.arange(0, d, 2, dtype=F32) / d)

    unused = jnp.zeros((LANES - D_HEAD // 2 - D_IDX // 2,), F32)
    inv = jnp.concatenate([inv_freq(D_HEAD), inv_freq(D_IDX), unused])[None, :]

    def sign(d):
        return jnp.where(lane % d < d // 2, -1.0, 1.0).astype(F32)[None, :]

    row = pl.BlockSpec((1, LANES), lambda i: (0, 0))
    tab = pl.BlockSpec((tm, LANES), lambda i: (i, 0))
    out = jax.ShapeDtypeStruct((n_tok, LANES), F32)
    slab = lambda n_feats: pl.BlockSpec((depth, n_feats, w_cols), lambda i: (0, 0, i))
    w_out = lambda n_feats: jax.ShapeDtypeStruct((depth, n_feats, d_model), BF16)
    outs = pl.pallas_call(
        _prep_kernel,
        grid=(n_steps,),
        in_specs=[pl.BlockSpec((tm, 1), lambda i: (i, 0)), row, row, row, slab(w_in_t.shape[1])],
        out_specs=[tab, tab, tab, tab, slab(ROPE_COLS), slab(PLAIN_COLS), slab(IDX_COLS)],
        out_shape=[out, out, out, out, w_out(ROPE_COLS), w_out(PLAIN_COLS), w_out(IDX_COLS)],
        compiler_params=_params(("parallel",)),
        name="prepare",
    )(positions.reshape(n_tok, 1), inv, sign(D_HEAD), sign(D_IDX), w_in_t)
    return outs[:4], outs[4:]


FFN_ROW_CHUNK = 128


def _ffn_kernel(x_ref, g_ref, wg_ref, wu_ref, wd_ref, g2_ref, *refs, emit_residual):
    if emit_residual:
        res_ref, normed_ref, xn_ref = refs
        acc_ref = res_ref
    else:
        normed_ref, xn_ref = refs
        acc_ref = normed_ref
    f = pl.program_id(1)
    n_row_chunks = x_ref.shape[0] // FFN_ROW_CHUNK

    def rows_of(r):
        return pl.ds(pl.multiple_of(r * FFN_ROW_CHUNK, FFN_ROW_CHUNK), FFN_ROW_CHUNK)

    @pl.when(f == 0)
    def _():
        def prologue(r, carry):
            xf = x_ref[rows_of(r), :]
            xn_ref[rows_of(r), :] = _rmsnorm(xf, g_ref[...]).astype(BF16)
            acc_ref[rows_of(r), :] = 2.0 * xf
            return carry
        lax.fori_loop(0, n_row_chunks, prologue, 0)

    xn = xn_ref[...]
    a = _dot(xn, wg_ref[...].astype(BF16))
    b = _dot(xn, wu_ref[...].astype(BF16))
    hidden = (a * jax.nn.sigmoid(a) * b).astype(BF16)
    acc_ref[...] += _dot(hidden, wd_ref[...].astype(BF16))

    @pl.when(f == pl.num_programs(1) - 1)
    def _():
        def epilogue(r, carry):
            y = 0.5 * acc_ref[rows_of(r), :]
            if emit_residual:
                res_ref[rows_of(r), :] = y
            normed_ref[rows_of(r), :] = _rmsnorm(y, g2_ref[...]).astype(normed_ref.dtype)
            return carry
        lax.fori_loop(0, n_row_chunks, epilogue, 0)


FFN_TM = 1024
FFN_TF_F32 = 256
FFN_TF_BF16 = 512


def _ffn(x2d, g, wg, wu, wd, g2, *, emit_residual, tf, tm=FFN_TM):
    n_tok = x2d.shape[0]
    tile_map = lambda i, f: (i, 0)
    x_tile = pl.BlockSpec((tm, D_MODEL), tile_map)
    out_tile = pl.BlockSpec((tm, D_MODEL), tile_map)
    normed_tile = out_tile if wg.dtype == BF16 else pl.BlockSpec(
        (tm, D_MODEL), tile_map, pipeline_mode=pl.Buffered(1))
    row = pl.BlockSpec((1, D_MODEL), lambda i, f: (0, 0))
    if emit_residual:
        out_specs = [out_tile, normed_tile]
        out_shape = [jax.ShapeDtypeStruct((n_tok, D_MODEL), F32),
                     jax.ShapeDtypeStruct((n_tok, D_MODEL), BF16)]
    else:
        out_specs = out_tile
        out_shape = jax.ShapeDtypeStruct((n_tok, D_MODEL), F32)
    return pl.pallas_call(
        functools.partial(_ffn_kernel, emit_residual=emit_residual),
        grid=(n_tok // tm, D_FF // tf),
        in_specs=[x_tile, row,
                  pl.BlockSpec((D_MODEL, tf), lambda i, f: (0, f)),
                  pl.BlockSpec((D_MODEL, tf), lambda i, f: (0, f)),
                  pl.BlockSpec((tf, D_MODEL), lambda i, f: (f, 0)),
                  row],
        out_specs=out_specs,
        out_shape=out_shape,
        scratch_shapes=[pltpu.VMEM((tm, D_MODEL), BF16)],
        compiler_params=_params(("parallel", "arbitrary")),
        name="ffn_residual" if emit_residual else "ffn_final",
    )(x2d, g, wg, wu, wd, g2)


def _rope128(y, cos, sin):
    return y * cos + pltpu.roll(y, D_HEAD // 2, axis=1) * sin


def _rope64(y, cos, sin):
    lane = lax.broadcasted_iota(jnp.int32, y.shape, 1)
    first_half = (lane & (D_IDX - 1)) < D_IDX // 2
    partner = jnp.where(first_half,
                        pltpu.roll(y, LANES - D_IDX // 2, axis=1),
                        pltpu.roll(y, D_IDX // 2, axis=1))
    return y * cos + partner * sin


def _proj_kernel(h_ref, w_ref, cos_ref, sin_ref, o_ref, *, mode):
    y = _dot_nt(h_ref[...], w_ref[...])
    n_chunks = y.shape[1] // LANES
    if mode == "plain":
        o_ref[...] = y.astype(o_ref.dtype)
        return
    cos = cos_ref[...]
    sin = sin_ref[...]
    for c in range(n_chunks):
        yc = y[:, c * LANES:(c + 1) * LANES]
        if mode == "rope128":
            yc = _rope128(yc, cos, sin)
            if AQ_CHUNK0 <= c < AQ_CHUNK0 + H_ATT:
                yc = yc * (LOG2_E * D_HEAD ** -0.5)
        elif c < IDX_Q_COLS // LANES:
            yc = _rope64(yc, cos, sin) * (D_IDX ** -0.5)
        elif c == IDX_Q_COLS // LANES:
            yc = _rope64(yc, cos, sin)
        else:
            yc = yc * (H_IDX ** -0.5)
        o_ref[:, c * LANES:(c + 1) * LANES] = yc.astype(o_ref.dtype)


def _proj(h, w, cos, sin, *, mode, tm, out_dtype):
    n_tok, n_cols = h.shape[0], w.shape[0]
    tab = pl.BlockSpec((tm, LANES), lambda i: (i, 0))
    return pl.pallas_call(
        functools.partial(_proj_kernel, mode=mode),
        grid=(n_tok // tm,),
        in_specs=[pl.BlockSpec((tm, D_MODEL), lambda i: (i, 0)),
                  pl.BlockSpec((n_cols, D_MODEL), lambda i: (0, 0), pipeline_mode=pl.Buffered(1)),
                  tab, tab],
        out_specs=pl.BlockSpec((tm, n_cols), lambda i: (i, 0)),
        out_shape=jax.ShapeDtypeStruct((n_tok, n_cols), out_dtype),
        compiler_params=_params(("parallel",)),
        name="proj_" + mode,
    )(h, w, cos, sin)


RET_UNROLL = 8


def _retention_kernel(lg_ref, q_ref, k_ref, v_ref, g_ref, rn_ref, o_ref):
    C = RET_CHUNK
    n_chunks = q_ref.shape[0] // C
    lg = lg_ref[...]
    row = lax.broadcasted_iota(jnp.int32, (C, C), 0).astype(F32)
    col = lax.broadcasted_iota(jnp.int32, (C, C), 1).astype(F32)
    diff = row - col
    scale = DK_RET ** -0.5
    decay = jnp.where(diff >= 0, jnp.exp(jnp.maximum(diff, 0.0) * lg), 0.0) * scale
    k_dec = jnp.exp((C - 1 - row) * lg) * scale
    q_dec = jnp.exp((row + 1) * lg)
    g_chunk = jnp.exp(C * lg)
    rn = rn_ref[...]

    def body(n, state):
        sl = pl.ds(pl.multiple_of(n * C, C), C)
        qc = q_ref[sl, :]
        kc = k_ref[sl, :]
        vc = v_ref[sl, :]
        s = _dot_nt(qc, kc) * decay
        intra = _dot(s.astype(BF16), vc)
        cross = _dot(qc, state.astype(BF16)) * q_dec
        kv = _dot_tn((kc.astype(F32) * k_dec).astype(BF16), vc)
        o = intra + cross
        mu = jnp.mean(o, axis=-1, keepdims=True)
        d = o - mu
        var = jnp.mean(d * d, axis=-1, keepdims=True)
        y = d * lax.rsqrt(var + NORM_EPS) * rn
        gate = g_ref[sl, :].astype(F32)
        o_ref[sl, :] = (y * (gate * jax.nn.sigmoid(gate))).astype(o_ref.dtype)
        return state * g_chunk + kv

    lax.fori_loop(0, n_chunks, body, jnp.zeros((DK_RET, DV_RET), F32), unroll=RET_UNROLL)


def _retention(rope_out, plain_out, ret_norm, batch, seq):
    n_tok = batch * seq
    lg = jnp.log1p(-jnp.exp2(-5.0 - jnp.arange(H_RET, dtype=F32)))
    lg = jnp.broadcast_to(lg[:, None, None], (H_RET, 1, LANES))
    head = lambda off: pl.BlockSpec((seq, LANES), lambda b, h: (b, off + h))
    return pl.pallas_call(
        _retention_kernel,
        grid=(batch, H_RET),
        in_specs=[pl.BlockSpec((None, 1, LANES), lambda b, h: (h, 0, 0)),
                  head(0), head(H_RET),
                  head(0), head(H_RET),
                  pl.BlockSpec((1, LANES), lambda b, h: (0, h))],
        out_specs=head(0),
        out_shape=jax.ShapeDtypeStruct((n_tok, RET_W), BF16),
        compiler_params=_params(("parallel", "parallel")),
        name="retention",
    )(lg, rope_out, rope_out, plain_out, plain_out, ret_norm)


KEY_TILE = 512
SCORE_ROWS = 128
FLT_MAX_KEY = 0x00800000
KEY_BITS = 32
FLT_MIN = 1.1754943508222875e-38
ATT_HEADS_PER_CHAIN = 2
LOG2_E = 1.4426950408889634


def _order_key_to_float(u):
    s = u ^ INT_MIN
    return pltpu.bitcast(s ^ ((s >> 31) & jnp.int32(0x7FFFFFFF)), F32)


def _attn_kernel(iq_ref, ik_ref, iw_ref, aq_ref, ak_ref, av_ref, *rest, top_k, n_cast):
    o_ref, sc_ref, sc16_ref = rest[n_cast], rest[-2], rest[-1]
    for w32_ref, w16_ref in zip(rest[:n_cast], rest[n_cast + 1:-2]):
        w16_ref[...] = w32_ref[...].astype(BF16)

    qb = pl.program_id(1)
    t0 = qb * Q_BLOCK
    n_tiles = (t0 + Q_BLOCK + KEY_TILE - 1) // KEY_TILE

    w_t = iw_ref[...].astype(F32).T
    w_rows = [w_t[h:h + 1, :] for h in range(H_IDX)]

    lane = lax.broadcasted_iota(jnp.int32, (Q_BLOCK, LANES), 1)
    low = lane < D_IDX
    q_pairs = []
    for c in range(IDX_Q_COLS // LANES):
        qc = iq_ref[:, c * LANES:(c + 1) * LANES]
        zero = jnp.zeros_like(qc)
        q_pairs.append(jnp.concatenate([jnp.where(low, qc, zero), jnp.where(low, zero, qc)], axis=0))

    q_pos = t0 + lax.broadcasted_iota(jnp.int32, (SCORE_ROWS, Q_BLOCK), 1)
    key_off = lax.broadcasted_iota(jnp.int32, (SCORE_ROWS, Q_BLOCK), 0)

    def score_tile(kt, carry):
        for s in range(KEY_TILE // SCORE_ROWS):
            base = pl.multiple_of(kt * KEY_TILE + s * SCORE_ROWS, SCORE_ROWS)
            kk = ik_ref[pl.ds(base, SCORE_ROWS), :]
            acc = jnp.zeros((SCORE_ROWS, Q_BLOCK), F32)
            for c, qp in enumerate(q_pairs):
                z = _dot_nt(kk, qp)
                acc = acc + jnp.maximum(z[:, :Q_BLOCK], 0.0) * w_rows[2 * c]
                acc = acc + jnp.maximum(z[:, Q_BLOCK:], 0.0) * w_rows[2 * c + 1]
            causal = base + key_off <= q_pos
            score = jnp.where(causal, acc, -jnp.inf)
            sc_ref[pl.ds(base, SCORE_ROWS), :] = score
            sc16_ref[pl.ds(base, SCORE_ROWS), :] = score.astype(BF16)
        return carry

    lax.fori_loop(0, n_tiles, score_tile, 0)

    def count_tiles(ref, indicator):
        rows = SUBLANES * 4 // ref.dtype.itemsize

        def count_tile(kt, cnt):
            base = pl.multiple_of(kt * KEY_TILE, KEY_TILE)
            part = indicator(ref[pl.ds(base, KEY_TILE), :], base)
            part = part.reshape(KEY_TILE // rows, rows, Q_BLOCK)
            while part.shape[0] > 1:
                half = part.shape[0] // 2
                part = part[:half] + part[half:]
            return cnt + part[0].astype(F32)

        cnt = lax.fori_loop(0, n_tiles, count_tile, jnp.zeros((rows, Q_BLOCK), F32))
        return jnp.sum(cnt, axis=0, keepdims=True)

    def count_ge(ref, cand):
        one, zero = jnp.ones((), ref.dtype), jnp.zeros((), ref.dtype)
        return count_tiles(ref, lambda tile, base: jnp.where(tile >= cand, one, zero))

    def search_bits(ref, tau_u, first_bit, n_bits):
        def search_bit(it, tau_u):
            cand_u = tau_u | lax.shift_left(jnp.int32(1), first_bit - it)
            cand = _order_key_to_float(cand_u).astype(ref.dtype)
            return jnp.where(count_ge(ref, cand) >= top_k, cand_u, tau_u)
        return lax.fori_loop(0, n_bits, search_bit, tau_u)

    searched = t0 + Q_BLOCK > top_k
    half_bits = jnp.where(searched, KEY_BITS // 2, 0)
    rounded = search_bits(sc16_ref, jnp.zeros((1, Q_BLOCK), jnp.int32), KEY_BITS - 1, half_bits)
    fits = count_ge(sc_ref, _order_key_to_float(rounded)) >= top_k
    prefix = jnp.where(fits, rounded, rounded - (1 << KEY_BITS // 2))
    tau_u = search_bits(sc_ref, prefix, KEY_BITS // 2 - 1, half_bits)
    tau_u = jnp.where(searched, tau_u, FLT_MAX_KEY)
    tau = _order_key_to_float(tau_u)

    n_ge = count_ge(sc_ref, tau)
    surplus = jnp.max(n_ge) > top_k

    @pl.when(jnp.logical_and(searched, surplus))
    def _():
        key_row = lax.broadcasted_iota(jnp.int32, (KEY_TILE, Q_BLOCK), 0)

        def tied_before(cutoff):
            return lambda tile, base: jnp.where(
                tile == tau, jnp.where(base + key_row < cutoff, 1.0, 0.0), 0.0)

        n_eq = count_tiles(sc_ref, tied_before(jnp.int32(sc_ref.shape[0])))
        keep = top_k - (n_ge - n_eq)
        n_cut_bits = sc_ref.shape[0].bit_length()

        def cutoff_bit(it, cutoff):
            cand = cutoff | lax.shift_left(jnp.int32(1), n_cut_bits - 1 - it)
            return jnp.where(count_tiles(sc_ref, tied_before(cand)) <= keep, cand, cutoff)

        cutoff = lax.fori_loop(0, n_cut_bits, cutoff_bit, jnp.zeros((1, Q_BLOCK), jnp.int32))
        below = jnp.where(tau == 0.0, -FLT_MIN, _order_key_to_float(tau_u - 1))

        def demote(kt, carry):
            base = pl.multiple_of(kt * KEY_TILE, KEY_TILE)
            tile = sc_ref[pl.ds(base, KEY_TILE), :]
            demoted = jnp.where(base + key_row < cutoff, tile, below)
            sc_ref[pl.ds(base, KEY_TILE), :] = jnp.where(tile == tau, demoted, tile)
            return carry

        lax.fori_loop(0, n_tiles, demote, 0)

    n_chains = H_ATT // ATT_HEADS_PER_CHAIN
    width = ATT_HEADS_PER_CHAIN * Q_BLOCK
    q_chains = []
    for c in range(n_chains):
        heads = range(c * ATT_HEADS_PER_CHAIN, (c + 1) * ATT_HEADS_PER_CHAIN)
        q_chains.append(jnp.concatenate([aq_ref[:, h * D_HEAD:(h + 1) * D_HEAD] for h in heads], axis=0))

    def attend_tile(kt, carry):
        base = pl.multiple_of(kt * KEY_TILE, KEY_TILE)
        bias = jnp.where(sc_ref[pl.ds(base, KEY_TILE), :] >= tau, 0.0, NEG_BIG)
        bias = jnp.concatenate([bias] * ATT_HEADS_PER_CHAIN, axis=1)
        kv_of = lambda c: c * ATT_HEADS_PER_CHAIN // GROUP
        k_tiles = [ak_ref[pl.ds(base, KEY_TILE), g * D_HEAD:(g + 1) * D_HEAD] for g in range(H_KV)]
        v_tiles = [av_ref[pl.ds(base, KEY_TILE), g * D_HEAD:(g + 1) * D_HEAD] for g in range(H_KV)]
        logits = [_dot_nt(k_tiles[kv_of(c)], q_chains[c]) + bias for c in range(n_chains)]
        stats = []
        for c in range(n_chains):
            m_old, l_old, _ = carry[c]
            m_new = jnp.maximum(m_old, jnp.max(logits[c], axis=0, keepdims=True))
            alpha = jnp.exp2(m_old - m_new)
            p = jnp.exp2(logits[c] - m_new)
            l_new = alpha * l_old + jnp.sum(p, axis=0, keepdims=True)
            stats.append((m_new, l_new, alpha, p.astype(BF16)))
        new = []
        for c in range(n_chains):
            m_new, l_new, alpha, p = stats[c]
            acc_new = alpha * carry[c][2] + _dot_tn(v_tiles[kv_of(c)], p)
            new.append((m_new, l_new, acc_new))
        return tuple(new)

    init = tuple((jnp.full((1, width), NEG_BIG, F32), jnp.zeros((1, width), F32),
                  jnp.zeros((D_HEAD, width), F32)) for _ in range(n_chains))
    final = lax.fori_loop(0, n_tiles, attend_tile, init)
    for c in range(n_chains):
        _, l_fin, acc_fin = final[c]
        out_t = acc_fin / l_fin
        for j in range(ATT_HEADS_PER_CHAIN):
            hcol = (c * ATT_HEADS_PER_CHAIN + j) * D_HEAD
            o_ref[:, hcol:hcol + D_HEAD] = out_t[:, j * Q_BLOCK:(j + 1) * Q_BLOCK].T.astype(o_ref.dtype)


def _sparse_attention(idx_out, rope_out, plain_out, batch, seq, weights_to_cast):
    n_tok = batch * seq
    nb = seq // Q_BLOCK
    n_steps = batch * nb
    top_k = min(MAX_TOPK, seq // 4)
    kv_w = H_KV * D_HEAD
    step = lambda b, q: (b * nb + q, 0)
    slabs = [pl.BlockSpec((w.shape[0] // n_steps, w.shape[1]), step) for w in weights_to_cast]
    outs = pl.pallas_call(
        functools.partial(_attn_kernel, top_k=top_k, n_cast=len(weights_to_cast)),
        grid=(batch, nb),
        in_specs=[
            pl.BlockSpec((Q_BLOCK, IDX_Q_COLS), step),
            pl.BlockSpec((seq, LANES), lambda b, q: (b, IDX_Q_COLS // LANES)),
            pl.BlockSpec((Q_BLOCK, LANES), lambda b, q: (b * nb + q, IDX_Q_COLS // LANES + 1)),
            pl.BlockSpec((Q_BLOCK, ATT_W), lambda b, q: (b * nb + q, 2 * RET_W // ATT_W)),
            pl.BlockSpec((seq, kv_w), lambda b, q: (b, (2 * RET_W + ATT_W) // kv_w)),
            pl.BlockSpec((seq, kv_w), lambda b, q: (b, 2 * RET_W // kv_w)),
        ] + slabs,
        out_specs=[pl.BlockSpec((Q_BLOCK, ATT_W), step)] + slabs,
        out_shape=[jax.ShapeDtypeStruct((n_tok, ATT_W), BF16)]
                  + [jax.ShapeDtypeStruct(w.shape, BF16) for w in weights_to_cast],
        scratch_shapes=[pltpu.VMEM((seq, Q_BLOCK), F32), pltpu.VMEM((seq, Q_BLOCK), BF16)],
        compiler_params=_params(("parallel", "arbitrary")),
        name="sparse_attention",
    )(idx_out, idx_out, idx_out, rope_out, rope_out, plain_out, *weights_to_cast)
    return outs[0], outs[1:]


def _out_proj_kernel(x_ref, ro_ref, ao_ref, wr_ref, wa_ref, o_ref):
    o_ref[...] = x_ref[...] + _dot(ro_ref[...], wr_ref[...]) + _dot(ao_ref[...], wa_ref[...])


def _out_proj(x2d, ro, ao, w_out, tm=512):
    n_tok = x2d.shape[0]
    tile = pl.BlockSpec((tm, D_MODEL), lambda i: (i, 0))
    half = pl.BlockSpec((tm, RET_W), lambda i: (i, 0))
    w_ret = pl.BlockSpec((RET_W, D_MODEL), lambda i: (0, 0))
    w_att = pl.BlockSpec((ATT_W, D_MODEL), lambda i: (RET_W // ATT_W, 0))
    return pl.pallas_call(
        _out_proj_kernel,
        grid=(n_tok // tm,),
        in_specs=[tile, half, half, w_ret, w_att],
        out_specs=tile,
        out_shape=jax.ShapeDtypeStruct((n_tok, D_MODEL), F32),
        compiler_params=_params(("parallel",)),
        name="out_proj",
    )(x2d, ro, ao, w_out, w_out)


def _layer(x2d, tables, batch, seq, ffn1_norm, ffn1_w_gate, ffn1_w_up, ffn1_w_down, mix_norm,
           w_in_groups, ret_norm, w_out, ffn2_norm, ffn2_w_gate, ffn2_w_up, ffn2_w_down, final_norm, last):
    cos_a, sin_a, cos_b, sin_b = tables
    w_rope, w_plain, w_idx = w_in_groups
    row = lambda g: g.reshape(1, -1).astype(F32)

    x1, h = _ffn(x2d, row(ffn1_norm), ffn1_w_gate, ffn1_w_up, ffn1_w_down,
                 row(mix_norm), emit_residual=True, tf=FFN_TF_F32)
    rope_out = _proj(h, w_rope, cos_a, sin_a, mode="rope128", tm=512, out_dtype=BF16)
    plain_out = _proj(h, w_plain, cos_a, sin_a, mode="plain", tm=512, out_dtype=BF16)
    idx_out = _proj(h, w_idx, cos_b, sin_b, mode="idx", tm=1024, out_dtype=BF16)
    ro = _retention(rope_out, plain_out, row(ret_norm), batch, seq)
    ao, (w_out16, wg16, wu16, wd16) = _sparse_attention(
        idx_out, rope_out, plain_out, batch, seq, [w_out, ffn2_w_gate, ffn2_w_up, ffn2_w_down])
    x2 = _out_proj(x1, ro, ao, w_out16)
    out = _ffn(x2, row(ffn2_norm), wg16, wu16, wd16, row(final_norm), emit_residual=not last,
               tf=FFN_TF_BF16)
    return out if last else out[0]


def kernel(x, positions, ffn1_norm, ffn1_w_gate, ffn1_w_up, ffn1_w_down, mix_norm, w_in, ret_norm,
           w_out, ffn2_norm, ffn2_w_gate, ffn2_w_up, ffn2_w_down, final_norm):
    batch, seq, _ = x.shape
    depth = w_in.shape[0]
    tables, w_in_groups = _prepare(positions, jnp.swapaxes(w_in, 1, 2))
    x2d = x.reshape(batch * seq, D_MODEL)
    for l in range(depth):
        last = l == depth - 1
        groups_l = [w[l] for w in w_in_groups]
        x2d = _layer(x2d, tables, batch, seq, ffn1_norm[l], ffn1_w_gate[l], ffn1_w_up[l], ffn1_w_down[l],
                     mix_norm[l], groups_l, ret_norm[l], w_out[l], ffn2_norm[l], ffn2_w_gate[l],
                     ffn2_w_up[l], ffn2_w_down[l], final_norm, last)
    return x2d.reshape(batch, seq, D_MODEL)
```

```python
import functools

import jax
import jax.numpy as jnp
from jax import lax
from jax.experimental import pallas as pl
from jax.experimental.pallas import tpu as pltpu

D_MODEL = 2048
H_RET = 8
DK_RET = 128
DV_RET = 128
RET_CHUNK = 128
H_ATT = 8
H_KV = 2
D_HEAD = 128
H_IDX = 16
D_IDX = 64
MAX_TOPK = 256
Q_BLOCK = 256
D_FF = 5632
ROPE_THETA = 10000.0
NORM_EPS = 1e-6

RET_W = H_RET * DV_RET
ATT_W = H_ATT * D_HEAD
GROUP = H_ATT // H_KV

LANES = 128
SUBLANES = 8
VMEM_LIMIT = 56 * 1024 * 1024

ROPE_COLS = 2 * H_RET * DK_RET + ATT_W + H_KV * D_HEAD
PLAIN_COLS = 2 * RET_W + H_KV * D_HEAD
IDX_Q_COLS = H_IDX * D_IDX
IDX_COLS = IDX_Q_COLS + 2 * LANES
AQ_CHUNK0 = 2 * H_RET * DK_RET // LANES

INT_MIN = -2 ** 31
NEG_BIG = -1e30

F32 = jnp.float32
BF16 = jnp.bfloat16


def _dot(a, b):
    return jnp.dot(a, b, preferred_element_type=F32)


def _dot_nt(a, b):
    return lax.dot_general(a, b, (((1,), (1,)), ((), ())), preferred_element_type=F32)


def _dot_tn(a, b):
    return lax.dot_general(a, b, (((0,), (0,)), ((), ())), preferred_element_type=F32)


def _rmsnorm(xf, g):
    ms = jnp.mean(xf * xf, axis=-1, keepdims=True)
    return xf * lax.rsqrt(ms + NORM_EPS) * g


def _params(sem):
    return pltpu.CompilerParams(dimension_semantics=sem, vmem_limit_bytes=VMEM_LIMIT)


W_IN_SPLITS = (0, 1024, 2048, 3072, 4096, 5120, 5376, 5632, 6656, 6720, 6736)


def _prep_kernel(pos_ref, inv_ref, sgn_a_ref, sgn_b_ref, w_in_ref,
                 cos_a_ref, sin_a_ref, cos_b_ref, sin_b_ref, w_rope_ref, w_plain_ref, w_idx_ref):
    ang = pos_ref[...].astype(F32) * inv_ref[...]
    lane = lax.broadcasted_iota(jnp.int32, ang.shape, 1)
    half, quarter = D_HEAD // 2, D_IDX // 2

    def table_a(t):
        return jnp.where(lane < half, t, pltpu.roll(t, half, axis=1))

    def table_b(t):
        upper = jnp.where(lane < half + quarter, t, pltpu.roll(t, quarter, axis=1))
        return jnp.where(lane >= half, upper, pltpu.roll(upper, half, axis=1))

    cos, sin = jnp.cos(ang), jnp.sin(ang)
    cos_a_ref[...] = table_a(cos)
    sin_a_ref[...] = table_a(sin) * sgn_a_ref[...]
    cos_b_ref[...] = table_b(cos)
    sin_b_ref[...] = table_b(sin) * sgn_b_ref[...]

    rq, rk, rv, rg, aq, ak, av, iq, ik, iw, end = W_IN_SPLITS
    for layer in range(w_in_ref.shape[0]):
        feats = lambda lo, hi: w_in_ref[layer, lo:hi, :].astype(BF16)
        w_rope_ref[layer, :rv - rq, :] = feats(rq, rv)
        w_rope_ref[layer, rv - rq:, :] = feats(aq, av)
        w_plain_ref[layer, :aq - rv, :] = feats(rv, aq)
        w_plain_ref[layer, aq - rv:, :] = feats(av, iq)
        w_idx_ref[layer, :ik - iq, :] = feats(iq, ik)
        key = feats(ik, iw)
        w_idx_ref[layer, ik - iq:ik - iq + D_IDX, :] = key
        w_idx_ref[layer, ik - iq + D_IDX:ik - iq + LANES, :] = key
        w_idx_ref[layer, ik - iq + LANES:ik - iq + LANES + H_IDX, :] = feats(iw, end)
        w_idx_ref[layer, ik - iq + LANES + H_IDX:, :] = jnp.zeros(
            (LANES - H_IDX, w_idx_ref.shape[2]), BF16)


def _prepare(positions, w_in_t):
    n_tok = positions.size
    tm = 1024
    n_steps = n_tok // tm
    depth, _, d_model = w_in_t.shape
    w_cols = d_model // n_steps
    lane = jnp.arange(LANES)

    def inv_freq(d):
        return ROPE_THETA ** (-jnp.arange(0, d, 2, dtype=F32) / d)

    unused = jnp.zeros((LANES - D_HEAD // 2 - D_IDX // 2,), F32)
    inv = jnp.concatenate([inv_freq(D_HEAD), inv_freq(D_IDX), unused])[None, :]

    def sign(d):
        return jnp.where(lane % d < d // 2, -1.0, 1.0).astype(F32)[None, :]

    row = pl.BlockSpec((1, LANES), lambda i: (0, 0))
    tab = pl.BlockSpec((tm, LANES), lambda i: (i, 0))
    out = jax.ShapeDtypeStruct((n_tok, LANES), F32)
    slab = lambda n_feats: pl.BlockSpec((depth, n_feats, w_cols), lambda i: (0, 0, i))
    w_out = lambda n_feats: jax.ShapeDtypeStruct((depth, n_feats, d_model), BF16)
    outs = pl.pallas_call(
        _prep_kernel,
        grid=(n_steps,),
        in_specs=[pl.BlockSpec((tm, 1), lambda i: (i, 0)), row, row, row, slab(w_in_t.shape[1])],
        out_specs=[tab, tab, tab, tab, slab(ROPE_COLS), slab(PLAIN_COLS), slab(IDX_COLS)],
        out_shape=[out, out, out, out, w_out(ROPE_COLS), w_out(PLAIN_COLS), w_out(IDX_COLS)],
        compiler_params=_params(("parallel",)),
        name="prepare",
    )(positions.reshape(n_tok, 1), inv, sign(D_HEAD), sign(D_IDX), w_in_t)
    return outs[:4], outs[4:]


FFN_ROW_CHUNK = 128


def _ffn_kernel(x_ref, g_ref, wg_ref, wu_ref, wd_ref, g2_ref, *refs, emit_residual):
    if emit_residual:
        res_ref, normed_ref, xn_ref = refs
        acc_ref = res_ref
    else:
        normed_ref, xn_ref = refs
        acc_ref = normed_ref
    f = pl.program_id(1)
    n_row_chunks = x_ref.shape[0] // FFN_ROW_CHUNK

    def rows_of(r):
        return pl.ds(pl.multiple_of(r * FFN_ROW_CHUNK, FFN_ROW_CHUNK), FFN_ROW_CHUNK)

    @pl.when(f == 0)
    def _():
        def prologue(r, carry):
            xf = x_ref[rows_of(r), :]
            xn_ref[rows_of(r), :] = _rmsnorm(xf, g_ref[...]).astype(BF16)
            acc_ref[rows_of(r), :] = 2.0 * xf
            return carry
        lax.fori_loop(0, n_row_chunks, prologue, 0)

    xn = xn_ref[...]
    a = _dot(xn, wg_ref[...].astype(BF16))
    b = _dot(xn, wu_ref[...].astype(BF16))
    hidden = (a * jax.nn.sigmoid(a) * b).astype(BF16)
    acc_ref[...] += _dot(hidden, wd_ref[...].astype(BF16))

    @pl.when(f == pl.num_programs(1) - 1)
    def _():
        def epilogue(r, carry):
            y = 0.5 * acc_ref[rows_of(r), :]
            if emit_residual:
                res_ref[rows_of(r), :] = y
            normed_ref[rows_of(r), :] = _rmsnorm(y, g2_ref[...]).astype(normed_ref.dtype)
            return carry
        lax.fori_loop(0, n_row_chunks, epilogue, 0)


FFN_TM = 1024
FFN_TF_F32 = 256
FFN_TF_BF16 = 512


def _ffn(x2d, g, wg, wu, wd, g2, *, emit_residual, tf, tm=FFN_TM):
    n_tok = x2d.shape[0]
    tile_map = lambda i, f: (i, 0)
    x_tile = pl.BlockSpec((tm, D_MODEL), tile_map)
    out_tile = pl.BlockSpec((tm, D_MODEL), tile_map)
    normed_tile = out_tile if wg.dtype == BF16 else pl.BlockSpec(
        (tm, D_MODEL), tile_map, pipeline_mode=pl.Buffered(1))
    row = pl.BlockSpec((1, D_MODEL), lambda i, f: (0, 0))
    if emit_residual:
        out_specs = [out_tile, normed_tile]
        out_shape = [jax.ShapeDtypeStruct((n_tok, D_MODEL), F32),
                     jax.ShapeDtypeStruct((n_tok, D_MODEL), BF16)]
    else:
        out_specs = out_tile
        out_shape = jax.ShapeDtypeStruct((n_tok, D_MODEL), F32)
    return pl.pallas_call(
        functools.partial(_ffn_kernel, emit_residual=emit_residual),
        grid=(n_tok // tm, D_FF // tf),
        in_specs=[x_tile, row,
                  pl.BlockSpec((D_MODEL, tf), lambda i, f: (0, f)),
                  pl.BlockSpec((D_MODEL, tf), lambda i, f: (0, f)),
                  pl.BlockSpec((tf, D_MODEL), lambda i, f: (f, 0)),
                  row],
        out_specs=out_specs,
        out_shape=out_shape,
        scratch_shapes=[pltpu.VMEM((tm, D_MODEL), BF16)],
        compiler_params=_params(("parallel", "arbitrary")),
        name="ffn_residual" if emit_residual else "ffn_final",
    )(x2d, g, wg, wu, wd, g2)


def _rope128(y, cos, sin):
    return y * cos + pltpu.roll(y, D_HEAD // 2, axis=1) * sin


def _rope64(y, cos, sin):
    lane = lax.broadcasted_iota(jnp.int32, y.shape, 1)
    first_half = (lane & (D_IDX - 1)) < D_IDX // 2
    partner = jnp.where(first_half,
                        pltpu.roll(y, LANES - D_IDX // 2, axis=1),
                        pltpu.roll(y, D_IDX // 2, axis=1))
    return y * cos + partner * sin


def _proj_kernel(h_ref, w_ref, cos_ref, sin_ref, o_ref, *, mode):
    y = _dot_nt(h_ref[...], w_ref[...])
    n_chunks = y.shape[1] // LANES
    if mode == "plain":
        o_ref[...] = y.astype(o_ref.dtype)
        return
    cos = cos_ref[...]
    sin = sin_ref[...]
    for c in range(n_chunks):
        yc = y[:, c * LANES:(c + 1) * LANES]
        if mode == "rope128":
            yc = _rope128(yc, cos, sin)
            if AQ_CHUNK0 <= c < AQ_CHUNK0 + H_ATT:
                yc = yc * (LOG2_E * D_HEAD ** -0.5)
        elif c < IDX_Q_COLS // LANES:
            yc = _rope64(yc, cos, sin) * (D_IDX ** -0.5)
        elif c == IDX_Q_COLS // LANES:
            yc = _rope64(yc, cos, sin)
        else:
            yc = yc * (H_IDX ** -0.5)
        o_ref[:, c * LANES:(c + 1) * LANES] = yc.astype(o_ref.dtype)


def _proj(h, w, cos, sin, *, mode, tm, out_dtype):
    n_tok, n_cols = h.shape[0], w.shape[0]
    tab = pl.BlockSpec((tm, LANES), lambda i: (i, 0))
    return pl.pallas_call(
        functools.partial(_proj_kernel, mode=mode),
        grid=(n_tok // tm,),
        in_specs=[pl.BlockSpec((tm, D_MODEL), lambda i: (i, 0)),
                  pl.BlockSpec((n_cols, D_MODEL), lambda i: (0, 0), pipeline_mode=pl.Buffered(1)),
                  tab, tab],
        out_specs=pl.BlockSpec((tm, n_cols), lambda i: (i, 0)),
        out_shape=jax.ShapeDtypeStruct((n_tok, n_cols), out_dtype),
        compiler_params=_params(("parallel",)),
        name="proj_" + mode,
    )(h, w, cos, sin)


RET_UNROLL = 8
RET_HEADS_PER_STEP = 2


def _retention_kernel(lg_ref, q_ref, k_ref, v_ref, g_ref, rn_ref, o_ref):
    C = RET_CHUNK
    n_chunks = q_ref.shape[0] // C
    row = lax.broadcasted_iota(jnp.int32, (C, C), 0).astype(F32)
    col = lax.broadcasted_iota(jnp.int32, (C, C), 1).astype(F32)
    diff = row - col
    scale = DK_RET ** -0.5
    heads = []
    for j in range(RET_HEADS_PER_STEP):
        lg = lg_ref[j]
        heads.append(dict(
            lanes=slice(j * LANES, (j + 1) * LANES),
            decay=jnp.where(diff >= 0, jnp.exp(jnp.maximum(diff, 0.0) * lg), 0.0) * scale,
            k_dec=jnp.exp((C - 1 - row) * lg) * scale,
            q_dec=jnp.exp((row + 1) * lg),
            g_chunk=jnp.exp(C * lg)))

    def body(n, states):
        sl = pl.ds(pl.multiple_of(n * C, C), C)
        new_states = []
        for hd, state in zip(heads, states):
            qc = q_ref[sl, hd["lanes"]]
            kc = k_ref[sl, hd["lanes"]]
            vc = v_ref[sl, hd["lanes"]]
            s = _dot_nt(qc, kc) * hd["decay"]
            intra = _dot(s.astype(BF16), vc)
            cross = _dot(qc, state.astype(BF16)) * hd["q_dec"]
            kv = _dot_tn((kc.astype(F32) * hd["k_dec"]).astype(BF16), vc)
            o = intra + cross
            mu = jnp.mean(o, axis=-1, keepdims=True)
            d = o - mu
            var = jnp.mean(d * d, axis=-1, keepdims=True)
            y = d * lax.rsqrt(var + NORM_EPS) * rn_ref[:, hd["lanes"]]
            gate = g_ref[sl, hd["lanes"]].astype(F32)
            o_ref[sl, hd["lanes"]] = (y * (gate * jax.nn.sigmoid(gate))).astype(o_ref.dtype)
            new_states.append(state * hd["g_chunk"] + kv)
        return tuple(new_states)

    init = tuple(jnp.zeros((DK_RET, DV_RET), F32) for _ in heads)
    lax.fori_loop(0, n_chunks, body, init, unroll=RET_UNROLL)


def _retention(rope_out, plain_out, ret_norm, batch, seq):
    n_tok = batch * seq
    lg = jnp.log1p(-jnp.exp2(-5.0 - jnp.arange(H_RET, dtype=F32)))
    lg = jnp.broadcast_to(lg[:, None, None], (H_RET, 1, LANES))
    width = RET_HEADS_PER_STEP * LANES
    n_groups = H_RET // RET_HEADS_PER_STEP
    heads = lambda off: pl.BlockSpec((seq, width), lambda b, h: (b, off + h))
    return pl.pallas_call(
        _retention_kernel,
        grid=(batch, n_groups),
        in_specs=[pl.BlockSpec((RET_HEADS_PER_STEP, 1, LANES), lambda b, h: (h, 0, 0)),
                  heads(0), heads(n_groups),
                  heads(0), heads(n_groups),
                  pl.BlockSpec((1, width), lambda b, h: (0, h))],
        out_specs=heads(0),
        out_shape=jax.ShapeDtypeStruct((n_tok, RET_W), BF16),
        compiler_params=_params(("parallel", "parallel")),
        name="retention",
    )(lg, rope_out, rope_out, plain_out, plain_out, ret_norm)


KEY_TILE = 512
SCORE_ROWS = 128
FLT_MAX_KEY = 0x00800000
KEY_BITS = 32
FLT_MIN = 1.1754943508222875e-38
ATT_HEADS_PER_CHAIN = 2
LOG2_E = 1.4426950408889634


def _order_key_to_float(u):
    s = u ^ INT_MIN
    return pltpu.bitcast(s ^ ((s >> 31) & jnp.int32(0x7FFFFFFF)), F32)


def _attn_kernel(iq_ref, ik_ref, iw_ref, aq_ref, ak_ref, av_ref, *rest, top_k, n_cast):
    o_ref, sc_ref, sc16_ref = rest[n_cast], rest[-2], rest[-1]
    for w32_ref, w16_ref in zip(rest[:n_cast], rest[n_cast + 1:-2]):
        w16_ref[...] = w32_ref[...].astype(BF16)

    qb = pl.program_id(1)
    t0 = qb * Q_BLOCK
    n_tiles = (t0 + Q_BLOCK + KEY_TILE - 1) // KEY_TILE

    w_t = iw_ref[...].astype(F32).T
    w_rows = [w_t[h:h + 1, :] for h in range(H_IDX)]

    lane = lax.broadcasted_iota(jnp.int32, (Q_BLOCK, LANES), 1)
    low = lane < D_IDX
    q_pairs = []
    for c in range(IDX_Q_COLS // LANES):
        qc = iq_ref[:, c * LANES:(c + 1) * LANES]
        zero = jnp.zeros_like(qc)
        q_pairs.append(jnp.concatenate([jnp.where(low, qc, zero), jnp.where(low, zero, qc)], axis=0))

    q_pos = t0 + lax.broadcasted_iota(jnp.int32, (SCORE_ROWS, Q_BLOCK), 1)
    key_off = lax.broadcasted_iota(jnp.int32, (SCORE_ROWS, Q_BLOCK), 0)

    def score_tile(kt, carry):
        for s in range(KEY_TILE // SCORE_ROWS):
            base = pl.multiple_of(kt * KEY_TILE + s * SCORE_ROWS, SCORE_ROWS)
            kk = ik_ref[pl.ds(base, SCORE_ROWS), :]
            acc = jnp.zeros((SCORE_ROWS, Q_BLOCK), F32)
            for c, qp in enumerate(q_pairs):
                z = _dot_nt(kk, qp)
                acc = acc + jnp.maximum(z[:, :Q_BLOCK], 0.0) * w_rows[2 * c]
                acc = acc + jnp.maximum(z[:, Q_BLOCK:], 0.0) * w_rows[2 * c + 1]
            causal = base + key_off <= q_pos
            score = jnp.where(causal, acc, -jnp.inf)
            sc_ref[pl.ds(base, SCORE_ROWS), :] = score
            sc16_ref[pl.ds(base, SCORE_ROWS), :] = score.astype(BF16)
        return carry

    lax.fori_loop(0, n_tiles, score_tile, 0)

    def count_tiles(ref, indicator):
        rows = SUBLANES * 4 // ref.dtype.itemsize

        def count_tile(kt, cnt):
            base = pl.multiple_of(kt * KEY_TILE, KEY_TILE)
            part = indicator(ref[pl.ds(base, KEY_TILE), :], base)
            part = part.reshape(KEY_TILE // rows, rows, Q_BLOCK)
            while part.shape[0] > 1:
                half = part.shape[0] // 2
                part = part[:half] + part[half:]
            return cnt + part[0].astype(F32)

        cnt = lax.fori_loop(0, n_tiles, count_tile, jnp.zeros((rows, Q_BLOCK), F32))
        return jnp.sum(cnt, axis=0, keepdims=True)

    def count_ge(ref, cand):
        one, zero = jnp.ones((), ref.dtype), jnp.zeros((), ref.dtype)
        return count_tiles(ref, lambda tile, base: jnp.where(tile >= cand, one, zero))

    def search_bits(ref, tau_u, first_bit, n_bits):
        def search_bit(it, tau_u):
            cand_u = tau_u | lax.shift_left(jnp.int32(1), first_bit - it)
            cand = _order_key_to_float(cand_u).astype(ref.dtype)
            return jnp.where(count_ge(ref, cand) >= top_k, cand_u, tau_u)
        return lax.fori_loop(0, n_bits, search_bit, tau_u)

    searched = t0 + Q_BLOCK > top_k
    half_bits = jnp.where(searched, KEY_BITS // 2, 0)
    rounded = search_bits(sc16_ref, jnp.zeros((1, Q_BLOCK), jnp.int32), KEY_BITS - 1, half_bits)
    fits = count_ge(sc_ref, _order_key_to_float(rounded)) >= top_k
    prefix = jnp.where(fits, rounded, rounded - (1 << KEY_BITS // 2))
    tau_u = search_bits(sc_ref, prefix, KEY_BITS // 2 - 1, half_bits)
    tau_u = jnp.where(searched, tau_u, FLT_MAX_KEY)
    tau = _order_key_to_float(tau_u)

    n_ge = count_ge(sc_ref, tau)
    surplus = jnp.max(n_ge) > top_k

    @pl.when(jnp.logical_and(searched, surplus))
    def _():
        key_row = lax.broadcasted_iota(jnp.int32, (KEY_TILE, Q_BLOCK), 0)

        def tied_before(cutoff):
            return lambda tile, base: jnp.where(
                tile == tau, jnp.where(base + key_row < cutoff, 1.0, 0.0), 0.0)

        n_eq = count_tiles(sc_ref, tied_before(jnp.int32(sc_ref.shape[0])))
        keep = top_k - (n_ge - n_eq)
        n_cut_bits = sc_ref.shape[0].bit_length()

        def cutoff_bit(it, cutoff):
            cand = cutoff | lax.shift_left(jnp.int32(1), n_cut_bits - 1 - it)
            return jnp.where(count_tiles(sc_ref, tied_before(cand)) <= keep, cand, cutoff)

        cutoff = lax.fori_loop(0, n_cut_bits, cutoff_bit, jnp.zeros((1, Q_BLOCK), jnp.int32))
        below = jnp.where(tau == 0.0, -FLT_MIN, _order_key_to_float(tau_u - 1))

        def demote(kt, carry):
            base = pl.multiple_of(kt * KEY_TILE, KEY_TILE)
            tile = sc_ref[pl.ds(base, KEY_TILE), :]
            demoted = jnp.where(base + key_row < cutoff, tile, below)
            sc_ref[pl.ds(base, KEY_TILE), :] = jnp.where(tile == tau, demoted, tile)
            return carry

        lax.fori_loop(0, n_tiles, demote, 0)

    n_chains = H_ATT // ATT_HEADS_PER_CHAIN
    width = ATT_HEADS_PER_CHAIN * Q_BLOCK
    q_chains = []
    for c in range(n_chains):
        heads = range(c * ATT_HEADS_PER_CHAIN, (c + 1) * ATT_HEADS_PER_CHAIN)
        q_chains.append(jnp.concatenate([aq_ref[:, h * D_HEAD:(h + 1) * D_HEAD] for h in heads], axis=0))

    def attend_tile(kt, carry):
        base = pl.multiple_of(kt * KEY_TILE, KEY_TILE)
        bias = jnp.where(sc_ref[pl.ds(base, KEY_TILE), :] >= tau, 0.0, NEG_BIG)
        bias = jnp.concatenate([bias] * ATT_HEADS_PER_CHAIN, axis=1)
        kv_of = lambda c: c * ATT_HEADS_PER_CHAIN // GROUP
        k_tiles = [ak_ref[pl.ds(base, KEY_TILE), g * D_HEAD:(g + 1) * D_HEAD] for g in range(H_KV)]
        v_tiles = [av_ref[pl.ds(base, KEY_TILE), g * D_HEAD:(g + 1) * D_HEAD] for g in range(H_KV)]
        logits = [_dot_nt(k_tiles[kv_of(c)], q_chains[c]) + bias for c in range(n_chains)]
        stats = []
        for c in range(n_chains):
            m_old, l_old, _ = carry[c]
            m_new = jnp.maximum(m_old, jnp.max(logits[c], axis=0, keepdims=True))
            alpha = jnp.exp2(m_old - m_new)
            p = jnp.exp2(logits[c] - m_new)
            l_new = alpha * l_old + jnp.sum(p, axis=0, keepdims=True)
            stats.append((m_new, l_new, alpha, p.astype(BF16)))
        new = []
        for c in range(n_chains):
            m_new, l_new, alpha, p = stats[c]
            acc_new = alpha * carry[c][2] + _dot_tn(v_tiles[kv_of(c)], p)
            new.append((m_new, l_new, acc_new))
        return tuple(new)

    init = tuple((jnp.full((1, width), NEG_BIG, F32), jnp.zeros((1, width), F32),
                  jnp.zeros((D_HEAD, width), F32)) for _ in range(n_chains))
    final = lax.fori_loop(0, n_tiles, attend_tile, init)
    for c in range(n_chains):
        _, l_fin, acc_fin = final[c]
        out_t = acc_fin / l_fin
        for j in range(ATT_HEADS_PER_CHAIN):
            hcol = (c * ATT_HEADS_PER_CHAIN + j) * D_HEAD
            o_ref[:, hcol:hcol + D_HEAD] = out_t[:, j * Q_BLOCK:(j + 1) * Q_BLOCK].T.astype(o_ref.dtype)


def _sparse_attention(idx_out, rope_out, plain_out, batch, seq, weights_to_cast):
    n_tok = batch * seq
    nb = seq // Q_BLOCK
    n_steps = batch * nb
    top_k = min(MAX_TOPK, seq // 4)
    kv_w = H_KV * D_HEAD
    step = lambda b, q: (b * nb + q, 0)
    slabs = [pl.BlockSpec((w.shape[0] // n_steps, w.shape[1]), step) for w in weights_to_cast]
    outs = pl.pallas_call(
        functools.partial(_attn_kernel, top_k=top_k, n_cast=len(weights_to_cast)),
        grid=(batch, nb),
        in_specs=[
            pl.BlockSpec((Q_BLOCK, IDX_Q_COLS), step),
            pl.BlockSpec((seq, LANES), lambda b, q: (b, IDX_Q_COLS // LANES)),
            pl.BlockSpec((Q_BLOCK, LANES), lambda b, q: (b * nb + q, IDX_Q_COLS // LANES + 1)),
            pl.BlockSpec((Q_BLOCK, ATT_W), lambda b, q: (b * nb + q, 2 * RET_W // ATT_W)),
            pl.BlockSpec((seq, kv_w), lambda b, q: (b, (2 * RET_W + ATT_W) // kv_w)),
            pl.BlockSpec((seq, kv_w), lambda b, q: (b, 2 * RET_W // kv_w)),
        ] + slabs,
        out_specs=[pl.BlockSpec((Q_BLOCK, ATT_W), step)] + slabs,
        out_shape=[jax.ShapeDtypeStruct((n_tok, ATT_W), BF16)]
                  + [jax.ShapeDtypeStruct(w.shape, BF16) for w in weights_to_cast],
        scratch_shapes=[pltpu.VMEM((seq, Q_BLOCK), F32), pltpu.VMEM((seq, Q_BLOCK), BF16)],
        compiler_params=_params(("parallel", "arbitrary")),
        name="sparse_attention",
    )(idx_out, idx_out, idx_out, rope_out, rope_out, plain_out, *weights_to_cast)
    return outs[0], outs[1:]


def _out_proj_kernel(x_ref, ro_ref, ao_ref, wr_ref, wa_ref, o_ref):
    o_ref[...] = x_ref[...] + _dot(ro_ref[...], wr_ref[...]) + _dot(ao_ref[...], wa_ref[...])


def _out_proj(x2d, ro, ao, w_out, tm=512):
    n_tok = x2d.shape[0]
    tile = pl.BlockSpec((tm, D_MODEL), lambda i: (i, 0))
    half = pl.BlockSpec((tm, RET_W), lambda i: (i, 0))
    w_ret = pl.BlockSpec((RET_W, D_MODEL), lambda i: (0, 0))
    w_att = pl.BlockSpec((ATT_W, D_MODEL), lambda i: (RET_W // ATT_W, 0))
    return pl.pallas_call(
        _out_proj_kernel,
        grid=(n_tok // tm,),
        in_specs=[tile, half, half, w_ret, w_att],
        out_specs=tile,
        out_shape=jax.ShapeDtypeStruct((n_tok, D_MODEL), F32),
        compiler_params=_params(("parallel",)),
        name="out_proj",
    )(x2d, ro, ao, w_out, w_out)


def _layer(x2d, tables, batch, seq, ffn1_norm, ffn1_w_gate, ffn1_w_up, ffn1_w_down, mix_norm,
           w_in_groups, ret_norm, w_out, ffn2_norm, ffn2_w_gate, ffn2_w_up, ffn2_w_down, final_norm, last):
    cos_a, sin_a, cos_b, sin_b = tables
    w_rope, w_plain, w_idx = w_in_groups
    row = lambda g: g.reshape(1, -1).astype(F32)

    x1, h = _ffn(x2d, row(ffn1_norm), ffn1_w_gate, ffn1_w_up, ffn1_w_down,
                 row(mix_norm), emit_residual=True, tf=FFN_TF_F32)
    rope_out = _proj(h, w_rope, cos_a, sin_a, mode="rope128", tm=512, out_dtype=BF16)
    plain_out = _proj(h, w_plain, cos_a, sin_a, mode="plain", tm=512, out_dtype=BF16)
    idx_out = _proj(h, w_idx, cos_b, sin_b, mode="idx", tm=1024, out_dtype=BF16)
    ro = _retention(rope_out, plain_out, row(ret_norm), batch, seq)
    ao, (w_out16, wg16, wu16, wd16) = _sparse_attention(
        idx_out, rope_out, plain_out, batch, seq, [w_out, ffn2_w_gate, ffn2_w_up, ffn2_w_down])
    x2 = _out_proj(x1, ro, ao, w_out16)
    out = _ffn(x2, row(ffn2_norm), wg16, wu16, wd16, row(final_norm), emit_residual=not last,
               tf=FFN_TF_BF16)
    return out if last else out[0]


def kernel(x, positions, ffn1_norm, ffn1_w_gate, ffn1_w_up, ffn1_w_down, mix_norm, w_in, ret_norm,
           w_out, ffn2_norm, ffn2_w_gate, ffn2_w_up, ffn2_w_down, final_norm):
    batch, seq, _ = x.shape
    depth = w_in.shape[0]
    tables, w_in_groups = _prepare(positions, jnp.swapaxes(w_in, 1, 2))
    x2d = x.reshape(batch * seq, D_MODEL)
    for l in range(depth):
        last = l == depth - 1
        groups_l = [w[l] for w in w_in_groups]
        x2d = _layer(x2d, tables, batch, seq, ffn1_norm[l], ffn1_w_gate[l], ffn1_w_up[l], ffn1_w_down[l],
                     mix_norm[l], groups_l, ret_norm[l], w_out[l], ffn2_norm[l], ffn2_w_gate[l],
                     ffn2_w_up[l], ffn2_w_down[l], final_norm, last)
    return x2d.reshape(batch, seq, D_MODEL)
```

```python
import functools

import jax
import jax.numpy as jnp
from jax import lax
from jax.experimental import pallas as pl
from jax.experimental.pallas import tpu as pltpu

D_MODEL = 2048
H_RET = 8
DK_RET = 128
DV_RET = 128
RET_CHUNK = 128
H_ATT = 8
H_KV = 2
D_HEAD = 128
H_IDX = 16
D_IDX = 64
MAX_TOPK = 256
Q_BLOCK = 256
D_FF = 5632
ROPE_THETA = 10000.0
NORM_EPS = 1e-6

RET_W = H_RET * DV_RET
ATT_W = H_ATT * D_HEAD
GROUP = H_ATT // H_KV

LANES = 128
SUBLANES = 8
VMEM_LIMIT = 56 * 1024 * 1024

ROPE_COLS = 2 * H_RET * DK_RET + ATT_W + H_KV * D_HEAD
PLAIN_COLS = 2 * RET_W + H_KV * D_HEAD
IDX_Q_COLS = H_IDX * D_IDX
IDX_COLS = IDX_Q_COLS + 2 * LANES
AQ_CHUNK0 = 2 * H_RET * DK_RET // LANES

INT_MIN = -2 ** 31
NEG_BIG = -1e30
LOG2_E = 1.4426950408889634

F32 = jnp.float32
BF16 = jnp.bfloat16


def _dot(a, b):
    return jnp.dot(a, b, preferred_element_type=F32)


def _dot_nt(a, b):
    return lax.dot_general(a, b, (((1,), (1,)), ((), ())), preferred_element_type=F32)


def _dot_tn(a, b):
    return lax.dot_general(a, b, (((0,), (0,)), ((), ())), preferred_element_type=F32)


def _rmsnorm(xf, g):
    ms = jnp.mean(xf * xf, axis=-1, keepdims=True)
    return xf * lax.rsqrt(ms + NORM_EPS) * g


def _params(sem):
    return pltpu.CompilerParams(dimension_semantics=sem, vmem_limit_bytes=VMEM_LIMIT)


_IN_SIZES = (H_RET * DK_RET, H_RET * DK_RET, RET_W, RET_W, ATT_W, H_KV * D_HEAD, H_KV * D_HEAD,
             H_IDX * D_IDX, D_IDX, H_IDX)
W_IN_SPLITS = tuple(sum(_IN_SIZES[:n]) for n in range(len(_IN_SIZES) + 1))


def _prep_kernel(pos_ref, inv_ref, sgn_a_ref, sgn_b_ref, w_in_ref, w_down_ref,
                 cos_a_ref, sin_a_ref, cos_b_ref, sin_b_ref, w_rope_ref, w_plain_ref, w_idx_ref,
                 w_down16_ref):
    w_down16_ref[...] = w_down_ref[...].astype(BF16)

    ang = pos_ref[...].astype(F32) * inv_ref[...]
    lane = lax.broadcasted_iota(jnp.int32, ang.shape, 1)
    half, quarter = D_HEAD // 2, D_IDX // 2

    def table_a(t):
        return jnp.where(lane < half, t, pltpu.roll(t, half, axis=1))

    def table_b(t):
        upper = jnp.where(lane < half + quarter, t, pltpu.roll(t, quarter, axis=1))
        return jnp.where(lane >= half, upper, pltpu.roll(upper, half, axis=1))

    cos, sin = jnp.cos(ang), jnp.sin(ang)
    cos_a_ref[...] = table_a(cos)
    sin_a_ref[...] = table_a(sin) * sgn_a_ref[...]
    cos_b_ref[...] = table_b(cos)
    sin_b_ref[...] = table_b(sin) * sgn_b_ref[...]

    rq, rk, rv, rg, aq, ak, av, iq, ik, iw, end = W_IN_SPLITS
    for layer in range(w_in_ref.shape[0]):
        feats = lambda lo, hi: w_in_ref[layer, lo:hi, :].astype(BF16)
        w_rope_ref[layer, :rv - rq, :] = feats(rq, rv)
        w_rope_ref[layer, rv - rq:, :] = feats(aq, av)
        w_plain_ref[layer, :aq - rv, :] = feats(rv, aq)
        w_plain_ref[layer, aq - rv:, :] = feats(av, iq)
        w_idx_ref[layer, :ik - iq, :] = feats(iq, ik)
        key = feats(ik, iw)
        w_idx_ref[layer, ik - iq:ik - iq + D_IDX, :] = key
        w_idx_ref[layer, ik - iq + D_IDX:ik - iq + LANES, :] = key
        w_idx_ref[layer, ik - iq + LANES:ik - iq + LANES + H_IDX, :] = feats(iw, end)
        w_idx_ref[layer, ik - iq + LANES + H_IDX:, :] = jnp.zeros(
            (LANES - H_IDX, w_idx_ref.shape[2]), BF16)


def _prepare(positions, w_in_t, w_down):
    n_tok = positions.size
    tm = 1024
    n_steps = n_tok // tm
    depth, _, d_model = w_in_t.shape
    w_cols = d_model // n_steps
    lane = jnp.arange(LANES)

    def inv_freq(d):
        return ROPE_THETA ** (-jnp.arange(0, d, 2, dtype=F32) / d)

    unused = jnp.zeros((LANES - D_HEAD // 2 - D_IDX // 2,), F32)
    inv = jnp.concatenate([inv_freq(D_HEAD), inv_freq(D_IDX), unused])[None, :]

    def sign(d):
        return jnp.where(lane % d < d // 2, -1.0, 1.0).astype(F32)[None, :]

    row = pl.BlockSpec((1, LANES), lambda i: (0, 0))
    tab = pl.BlockSpec((tm, LANES), lambda i: (i, 0))
    out = jax.ShapeDtypeStruct((n_tok, LANES), F32)
    slab = lambda n_feats: pl.BlockSpec((depth, n_feats, w_cols), lambda i: (0, 0, i))
    w_out = lambda n_feats: jax.ShapeDtypeStruct((depth, n_feats, d_model), BF16)
    down_slab = pl.BlockSpec((depth, w_down.shape[1] // n_steps, d_model), lambda i: (0, i, 0))
    outs = pl.pallas_call(
        _prep_kernel,
        grid=(n_steps,),
        in_specs=[pl.BlockSpec((tm, 1), lambda i: (i, 0)), row, row, row, slab(w_in_t.shape[1]),
                  down_slab],
        out_specs=[tab, tab, tab, tab, slab(ROPE_COLS), slab(PLAIN_COLS), slab(IDX_COLS), down_slab],
        out_shape=[out, out, out, out, w_out(ROPE_COLS), w_out(PLAIN_COLS), w_out(IDX_COLS),
                   jax.ShapeDtypeStruct(w_down.shape, BF16)],
        compiler_params=_params(("parallel",)),
        name="prepare",
    )(positions.reshape(n_tok, 1), inv, sign(D_HEAD), sign(D_IDX), w_in_t, w_down)
    return outs[:4], outs[4:7], outs[7]


FFN_ROW_CHUNK = 128


def _ffn_kernel(x_ref, g_ref, wg_ref, wu_ref, wd_ref, g2_ref, *refs, emit_residual):
    if emit_residual:
        res_ref, normed_ref, xn_ref = refs
        acc_ref = res_ref
    else:
        normed_ref, xn_ref = refs
        acc_ref = normed_ref
    f = pl.program_id(1)
    n_row_chunks = x_ref.shape[0] // FFN_ROW_CHUNK

    def rows_of(r):
        return pl.ds(pl.multiple_of(r * FFN_ROW_CHUNK, FFN_ROW_CHUNK), FFN_ROW_CHUNK)

    @pl.when(f == 0)
    def _():
        def prologue(r, carry):
            xf = x_ref[rows_of(r), :]
            xn_ref[rows_of(r), :] = _rmsnorm(xf, g_ref[...]).astype(BF16)
            acc_ref[rows_of(r), :] = 2.0 * xf
            return carry
        lax.fori_loop(0, n_row_chunks, prologue, 0)

    xn = xn_ref[...]
    a = _dot(xn, wg_ref[...].astype(BF16))
    b = _dot(xn, wu_ref[...].astype(BF16))
    hidden = (a * jax.nn.sigmoid(a) * b).astype(BF16)
    acc_ref[...] += _dot(hidden, wd_ref[...].astype(BF16))

    @pl.when(f == pl.num_programs(1) - 1)
    def _():
        def epilogue(r, carry):
            y = 0.5 * acc_ref[rows_of(r), :]
            if emit_residual:
                res_ref[rows_of(r), :] = y
            normed_ref[rows_of(r), :] = _rmsnorm(y, g2_ref[...]).astype(normed_ref.dtype)
            return carry
        lax.fori_loop(0, n_row_chunks, epilogue, 0)


FFN_TM = 1024
FFN_TF_F32 = 256
FFN_TF_BF16 = 512


def _ffn(x2d, g, wg, wu, wd, g2, *, emit_residual, tf, tm=FFN_TM):
    n_tok = x2d.shape[0]
    tile_map = lambda i, f: (i, 0)
    x_tile = pl.BlockSpec((tm, D_MODEL), tile_map)
    out_tile = pl.BlockSpec((tm, D_MODEL), tile_map)
    normed_tile = out_tile if wg.dtype == BF16 else pl.BlockSpec(
        (tm, D_MODEL), tile_map, pipeline_mode=pl.Buffered(1))
    row = pl.BlockSpec((1, D_MODEL), lambda i, f: (0, 0))
    if emit_residual:
        out_specs = [out_tile, normed_tile]
        out_shape = [jax.ShapeDtypeStruct((n_tok, D_MODEL), F32),
                     jax.ShapeDtypeStruct((n_tok, D_MODEL), BF16)]
    else:
        out_specs = out_tile
        out_shape = jax.ShapeDtypeStruct((n_tok, D_MODEL), F32)
    return pl.pallas_call(
        functools.partial(_ffn_kernel, emit_residual=emit_residual),
        grid=(n_tok // tm, D_FF // tf),
        in_specs=[x_tile, row,
                  pl.BlockSpec((D_MODEL, tf), lambda i, f: (0, f)),
                  pl.BlockSpec((D_MODEL, tf), lambda i, f: (0, f)),
                  pl.BlockSpec((tf, D_MODEL), lambda i, f: (f, 0)),
                  row],
        out_specs=out_specs,
        out_shape=out_shape,
        scratch_shapes=[pltpu.VMEM((tm, D_MODEL), BF16)],
        compiler_params=_params(("parallel", "arbitrary")),
        name="ffn_residual" if emit_residual else "ffn_final",
    )(x2d, g, wg, wu, wd, g2)


def _rope128(y, cos, sin):
    return y * cos + pltpu.roll(y, D_HEAD // 2, axis=1) * sin


def _rope64(y, cos, sin):
    lane = lax.broadcasted_iota(jnp.int32, y.shape, 1)
    first_half = (lane & (D_IDX - 1)) < D_IDX // 2
    partner = jnp.where(first_half,
                        pltpu.roll(y, LANES - D_IDX // 2, axis=1),
                        pltpu.roll(y, D_IDX // 2, axis=1))
    return y * cos + partner * sin


def _proj_kernel(h_ref, w_ref, cos_ref, sin_ref, o_ref, *, mode):
    y = _dot_nt(h_ref[...], w_ref[...])
    n_chunks = y.shape[1] // LANES
    if mode == "plain":
        o_ref[...] = y.astype(o_ref.dtype)
        return
    cos = cos_ref[...]
    sin = sin_ref[...]
    for c in range(n_chunks):
        yc = y[:, c * LANES:(c + 1) * LANES]
        if mode == "rope128":
            yc = _rope128(yc, cos, sin)
            if AQ_CHUNK0 <= c < AQ_CHUNK0 + H_ATT:
                yc = yc * (LOG2_E * D_HEAD ** -0.5)
        elif c < IDX_Q_COLS // LANES:
            yc = _rope64(yc, cos, sin) * (D_IDX ** -0.5)
        elif c == IDX_Q_COLS // LANES:
            yc = _rope64(yc, cos, sin)
        else:
            yc = yc * (H_IDX ** -0.5)
        o_ref[:, c * LANES:(c + 1) * LANES] = yc.astype(o_ref.dtype)


def _proj(h, w, cos, sin, *, mode, tm, out_dtype):
    n_tok, n_cols = h.shape[0], w.shape[0]
    tab = pl.BlockSpec((tm, LANES), lambda i: (i, 0))
    return pl.pallas_call(
        functools.partial(_proj_kernel, mode=mode),
        grid=(n_tok // tm,),
        in_specs=[pl.BlockSpec((tm, D_MODEL), lambda i: (i, 0)),
                  pl.BlockSpec((n_cols, D_MODEL), lambda i: (0, 0), pipeline_mode=pl.Buffered(1)),
                  tab, tab],
        out_specs=pl.BlockSpec((tm, n_cols), lambda i: (i, 0)),
        out_shape=jax.ShapeDtypeStruct((n_tok, n_cols), out_dtype),
        compiler_params=_params(("parallel",)),
        name="proj_" + mode,
    )(h, w, cos, sin)


RET_UNROLL = 8
RET_HEADS_PER_STEP = 2


def _retention_kernel(lg_ref, q_ref, k_ref, v_ref, g_ref, rn_ref, o_ref):
    C = RET_CHUNK
    n_chunks = q_ref.shape[0] // C
    row = lax.broadcasted_iota(jnp.int32, (C, C), 0).astype(F32)
    col = lax.broadcasted_iota(jnp.int32, (C, C), 1).astype(F32)
    diff = row - col
    scale = DK_RET ** -0.5
    heads = []
    for j in range(RET_HEADS_PER_STEP):
        lg = lg_ref[j]
        heads.append(dict(
            lanes=slice(j * LANES, (j + 1) * LANES),
            decay=jnp.where(diff >= 0, jnp.exp(jnp.maximum(diff, 0.0) * lg), 0.0) * scale,
            k_dec=jnp.exp((C - 1 - row) * lg) * scale,
            q_dec=jnp.exp((row + 1) * lg),
            g_chunk=jnp.exp(C * lg)))

    def body(n, states):
        sl = pl.ds(pl.multiple_of(n * C, C), C)
        new_states = []
        for hd, state in zip(heads, states):
            qc = q_ref[sl, hd["lanes"]]
            kc = k_ref[sl, hd["lanes"]]
            vc = v_ref[sl, hd["lanes"]]
            s = _dot_nt(qc, kc) * hd["decay"]
            intra = _dot(s.astype(BF16), vc)
            cross = _dot(qc, state.astype(BF16)) * hd["q_dec"]
            kv = _dot_tn((kc.astype(F32) * hd["k_dec"]).astype(BF16), vc)
            o = intra + cross
            mu = jnp.mean(o, axis=-1, keepdims=True)
            d = o - mu
            var = jnp.mean(d * d, axis=-1, keepdims=True)
            y = d * lax.rsqrt(var + NORM_EPS) * rn_ref[:, hd["lanes"]]
            gate = g_ref[sl, hd["lanes"]].astype(F32)
            o_ref[sl, hd["lanes"]] = (y * (gate * jax.nn.sigmoid(gate))).astype(o_ref.dtype)
            new_states.append(state * hd["g_chunk"] + kv)
        return tuple(new_states)

    init = tuple(jnp.zeros((DK_RET, DV_RET), F32) for _ in heads)
    lax.fori_loop(0, n_chunks, body, init, unroll=RET_UNROLL)


def _retention(rope_out, plain_out, ret_norm, batch, seq):
    n_tok = batch * seq
    lg = jnp.log1p(-jnp.exp2(-5.0 - jnp.arange(H_RET, dtype=F32)))
    lg = jnp.broadcast_to(lg[:, None, None], (H_RET, 1, LANES))
    width = RET_HEADS_PER_STEP * LANES
    n_groups = H_RET // RET_HEADS_PER_STEP
    heads = lambda off: pl.BlockSpec((seq, width), lambda b, h: (b, off + h))
    return pl.pallas_call(
        _retention_kernel,
        grid=(batch, n_groups),
        in_specs=[pl.BlockSpec((RET_HEADS_PER_STEP, 1, LANES), lambda b, h: (h, 0, 0)),
                  heads(0), heads(n_groups),
                  heads(0), heads(n_groups),
                  pl.BlockSpec((1, width), lambda b, h: (0, h))],
        out_specs=heads(0),
        out_shape=jax.ShapeDtypeStruct((n_tok, RET_W), BF16),
        compiler_params=_params(("parallel", "parallel")),
        name="retention",
    )(lg, rope_out, rope_out, plain_out, plain_out, ret_norm)


KEY_TILE = 512
SCORE_ROWS = 128
FLT_MAX_KEY = 0x00800000
KEY_BITS = 32
FLT_MIN = 1.1754943508222875e-38
ATT_HEADS_PER_CHAIN = 2
ATT_LOOKAHEAD = 2


def _order_key_to_float(u):
    s = u ^ INT_MIN
    return pltpu.bitcast(s ^ ((s >> 31) & jnp.int32(0x7FFFFFFF)), F32)


def _attn_kernel(iq_ref, ik_ref, iw_ref, aq_ref, ak_ref, av_ref, *rest, top_k, n_cast):
    o_ref, sc_ref, sc16_ref = rest[n_cast], rest[-2], rest[-1]
    for w32_ref, w16_ref in zip(rest[:n_cast], rest[n_cast + 1:-2]):
        w16_ref[...] = w32_ref[...].astype(BF16)

    qb = pl.program_id(1)
    t0 = qb * Q_BLOCK
    n_tiles = (t0 + Q_BLOCK + KEY_TILE - 1) // KEY_TILE

    w_t = iw_ref[...].astype(F32).T
    w_rows = [w_t[h:h + 1, :] for h in range(H_IDX)]

    lane = lax.broadcasted_iota(jnp.int32, (Q_BLOCK, LANES), 1)
    low = lane < D_IDX
    q_pairs = []
    for c in range(IDX_Q_COLS // LANES):
        qc = iq_ref[:, c * LANES:(c + 1) * LANES]
        zero = jnp.zeros_like(qc)
        q_pairs.append(jnp.concatenate([jnp.where(low, qc, zero), jnp.where(low, zero, qc)], axis=0))

    q_pos = t0 + lax.broadcasted_iota(jnp.int32, (SCORE_ROWS, Q_BLOCK), 1)
    key_off = lax.broadcasted_iota(jnp.int32, (SCORE_ROWS, Q_BLOCK), 0)

    def score_tile(kt, carry):
        for s in range(KEY_TILE // SCORE_ROWS):
            base = pl.multiple_of(kt * KEY_TILE + s * SCORE_ROWS, SCORE_ROWS)
            kk = ik_ref[pl.ds(base, SCORE_ROWS), :]
            acc = jnp.zeros((SCORE_ROWS, Q_BLOCK), F32)
            for c, qp in enumerate(q_pairs):
                z = _dot_nt(kk, qp)
                acc = acc + jnp.maximum(z[:, :Q_BLOCK], 0.0) * w_rows[2 * c]
                acc = acc + jnp.maximum(z[:, Q_BLOCK:], 0.0) * w_rows[2 * c + 1]
            causal = base + key_off <= q_pos
            score = jnp.where(causal, acc, -jnp.inf)
            sc_ref[pl.ds(base, SCORE_ROWS), :] = score
            sc16_ref[pl.ds(base, SCORE_ROWS), :] = score.astype(BF16)
        return carry

    lax.fori_loop(0, n_tiles, score_tile, 0)

    def count_tiles(ref, indicator):
        rows = SUBLANES * 4 // ref.dtype.itemsize

        def count_tile(kt, cnt):
            base = pl.multiple_of(kt * KEY_TILE, KEY_TILE)
            part = indicator(ref[pl.ds(base, KEY_TILE), :], base)
            part = part.reshape(KEY_TILE // rows, rows, Q_BLOCK)
            while part.shape[0] > 1:
                half = part.shape[0] // 2
                part = part[:half] + part[half:]
            return cnt + part[0].astype(F32)

        cnt = lax.fori_loop(0, n_tiles, count_tile, jnp.zeros((rows, Q_BLOCK), F32))
        return jnp.sum(cnt, axis=0, keepdims=True)

    def count_ge(ref, cand):
        one, zero = jnp.ones((), ref.dtype), jnp.zeros((), ref.dtype)
        return count_tiles(ref, lambda tile, base: jnp.where(tile >= cand, one, zero))

    def search_bits(ref, tau_u, first_bit, n_bits):
        def search_bit(it, tau_u):
            cand_u = tau_u | lax.shift_left(jnp.int32(1), first_bit - it)
            cand = _order_key_to_float(cand_u).astype(ref.dtype)
            return jnp.where(count_ge(ref, cand) >= top_k, cand_u, tau_u)
        return lax.fori_loop(0, n_bits, search_bit, tau_u)

    searched = t0 + Q_BLOCK > top_k
    half_bits = jnp.where(searched, KEY_BITS // 2, 0)
    rounded = search_bits(sc16_ref, jnp.zeros((1, Q_BLOCK), jnp.int32), KEY_BITS - 1, half_bits)
    fits = count_ge(sc_ref, _order_key_to_float(rounded)) >= top_k
    prefix = jnp.where(fits, rounded, rounded - (1 << KEY_BITS // 2))
    tau_u = search_bits(sc_ref, prefix, KEY_BITS // 2 - 1, half_bits)
    tau_u = jnp.where(searched, tau_u, FLT_MAX_KEY)
    tau = _order_key_to_float(tau_u)

    n_ge = count_ge(sc_ref, tau)
    surplus = jnp.max(n_ge) > top_k

    @pl.when(jnp.logical_and(searched, surplus))
    def _():
        key_row = lax.broadcasted_iota(jnp.int32, (KEY_TILE, Q_BLOCK), 0)

        def tied_before(cutoff):
            return lambda tile, base: jnp.where(
                tile == tau, jnp.where(base + key_row < cutoff, 1.0, 0.0), 0.0)

        n_eq = count_tiles(sc_ref, tied_before(jnp.int32(sc_ref.shape[0])))
        keep = top_k - (n_ge - n_eq)
        n_cut_bits = sc_ref.shape[0].bit_length()

        def cutoff_bit(it, cutoff):
            cand = cutoff | lax.shift_left(jnp.int32(1), n_cut_bits - 1 - it)
            return jnp.where(count_tiles(sc_ref, tied_before(cand)) <= keep, cand, cutoff)

        cutoff = lax.fori_loop(0, n_cut_bits, cutoff_bit, jnp.zeros((1, Q_BLOCK), jnp.int32))
        below = jnp.where(tau == 0.0, -FLT_MIN, _order_key_to_float(tau_u - 1))

        def demote(kt, carry):
            base = pl.multiple_of(kt * KEY_TILE, KEY_TILE)
            tile = sc_ref[pl.ds(base, KEY_TILE), :]
            demoted = jnp.where(base + key_row < cutoff, tile, below)
            sc_ref[pl.ds(base, KEY_TILE), :] = jnp.where(tile == tau, demoted, tile)
            return carry

        lax.fori_loop(0, n_tiles, demote, 0)

    n_chains = H_ATT // ATT_HEADS_PER_CHAIN
    width = ATT_HEADS_PER_CHAIN * Q_BLOCK
    q_chains = []
    for c in range(n_chains):
        heads = range(c * ATT_HEADS_PER_CHAIN, (c + 1) * ATT_HEADS_PER_CHAIN)
        q_chains.append(jnp.concatenate([aq_ref[:, h * D_HEAD:(h + 1) * D_HEAD] for h in heads], axis=0))

    def attend_tile(kt, carry):
        base = pl.multiple_of(kt * KEY_TILE, KEY_TILE)
        bias = jnp.where(sc_ref[pl.ds(base, KEY_TILE), :] >= tau, 0.0, NEG_BIG)
        bias = jnp.concatenate([bias] * ATT_HEADS_PER_CHAIN, axis=1)
        kv_of = lambda c: c * ATT_HEADS_PER_CHAIN // GROUP
        k_tiles = [ak_ref[pl.ds(base, KEY_TILE), g * D_HEAD:(g + 1) * D_HEAD] for g in range(H_KV)]
        v_tiles = [av_ref[pl.ds(base, KEY_TILE), g * D_HEAD:(g + 1) * D_HEAD] for g in range(H_KV)]
        score = lambda c: _dot_nt(k_tiles[kv_of(c)], q_chains[c]) + bias
        logits = [score(c) for c in range(min(ATT_LOOKAHEAD, n_chains))]
        new = []
        for c in range(n_chains):
            m_old, l_old, acc_old = carry[c]
            m_new = jnp.maximum(m_old, jnp.max(logits[c], axis=0, keepdims=True))
            alpha = jnp.exp2(m_old - m_new)
            p = jnp.exp2(logits[c] - m_new)
            l_new = alpha * l_old + jnp.sum(p, axis=0, keepdims=True)
            if c + ATT_LOOKAHEAD < n_chains:
                logits.append(score(c + ATT_LOOKAHEAD))
            acc_new = alpha * acc_old + _dot_tn(v_tiles[kv_of(c)], p.astype(BF16))
            new.append((m_new, l_new, acc_new))
        return tuple(new)

    init = tuple((jnp.full((1, width), NEG_BIG, F32), jnp.zeros((1, width), F32),
                  jnp.zeros((D_HEAD, width), F32)) for _ in range(n_chains))
    final = lax.fori_loop(0, n_tiles, attend_tile, init)
    for c in range(n_chains):
        _, l_fin, acc_fin = final[c]
        out_t = acc_fin / l_fin
        for j in range(ATT_HEADS_PER_CHAIN):
            hcol = (c * ATT_HEADS_PER_CHAIN + j) * D_HEAD
            o_ref[:, hcol:hcol + D_HEAD] = out_t[:, j * Q_BLOCK:(j + 1) * Q_BLOCK].T.astype(o_ref.dtype)


def _sparse_attention(idx_out, rope_out, plain_out, batch, seq, weights_to_cast):
    n_tok = batch * seq
    nb = seq // Q_BLOCK
    n_steps = batch * nb
    top_k = min(MAX_TOPK, seq // 4)
    kv_w = H_KV * D_HEAD
    step = lambda b, q: (b * nb + q, 0)
    slabs = [pl.BlockSpec((w.shape[0] // n_steps, w.shape[1]), step) for w in weights_to_cast]
    outs = pl.pallas_call(
        functools.partial(_attn_kernel, top_k=top_k, n_cast=len(weights_to_cast)),
        grid=(batch, nb),
        in_specs=[
            pl.BlockSpec((Q_BLOCK, IDX_Q_COLS), step),
            pl.BlockSpec((seq, LANES), lambda b, q: (b, IDX_Q_COLS // LANES)),
            pl.BlockSpec((Q_BLOCK, LANES), lambda b, q: (b * nb + q, IDX_Q_COLS // LANES + 1)),
            pl.BlockSpec((Q_BLOCK, ATT_W), lambda b, q: (b * nb + q, 2 * RET_W // ATT_W)),
            pl.BlockSpec((seq, kv_w), lambda b, q: (b, (2 * RET_W + ATT_W) // kv_w)),
            pl.BlockSpec((seq, kv_w), lambda b, q: (b, 2 * RET_W // kv_w)),
        ] + slabs,
        out_specs=[pl.BlockSpec((Q_BLOCK, ATT_W), step)] + slabs,
        out_shape=[jax.ShapeDtypeStruct((n_tok, ATT_W), BF16)]
                  + [jax.ShapeDtypeStruct(w.shape, BF16) for w in weights_to_cast],
        scratch_shapes=[pltpu.VMEM((seq, Q_BLOCK), F32), pltpu.VMEM((seq, Q_BLOCK), BF16)],
        compiler_params=_params(("parallel", "arbitrary")),
        name="sparse_attention",
    )(idx_out, idx_out, idx_out, rope_out, rope_out, plain_out, *weights_to_cast)
    return outs[0], outs[1:]


def _out_proj_kernel(x_ref, ro_ref, ao_ref, wr_ref, wa_ref, o_ref):
    o_ref[...] = x_ref[...] + _dot(ro_ref[...], wr_ref[...]) + _dot(ao_ref[...], wa_ref[...])


def _out_proj(x2d, ro, ao, w_out, tm=512):
    n_tok = x2d.shape[0]
    tile = pl.BlockSpec((tm, D_MODEL), lambda i: (i, 0))
    half = pl.BlockSpec((tm, RET_W), lambda i: (i, 0))
    w_ret = pl.BlockSpec((RET_W, D_MODEL), lambda i: (0, 0))
    w_att = pl.BlockSpec((ATT_W, D_MODEL), lambda i: (RET_W // ATT_W, 0))
    return pl.pallas_call(
        _out_proj_kernel,
        grid=(n_tok // tm,),
        in_specs=[tile, half, half, w_ret, w_att],
        out_specs=tile,
        out_shape=jax.ShapeDtypeStruct((n_tok, D_MODEL), F32),
        compiler_params=_params(("parallel",)),
        name="out_proj",
    )(x2d, ro, ao, w_out, w_out)


def _layer(x2d, tables, batch, seq, ffn1_norm, ffn1_w_gate, ffn1_w_up, ffn1_w_down, mix_norm,
           w_in_groups, ret_norm, w_out, ffn2_norm, ffn2_w_gate, ffn2_w_up, ffn2_w_down, final_norm, last):
    cos_a, sin_a, cos_b, sin_b = tables
    w_rope, w_plain, w_idx = w_in_groups
    row = lambda g: g.reshape(1, -1).astype(F32)

    x1, h = _ffn(x2d, row(ffn1_norm), ffn1_w_gate, ffn1_w_up, ffn1_w_down,
                 row(mix_norm), emit_residual=True, tf=FFN_TF_F32)
    rope_out = _proj(h, w_rope, cos_a, sin_a, mode="rope128", tm=512, out_dtype=BF16)
    plain_out = _proj(h, w_plain, cos_a, sin_a, mode="plain", tm=512, out_dtype=BF16)
    idx_out = _proj(h, w_idx, cos_b, sin_b, mode="idx", tm=1024, out_dtype=BF16)
    ro = _retention(rope_out, plain_out, row(ret_norm), batch, seq)
    ao, (w_out16, wg16, wu16, wd16) = _sparse_attention(
        idx_out, rope_out, plain_out, batch, seq, [w_out, ffn2_w_gate, ffn2_w_up, ffn2_w_down])
    x2 = _out_proj(x1, ro, ao, w_out16)
    out = _ffn(x2, row(ffn2_norm), wg16, wu16, wd16, row(final_norm), emit_residual=not last,
               tf=FFN_TF_BF16)
    return out if last else out[0]


def kernel(x, positions, ffn1_norm, ffn1_w_gate, ffn1_w_up, ffn1_w_down, mix_norm, w_in, ret_norm,
           w_out, ffn2_norm, ffn2_w_gate, ffn2_w_up, ffn2_w_down, final_norm):
    batch, seq, _ = x.shape
    depth = w_in.shape[0]
    tables, w_in_groups, ffn1_w_down16 = _prepare(positions, jnp.swapaxes(w_in, 1, 2), ffn1_w_down)
    x2d = x.reshape(batch * seq, D_MODEL)
    for l in range(depth):
        last = l == depth - 1
        groups_l = [w[l] for w in w_in_groups]
        x2d = _layer(x2d, tables, batch, seq, ffn1_norm[l], ffn1_w_gate[l], ffn1_w_up[l], ffn1_w_down16[l],
                     mix_norm[l], groups_l, ret_norm[l], w_out[l], ffn2_norm[l], ffn2_w_gate[l],
                     ffn2_w_up[l], ffn2_w_down[l], final_norm, last)
    return x2d.reshape(batch, seq, D_MODEL)
```

```python
import functools

import jax
import jax.numpy as jnp
from jax import lax
from jax.experimental import pallas as pl
from jax.experimental.pallas import tpu as pltpu

D_MODEL = 2048
H_RET = 8
DK_RET = 128
DV_RET = 128
RET_CHUNK = 128
H_ATT = 8
H_KV = 2
D_HEAD = 128
H_IDX = 16
D_IDX = 64
MAX_TOPK = 256
Q_BLOCK = 256
D_FF = 5632
ROPE_THETA = 10000.0
NORM_EPS = 1e-6

RET_W = H_RET * DV_RET
ATT_W = H_ATT * D_HEAD
GROUP = H_ATT // H_KV

LANES = 128
SUBLANES = 8
VMEM_LIMIT = 56 * 1024 * 1024

ROPE_COLS = 2 * H_RET * DK_RET + ATT_W + H_KV * D_HEAD
PLAIN_COLS = 2 * RET_W + H_KV * D_HEAD
IDX_Q_COLS = H_IDX * D_IDX
IDX_COLS = IDX_Q_COLS + 2 * LANES
AQ_CHUNK0 = 2 * H_RET * DK_RET // LANES

INT_MIN = -2 ** 31
NEG_BIG = -1e30
LOG2_E = 1.4426950408889634

F32 = jnp.float32
BF16 = jnp.bfloat16


def _dot(a, b):
    return jnp.dot(a, b, preferred_element_type=F32)


def _dot_nt(a, b):
    return lax.dot_general(a, b, (((1,), (1,)), ((), ())), preferred_element_type=F32)


def _dot_tn(a, b):
    return lax.dot_general(a, b, (((0,), (0,)), ((), ())), preferred_element_type=F32)


def _rmsnorm(xf, g):
    ms = jnp.mean(xf * xf, axis=-1, keepdims=True)
    return xf * lax.rsqrt(ms + NORM_EPS) * g


def _params(sem):
    return pltpu.CompilerParams(dimension_semantics=sem, vmem_limit_bytes=VMEM_LIMIT)


_IN_SIZES = (H_RET * DK_RET, H_RET * DK_RET, RET_W, RET_W, ATT_W, H_KV * D_HEAD, H_KV * D_HEAD,
             H_IDX * D_IDX, D_IDX, H_IDX)
W_IN_SPLITS = tuple(sum(_IN_SIZES[:n]) for n in range(len(_IN_SIZES) + 1))


def _prep_kernel(pos_ref, inv_ref, sgn_a_ref, sgn_b_ref, w_in_ref,
                 cos_a_ref, sin_a_ref, cos_b_ref, sin_b_ref, w_rope_ref, w_plain_ref, w_idx_ref):
    ang = pos_ref[...].astype(F32) * inv_ref[...]
    lane = lax.broadcasted_iota(jnp.int32, ang.shape, 1)
    half, quarter = D_HEAD // 2, D_IDX // 2

    def table_a(t):
        return jnp.where(lane < half, t, pltpu.roll(t, half, axis=1))

    def table_b(t):
        upper = jnp.where(lane < half + quarter, t, pltpu.roll(t, quarter, axis=1))
        return jnp.where(lane >= half, upper, pltpu.roll(upper, half, axis=1))

    cos, sin = jnp.cos(ang), jnp.sin(ang)
    cos_a_ref[...] = table_a(cos)
    sin_a_ref[...] = table_a(sin) * sgn_a_ref[...]
    cos_b_ref[...] = table_b(cos)
    sin_b_ref[...] = table_b(sin) * sgn_b_ref[...]

    rq, rk, rv, rg, aq, ak, av, iq, ik, iw, end = W_IN_SPLITS
    for layer in range(w_in_ref.shape[0]):
        feats = lambda lo, hi: w_in_ref[layer, lo:hi, :].astype(BF16)
        w_rope_ref[layer, :rv - rq, :] = feats(rq, rv)
        w_rope_ref[layer, rv - rq:, :] = feats(aq, av)
        w_plain_ref[layer, :aq - rv, :] = feats(rv, aq)
        w_plain_ref[layer, aq - rv:, :] = feats(av, iq)
        w_idx_ref[layer, :ik - iq, :] = feats(iq, ik)
        key = feats(ik, iw)
        w_idx_ref[layer, ik - iq:ik - iq + D_IDX, :] = key
        w_idx_ref[layer, ik - iq + D_IDX:ik - iq + LANES, :] = key
        w_idx_ref[layer, ik - iq + LANES:ik - iq + LANES + H_IDX, :] = feats(iw, end)
        w_idx_ref[layer, ik - iq + LANES + H_IDX:, :] = jnp.zeros(
            (LANES - H_IDX, w_idx_ref.shape[2]), BF16)


def _prepare(positions, w_in_t):
    n_tok = positions.size
    tm = 1024
    n_steps = n_tok // tm
    depth, _, d_model = w_in_t.shape
    w_cols = d_model // n_steps
    lane = jnp.arange(LANES)

    def inv_freq(d):
        return ROPE_THETA ** (-jnp.arange(0, d, 2, dtype=F32) / d)

    unused = jnp.zeros((LANES - D_HEAD // 2 - D_IDX // 2,), F32)
    inv = jnp.concatenate([inv_freq(D_HEAD), inv_freq(D_IDX), unused])[None, :]

    def sign(d):
        return jnp.where(lane % d < d // 2, -1.0, 1.0).astype(F32)[None, :]

    row = pl.BlockSpec((1, LANES), lambda i: (0, 0))
    tab = pl.BlockSpec((tm, LANES), lambda i: (i, 0))
    out = jax.ShapeDtypeStruct((n_tok, LANES), F32)
    slab = lambda n_feats: pl.BlockSpec((depth, n_feats, w_cols), lambda i: (0, 0, i))
    w_out = lambda n_feats: jax.ShapeDtypeStruct((depth, n_feats, d_model), BF16)
    outs = pl.pallas_call(
        _prep_kernel,
        grid=(n_steps,),
        in_specs=[pl.BlockSpec((tm, 1), lambda i: (i, 0)), row, row, row, slab(w_in_t.shape[1])],
        out_specs=[tab, tab, tab, tab, slab(ROPE_COLS), slab(PLAIN_COLS), slab(IDX_COLS)],
        out_shape=[out, out, out, out, w_out(ROPE_COLS), w_out(PLAIN_COLS), w_out(IDX_COLS)],
        compiler_params=_params(("parallel",)),
        name="prepare",
    )(positions.reshape(n_tok, 1), inv, sign(D_HEAD), sign(D_IDX), w_in_t)
    return outs[:4], outs[4:]


FFN_ROW_CHUNK = 128


def _ffn_kernel(x_ref, g_ref, wg_ref, wu_ref, wd_ref, g2_ref, *refs, emit_residual):
    if emit_residual:
        res_ref, normed_ref, xn_ref = refs
        acc_ref = res_ref
    else:
        normed_ref, xn_ref = refs
        acc_ref = normed_ref
    f = pl.program_id(1)
    n_row_chunks = x_ref.shape[0] // FFN_ROW_CHUNK

    def rows_of(r):
        return pl.ds(pl.multiple_of(r * FFN_ROW_CHUNK, FFN_ROW_CHUNK), FFN_ROW_CHUNK)

    @pl.when(f == 0)
    def _():
        def prologue(r, carry):
            xf = x_ref[rows_of(r), :]
            xn_ref[rows_of(r), :] = _rmsnorm(xf, g_ref[...]).astype(BF16)
            acc_ref[rows_of(r), :] = 2.0 * xf
            return carry
        lax.fori_loop(0, n_row_chunks, prologue, 0)

    xn = xn_ref[...]
    a = _dot(xn, wg_ref[...].astype(BF16))
    b = _dot(xn, wu_ref[...].astype(BF16))
    hidden = (a * jax.nn.sigmoid(a) * b).astype(BF16)
    acc_ref[...] += _dot(hidden, wd_ref[...].astype(BF16))

    @pl.when(f == pl.num_programs(1) - 1)
    def _():
        def epilogue(r, carry):
            y = 0.5 * acc_ref[rows_of(r), :]
            if emit_residual:
                res_ref[rows_of(r), :] = y
            normed_ref[rows_of(r), :] = _rmsnorm(y, g2_ref[...]).astype(normed_ref.dtype)
            return carry
        lax.fori_loop(0, n_row_chunks, epilogue, 0)


FFN_TM = 1024
FFN_TF_F32 = 256
FFN_TF_BF16 = 512


def _ffn(x2d, g, wg, wu, wd, g2, *, emit_residual, tf, tm=FFN_TM):
    n_tok = x2d.shape[0]
    tile_map = lambda i, f: (i, 0)
    x_tile = pl.BlockSpec((tm, D_MODEL), tile_map)
    out_tile = pl.BlockSpec((tm, D_MODEL), tile_map)
    normed_tile = out_tile if wg.dtype == BF16 else pl.BlockSpec(
        (tm, D_MODEL), tile_map, pipeline_mode=pl.Buffered(1))
    row = pl.BlockSpec((1, D_MODEL), lambda i, f: (0, 0))
    if emit_residual:
        out_specs = [out_tile, normed_tile]
        out_shape = [jax.ShapeDtypeStruct((n_tok, D_MODEL), F32),
                     jax.ShapeDtypeStruct((n_tok, D_MODEL), BF16)]
    else:
        out_specs = out_tile
        out_shape = jax.ShapeDtypeStruct((n_tok, D_MODEL), F32)
    return pl.pallas_call(
        functools.partial(_ffn_kernel, emit_residual=emit_residual),
        grid=(n_tok // tm, D_FF // tf),
        in_specs=[x_tile, row,
                  pl.BlockSpec((D_MODEL, tf), lambda i, f: (0, f)),
                  pl.BlockSpec((D_MODEL, tf), lambda i, f: (0, f)),
                  pl.BlockSpec((tf, D_MODEL), lambda i, f: (f, 0)),
                  row],
        out_specs=out_specs,
        out_shape=out_shape,
        scratch_shapes=[pltpu.VMEM((tm, D_MODEL), BF16)],
        compiler_params=_params(("parallel", "arbitrary")),
        name="ffn_residual" if emit_residual else "ffn_final",
    )(x2d, g, wg, wu, wd, g2)


def _rope128(y, cos, sin):
    return y * cos + pltpu.roll(y, D_HEAD // 2, axis=1) * sin


def _rope64(y, cos, sin):
    lane = lax.broadcasted_iota(jnp.int32, y.shape, 1)
    first_half = (lane & (D_IDX - 1)) < D_IDX // 2
    partner = jnp.where(first_half,
                        pltpu.roll(y, LANES - D_IDX // 2, axis=1),
                        pltpu.roll(y, D_IDX // 2, axis=1))
    return y * cos + partner * sin


def _proj_kernel(h_ref, w_ref, cos_ref, sin_ref, o_ref, *, mode):
    y = _dot_nt(h_ref[...], w_ref[...])
    n_chunks = y.shape[1] // LANES
    if mode == "plain":
        o_ref[...] = y.astype(o_ref.dtype)
        return
    cos = cos_ref[...]
    sin = sin_ref[...]
    for c in range(n_chunks):
        yc = y[:, c * LANES:(c + 1) * LANES]
        if mode == "rope128":
            yc = _rope128(yc, cos, sin)
            if AQ_CHUNK0 <= c < AQ_CHUNK0 + H_ATT:
                yc = yc * (LOG2_E * D_HEAD ** -0.5)
        elif c < IDX_Q_COLS // LANES:
            yc = _rope64(yc, cos, sin) * (D_IDX ** -0.5)
        elif c == IDX_Q_COLS // LANES:
            yc = _rope64(yc, cos, sin)
        else:
            yc = yc * (H_IDX ** -0.5)
        o_ref[:, c * LANES:(c + 1) * LANES] = yc.astype(o_ref.dtype)


def _proj(h, w, cos, sin, *, mode, tm, out_dtype):
    n_tok, n_cols = h.shape[0], w.shape[0]
    tab = pl.BlockSpec((tm, LANES), lambda i: (i, 0))
    return pl.pallas_call(
        functools.partial(_proj_kernel, mode=mode),
        grid=(n_tok // tm,),
        in_specs=[pl.BlockSpec((tm, D_MODEL), lambda i: (i, 0)),
                  pl.BlockSpec((n_cols, D_MODEL), lambda i: (0, 0), pipeline_mode=pl.Buffered(1)),
                  tab, tab],
        out_specs=pl.BlockSpec((tm, n_cols), lambda i: (i, 0)),
        out_shape=jax.ShapeDtypeStruct((n_tok, n_cols), out_dtype),
        compiler_params=_params(("parallel",)),
        name="proj_" + mode,
    )(h, w, cos, sin)


RET_UNROLL = 8
RET_HEADS_PER_STEP = 2


def _retention_kernel(lg_ref, q_ref, k_ref, v_ref, g_ref, rn_ref, o_ref):
    C = RET_CHUNK
    n_chunks = q_ref.shape[0] // C
    row = lax.broadcasted_iota(jnp.int32, (C, C), 0).astype(F32)
    col = lax.broadcasted_iota(jnp.int32, (C, C), 1).astype(F32)
    diff = row - col
    scale = DK_RET ** -0.5
    heads = []
    for j in range(RET_HEADS_PER_STEP):
        lg = lg_ref[j]
        heads.append(dict(
            lanes=slice(j * LANES, (j + 1) * LANES),
            decay=jnp.where(diff >= 0, jnp.exp(jnp.maximum(diff, 0.0) * lg), 0.0) * scale,
            k_dec=jnp.exp((C - 1 - row) * lg) * scale,
            q_dec=jnp.exp((row + 1) * lg),
            g_chunk=jnp.exp(C * lg)))

    def body(n, states):
        sl = pl.ds(pl.multiple_of(n * C, C), C)
        new_states = []
        for hd, state in zip(heads, states):
            qc = q_ref[sl, hd["lanes"]]
            kc = k_ref[sl, hd["lanes"]]
            vc = v_ref[sl, hd["lanes"]]
            s = _dot_nt(qc, kc) * hd["decay"]
            intra = _dot(s.astype(BF16), vc)
            cross = _dot(qc, state.astype(BF16)) * hd["q_dec"]
            kv = _dot_tn((kc.astype(F32) * hd["k_dec"]).astype(BF16), vc)
            o = intra + cross
            mu = jnp.mean(o, axis=-1, keepdims=True)
            d = o - mu
            var = jnp.mean(d * d, axis=-1, keepdims=True)
            y = d * lax.rsqrt(var + NORM_EPS) * rn_ref[:, hd["lanes"]]
            gate = g_ref[sl, hd["lanes"]].astype(F32)
            o_ref[sl, hd["lanes"]] = (y * (gate * jax.nn.sigmoid(gate))).astype(o_ref.dtype)
            new_states.append(state * hd["g_chunk"] + kv)
        return tuple(new_states)

    init = tuple(jnp.zeros((DK_RET, DV_RET), F32) for _ in heads)
    lax.fori_loop(0, n_chunks, body, init, unroll=RET_UNROLL)


def _retention(rope_out, plain_out, ret_norm, batch, seq):
    n_tok = batch * seq
    lg = jnp.log1p(-jnp.exp2(-5.0 - jnp.arange(H_RET, dtype=F32)))
    lg = jnp.broadcast_to(lg[:, None, None], (H_RET, 1, LANES))
    width = RET_HEADS_PER_STEP * LANES
    n_groups = H_RET // RET_HEADS_PER_STEP
    heads = lambda off: pl.BlockSpec((seq, width), lambda b, h: (b, off + h))
    return pl.pallas_call(
        _retention_kernel,
        grid=(batch, n_groups),
        in_specs=[pl.BlockSpec((RET_HEADS_PER_STEP, 1, LANES), lambda b, h: (h, 0, 0)),
                  heads(0), heads(n_groups),
                  heads(0), heads(n_groups),
                  pl.BlockSpec((1, width), lambda b, h: (0, h))],
        out_specs=heads(0),
        out_shape=jax.ShapeDtypeStruct((n_tok, RET_W), BF16),
        compiler_params=_params(("parallel", "parallel")),
        name="retention",
    )(lg, rope_out, rope_out, plain_out, plain_out, ret_norm)


KEY_TILE = 512
SCORE_ROWS = 128
FLT_MAX_KEY = 0x00800000
KEY_BITS = 32
FLT_MIN = 1.1754943508222875e-38
ATT_HEADS_PER_CHAIN = 2
ATT_LOOKAHEAD = 2


def _order_key_to_float(u):
    s = u ^ INT_MIN
    return pltpu.bitcast(s ^ ((s >> 31) & jnp.int32(0x7FFFFFFF)), F32)


def _attn_kernel(iq_ref, ik_ref, iw_ref, aq_ref, ak_ref, av_ref, *rest, top_k, n_cast):
    o_ref, sc_ref, sc16_ref = rest[n_cast], rest[-2], rest[-1]
    for w32_ref, w16_ref in zip(rest[:n_cast], rest[n_cast + 1:-2]):
        w16_ref[...] = w32_ref[...].astype(BF16)

    qb = pl.program_id(1)
    t0 = qb * Q_BLOCK
    n_tiles = (t0 + Q_BLOCK + KEY_TILE - 1) // KEY_TILE

    w_t = iw_ref[...].astype(F32).T
    w_rows = [w_t[h:h + 1, :] for h in range(H_IDX)]

    lane = lax.broadcasted_iota(jnp.int32, (Q_BLOCK, LANES), 1)
    low = lane < D_IDX
    q_pairs = []
    for c in range(IDX_Q_COLS // LANES):
        qc = iq_ref[:, c * LANES:(c + 1) * LANES]
        zero = jnp.zeros_like(qc)
        q_pairs.append(jnp.concatenate([jnp.where(low, qc, zero), jnp.where(low, zero, qc)], axis=0))

    q_pos = t0 + lax.broadcasted_iota(jnp.int32, (SCORE_ROWS, Q_BLOCK), 1)
    key_off = lax.broadcasted_iota(jnp.int32, (SCORE_ROWS, Q_BLOCK), 0)

    def score_tile(kt, carry):
        for s in range(KEY_TILE // SCORE_ROWS):
            base = pl.multiple_of(kt * KEY_TILE + s * SCORE_ROWS, SCORE_ROWS)
            kk = ik_ref[pl.ds(base, SCORE_ROWS), :]
            acc = jnp.zeros((SCORE_ROWS, Q_BLOCK), F32)
            for c, qp in enumerate(q_pairs):
                z = _dot_nt(kk, qp)
                acc = acc + jnp.maximum(z[:, :Q_BLOCK], 0.0) * w_rows[2 * c]
                acc = acc + jnp.maximum(z[:, Q_BLOCK:], 0.0) * w_rows[2 * c + 1]
            causal = base + key_off <= q_pos
            score = jnp.where(causal, acc, -jnp.inf)
            sc_ref[pl.ds(base, SCORE_ROWS), :] = score
            sc16_ref[pl.ds(base, SCORE_ROWS), :] = score.astype(BF16)
        return carry

    lax.fori_loop(0, n_tiles, score_tile, 0)

    def count_tiles(ref, indicator):
        rows = SUBLANES * 4 // ref.dtype.itemsize

        def count_tile(kt, cnt):
            base = pl.multiple_of(kt * KEY_TILE, KEY_TILE)
            part = indicator(ref[pl.ds(base, KEY_TILE), :], base)
            part = part.reshape(KEY_TILE // rows, rows, Q_BLOCK)
            while part.shape[0] > 1:
                half = part.shape[0] // 2
                part = part[:half] + part[half:]
            return cnt + part[0].astype(F32)

        cnt = lax.fori_loop(0, n_tiles, count_tile, jnp.zeros((rows, Q_BLOCK), F32))
        return jnp.sum(cnt, axis=0, keepdims=True)

    def count_ge(ref, cand):
        one, zero = jnp.ones((), ref.dtype), jnp.zeros((), ref.dtype)
        return count_tiles(ref, lambda tile, base: jnp.where(tile >= cand, one, zero))

    def search_bits(ref, tau_u, first_bit, n_bits):
        def search_bit(it, tau_u):
            cand_u = tau_u | lax.shift_left(jnp.int32(1), first_bit - it)
            cand = _order_key_to_float(cand_u).astype(ref.dtype)
            return jnp.where(count_ge(ref, cand) >= top_k, cand_u, tau_u)
        return lax.fori_loop(0, n_bits, search_bit, tau_u)

    searched = t0 + Q_BLOCK > top_k
    half_bits = jnp.where(searched, KEY_BITS // 2, 0)
    rounded = search_bits(sc16_ref, jnp.zeros((1, Q_BLOCK), jnp.int32), KEY_BITS - 1, half_bits)
    fits = count_ge(sc_ref, _order_key_to_float(rounded)) >= top_k
    prefix = jnp.where(fits, rounded, rounded - (1 << KEY_BITS // 2))
    tau_u = search_bits(sc_ref, prefix, KEY_BITS // 2 - 1, half_bits)
    tau_u = jnp.where(searched, tau_u, FLT_MAX_KEY)
    tau = _order_key_to_float(tau_u)

    n_ge = count_ge(sc_ref, tau)
    surplus = jnp.max(n_ge) > top_k

    @pl.when(jnp.logical_and(searched, surplus))
    def _():
        key_row = lax.broadcasted_iota(jnp.int32, (KEY_TILE, Q_BLOCK), 0)

        def tied_before(cutoff):
            return lambda tile, base: jnp.where(
                tile == tau, jnp.where(base + key_row < cutoff, 1.0, 0.0), 0.0)

        n_eq = count_tiles(sc_ref, tied_before(jnp.int32(sc_ref.shape[0])))
        keep = top_k - (n_ge - n_eq)
        n_cut_bits = sc_ref.shape[0].bit_length()

        def cutoff_bit(it, cutoff):
            cand = cutoff | lax.shift_left(jnp.int32(1), n_cut_bits - 1 - it)
            return jnp.where(count_tiles(sc_ref, tied_before(cand)) <= keep, cand, cutoff)

        cutoff = lax.fori_loop(0, n_cut_bits, cutoff_bit, jnp.zeros((1, Q_BLOCK), jnp.int32))
        below = jnp.where(tau == 0.0, -FLT_MIN, _order_key_to_float(tau_u - 1))

        def demote(kt, carry):
            base = pl.multiple_of(kt * KEY_TILE, KEY_TILE)
            tile = sc_ref[pl.ds(base, KEY_TILE), :]
            demoted = jnp.where(base + key_row < cutoff, tile, below)
            sc_ref[pl.ds(base, KEY_TILE), :] = jnp.where(tile == tau, demoted, tile)
            return carry

        lax.fori_loop(0, n_tiles, demote, 0)

    n_chains = H_ATT // ATT_HEADS_PER_CHAIN
    width = ATT_HEADS_PER_CHAIN * Q_BLOCK
    q_chains = []
    for c in range(n_chains):
        heads = range(c * ATT_HEADS_PER_CHAIN, (c + 1) * ATT_HEADS_PER_CHAIN)
        q_chains.append(jnp.concatenate([aq_ref[:, h * D_HEAD:(h + 1) * D_HEAD] for h in heads], axis=0))

    def attend_tile(kt, carry):
        base = pl.multiple_of(kt * KEY_TILE, KEY_TILE)
        bias = jnp.where(sc_ref[pl.ds(base, KEY_TILE), :] >= tau, 0.0, NEG_BIG)
        bias = jnp.concatenate([bias] * ATT_HEADS_PER_CHAIN, axis=1)
        kv_of = lambda c: c * ATT_HEADS_PER_CHAIN // GROUP
        k_tiles = [ak_ref[pl.ds(base, KEY_TILE), g * D_HEAD:(g + 1) * D_HEAD] for g in range(H_KV)]
        v_tiles = [av_ref[pl.ds(base, KEY_TILE), g * D_HEAD:(g + 1) * D_HEAD] for g in range(H_KV)]
        score = lambda c: _dot_nt(k_tiles[kv_of(c)], q_chains[c]) + bias
        logits = [score(c) for c in range(min(ATT_LOOKAHEAD, n_chains))]
        new = []
        for c in range(n_chains):
            m_old, l_old, acc_old = carry[c]
            m_new = jnp.maximum(m_old, jnp.max(logits[c], axis=0, keepdims=True))
            alpha = jnp.exp2(m_old - m_new)
            p = jnp.exp2(logits[c] - m_new)
            l_new = alpha * l_old + jnp.sum(p, axis=0, keepdims=True)
            if c + ATT_LOOKAHEAD < n_chains:
                logits.append(score(c + ATT_LOOKAHEAD))
            acc_new = alpha * acc_old + _dot_tn(v_tiles[kv_of(c)], p.astype(BF16))
            new.append((m_new, l_new, acc_new))
        return tuple(new)

    init = tuple((jnp.full((1, width), NEG_BIG, F32), jnp.zeros((1, width), F32),
                  jnp.zeros((D_HEAD, width), F32)) for _ in range(n_chains))
    final = lax.fori_loop(0, n_tiles, attend_tile, init)
    for c in range(n_chains):
        _, l_fin, acc_fin = final[c]
        out_t = acc_fin / l_fin
        for j in range(ATT_HEADS_PER_CHAIN):
            hcol = (c * ATT_HEADS_PER_CHAIN + j) * D_HEAD
            o_ref[:, hcol:hcol + D_HEAD] = out_t[:, j * Q_BLOCK:(j + 1) * Q_BLOCK].T.astype(o_ref.dtype)


def _sparse_attention(idx_out, rope_out, plain_out, batch, seq, weights_to_cast):
    n_tok = batch * seq
    nb = seq // Q_BLOCK
    n_steps = batch * nb
    top_k = min(MAX_TOPK, seq // 4)
    kv_w = H_KV * D_HEAD
    step = lambda b, q: (b * nb + q, 0)
    slabs = [pl.BlockSpec((w.shape[0] // n_steps, w.shape[1]), step) for w in weights_to_cast]
    outs = pl.pallas_call(
        functools.partial(_attn_kernel, top_k=top_k, n_cast=len(weights_to_cast)),
        grid=(batch, nb),
        in_specs=[
            pl.BlockSpec((Q_BLOCK, IDX_Q_COLS), step),
            pl.BlockSpec((seq, LANES), lambda b, q: (b, IDX_Q_COLS // LANES)),
            pl.BlockSpec((Q_BLOCK, LANES), lambda b, q: (b * nb + q, IDX_Q_COLS // LANES + 1)),
            pl.BlockSpec((Q_BLOCK, ATT_W), lambda b, q: (b * nb + q, 2 * RET_W // ATT_W)),
            pl.BlockSpec((seq, kv_w), lambda b, q: (b, (2 * RET_W + ATT_W) // kv_w)),
            pl.BlockSpec((seq, kv_w), lambda b, q: (b, 2 * RET_W // kv_w)),
        ] + slabs,
        out_specs=[pl.BlockSpec((Q_BLOCK, ATT_W), step)] + slabs,
        out_shape=[jax.ShapeDtypeStruct((n_tok, ATT_W), BF16)]
                  + [jax.ShapeDtypeStruct(w.shape, BF16) for w in weights_to_cast],
        scratch_shapes=[pltpu.VMEM((seq, Q_BLOCK), F32), pltpu.VMEM((seq, Q_BLOCK), BF16)],
        compiler_params=_params(("parallel", "arbitrary")),
        name="sparse_attention",
    )(idx_out, idx_out, idx_out, rope_out, rope_out, plain_out, *weights_to_cast)
    return outs[0], outs[1:]


def _out_proj_kernel(x_ref, ro_ref, ao_ref, wr_ref, wa_ref, o_ref):
    o_ref[...] = x_ref[...] + _dot(ro_ref[...], wr_ref[...]) + _dot(ao_ref[...], wa_ref[...])


def _out_proj(x2d, ro, ao, w_out, tm=512):
    n_tok = x2d.shape[0]
    tile = pl.BlockSpec((tm, D_MODEL), lambda i: (i, 0))
    half = pl.BlockSpec((tm, RET_W), lambda i: (i, 0))
    w_ret = pl.BlockSpec((RET_W, D_MODEL), lambda i: (0, 0))
    w_att = pl.BlockSpec((ATT_W, D_MODEL), lambda i: (RET_W // ATT_W, 0))
    return pl.pallas_call(
        _out_proj_kernel,
        grid=(n_tok // tm,),
        in_specs=[tile, half, half, w_ret, w_att],
        out_specs=tile,
        out_shape=jax.ShapeDtypeStruct((n_tok, D_MODEL), F32),
        compiler_params=_params(("parallel",)),
        name="out_proj",
    )(x2d, ro, ao, w_out, w_out)


def _layer(x2d, tables, batch, seq, ffn1_norm, ffn1_w_gate, ffn1_w_up, ffn1_w_down, mix_norm,
           w_in_groups, ret_norm, w_out, ffn2_norm, ffn2_w_gate, ffn2_w_up, ffn2_w_down, final_norm, last):
    cos_a, sin_a, cos_b, sin_b = tables
    w_rope, w_plain, w_idx = w_in_groups
    row = lambda g: g.reshape(1, -1).astype(F32)

    x1, h = _ffn(x2d, row(ffn1_norm), ffn1_w_gate, ffn1_w_up, ffn1_w_down,
                 row(mix_norm), emit_residual=True, tf=FFN_TF_F32)
    rope_out = _proj(h, w_rope, cos_a, sin_a, mode="rope128", tm=512, out_dtype=BF16)
    plain_out = _proj(h, w_plain, cos_a, sin_a, mode="plain", tm=512, out_dtype=BF16)
    idx_out = _proj(h, w_idx, cos_b, sin_b, mode="idx", tm=1024, out_dtype=BF16)
    ro = _retention(rope_out, plain_out, row(ret_norm), batch, seq)
    ao, (w_out16, wg16, wu16, wd16) = _sparse_attention(
        idx_out, rope_out, plain_out, batch, seq, [w_out, ffn2_w_gate, ffn2_w_up, ffn2_w_down])
    x2 = _out_proj(x1, ro, ao, w_out16)
    out = _ffn(x2, row(ffn2_norm), wg16, wu16, wd16, row(final_norm), emit_residual=not last,
               tf=FFN_TF_BF16)
    return out if last else out[0]


def kernel(x, positions, ffn1_norm, ffn1_w_gate, ffn1_w_up, ffn1_w_down, mix_norm, w_in, ret_norm,
           w_out, ffn2_norm, ffn2_w_gate, ffn2_w_up, ffn2_w_down, final_norm):
    batch, seq, _ = x.shape
    depth = w_in.shape[0]
    tables, w_in_groups = _prepare(positions, jnp.swapaxes(w_in, 1, 2))
    x2d = x.reshape(batch * seq, D_MODEL)
    for l in range(depth):
        last = l == depth - 1
        groups_l = [w[l] for w in w_in_groups]
        x2d = _layer(x2d, tables, batch, seq, ffn1_norm[l], ffn1_w_gate[l], ffn1_w_up[l], ffn1_w_down[l],
                     mix_norm[l], groups_l, ret_norm[l], w_out[l], ffn2_norm[l], ffn2_w_gate[l],
                     ffn2_w_up[l], ffn2_w_down[l], final_norm, last)
    return x2d.reshape(batch, seq, D_MODEL)
```

```python
import functools

import jax
import jax.numpy as jnp
from jax import lax
from jax.experimental import pallas as pl
from jax.experimental.pallas import tpu as pltpu

D_MODEL = 2048
H_RET = 8
DK_RET = 128
DV_RET = 128
RET_CHUNK = 128
H_ATT = 8
H_KV = 2
D_HEAD = 128
H_IDX = 16
D_IDX = 64
MAX_TOPK = 256
Q_BLOCK = 256
D_FF = 5632
ROPE_THETA = 10000.0
NORM_EPS = 1e-6

RET_W = H_RET * DV_RET
ATT_W = H_ATT * D_HEAD
GROUP = H_ATT // H_KV

LANES = 128
SUBLANES = 8
VMEM_LIMIT = 60 * 1024 * 1024

ROPE_COLS = 2 * H_RET * DK_RET + ATT_W + H_KV * D_HEAD
PLAIN_COLS = 2 * RET_W + H_KV * D_HEAD
IDX_Q_COLS = H_IDX * D_IDX
IDX_COLS = IDX_Q_COLS + 2 * LANES
AQ_CHUNK0 = 2 * H_RET * DK_RET // LANES

INT_MIN = -2 ** 31
NEG_BIG = -1e30
LOG2_E = 1.4426950408889634

F32 = jnp.float32
BF16 = jnp.bfloat16


def _dot(a, b):
    return jnp.dot(a, b, preferred_element_type=F32)


def _dot_nt(a, b):
    return lax.dot_general(a, b, (((1,), (1,)), ((), ())), preferred_element_type=F32)


def _dot_tn(a, b):
    return lax.dot_general(a, b, (((0,), (0,)), ((), ())), preferred_element_type=F32)


def _rmsnorm(xf, g):
    ms = jnp.mean(xf * xf, axis=-1, keepdims=True)
    return xf * lax.rsqrt(ms + NORM_EPS) * g


def _params(sem):
    return pltpu.CompilerParams(dimension_semantics=sem, vmem_limit_bytes=VMEM_LIMIT)


_IN_SIZES = (H_RET * DK_RET, H_RET * DK_RET, RET_W, RET_W, ATT_W, H_KV * D_HEAD, H_KV * D_HEAD,
             H_IDX * D_IDX, D_IDX, H_IDX)
W_IN_SPLITS = tuple(sum(_IN_SIZES[:n]) for n in range(len(_IN_SIZES) + 1))


N_PREP_OUT = 7


def _prep_kernel(pos_ref, inv_ref, sgn_a_ref, sgn_b_ref, w_in_ref, *rest):
    n_cast = (len(rest) - N_PREP_OUT) // 2
    (cos_a_ref, sin_a_ref, cos_b_ref, sin_b_ref,
     w_rope_ref, w_plain_ref, w_idx_ref) = rest[n_cast:n_cast + N_PREP_OUT]
    for w32_ref, w16_ref in zip(rest[:n_cast], rest[n_cast + N_PREP_OUT:]):
        w16_ref[...] = w32_ref[...].astype(BF16)

    ang = pos_ref[...].astype(F32) * inv_ref[...]
    lane = lax.broadcasted_iota(jnp.int32, ang.shape, 1)
    half, quarter = D_HEAD // 2, D_IDX // 2

    def table_a(t):
        return jnp.where(lane < half, t, pltpu.roll(t, half, axis=1))

    def table_b(t):
        upper = jnp.where(lane < half + quarter, t, pltpu.roll(t, quarter, axis=1))
        return jnp.where(lane >= half, upper, pltpu.roll(upper, half, axis=1))

    cos, sin = jnp.cos(ang), jnp.sin(ang)
    cos_a_ref[...] = table_a(cos)
    sin_a_ref[...] = table_a(sin) * sgn_a_ref[...]
    cos_b_ref[...] = table_b(cos)
    sin_b_ref[...] = table_b(sin) * sgn_b_ref[...]

    rq, rk, rv, rg, aq, ak, av, iq, ik, iw, end = W_IN_SPLITS
    for layer in range(w_in_ref.shape[0]):
        feats = lambda lo, hi: w_in_ref[layer, lo:hi, :].astype(BF16)
        w_rope_ref[layer, :rv - rq, :] = feats(rq, rv)
        w_rope_ref[layer, rv - rq:, :] = feats(aq, av)
        w_plain_ref[layer, :aq - rv, :] = feats(rv, aq)
        w_plain_ref[layer, aq - rv:, :] = feats(av, iq)
        w_idx_ref[layer, :ik - iq, :] = feats(iq, ik)
        key = feats(ik, iw)
        w_idx_ref[layer, ik - iq:ik - iq + D_IDX, :] = key
        w_idx_ref[layer, ik - iq + D_IDX:ik - iq + LANES, :] = key
        w_idx_ref[layer, ik - iq + LANES:ik - iq + LANES + H_IDX, :] = feats(iw, end)
        w_idx_ref[layer, ik - iq + LANES + H_IDX:, :] = jnp.zeros(
            (LANES - H_IDX, w_idx_ref.shape[2]), BF16)


def _prepare(positions, w_in_t, weights_to_cast):
    n_tok = positions.size
    tm = 512
    n_steps = n_tok // tm
    depth, _, d_model = w_in_t.shape
    w_cols = d_model // n_steps
    lane = jnp.arange(LANES)

    def inv_freq(d):
        return ROPE_THETA ** (-jnp.arange(0, d, 2, dtype=F32) / d)

    unused = jnp.zeros((LANES - D_HEAD // 2 - D_IDX // 2,), F32)
    inv = jnp.concatenate([inv_freq(D_HEAD), inv_freq(D_IDX), unused])[None, :]

    def sign(d):
        return jnp.where(lane % d < d // 2, -1.0, 1.0).astype(F32)[None, :]

    row = pl.BlockSpec((1, LANES), lambda i: (0, 0))
    tab = pl.BlockSpec((tm, LANES), lambda i: (i, 0))
    out = jax.ShapeDtypeStruct((n_tok, LANES), F32)
    slab = lambda n_feats: pl.BlockSpec((depth, n_feats, w_cols), lambda i: (0, 0, i))
    w_out = lambda n_feats: jax.ShapeDtypeStruct((depth, n_feats, d_model), BF16)
    cast_slabs = [pl.BlockSpec((depth, w.shape[1] // n_steps, w.shape[2]), lambda i: (0, i, 0))
                  for w in weights_to_cast]
    outs = pl.pallas_call(
        _prep_kernel,
        grid=(n_steps,),
        in_specs=[pl.BlockSpec((tm, 1), lambda i: (i, 0)), row, row, row, slab(w_in_t.shape[1])]
                 + cast_slabs,
        out_specs=[tab, tab, tab, tab, slab(ROPE_COLS), slab(PLAIN_COLS), slab(IDX_COLS)] + cast_slabs,
        out_shape=[out, out, out, out, w_out(ROPE_COLS), w_out(PLAIN_COLS), w_out(IDX_COLS)]
                  + [jax.ShapeDtypeStruct(w.shape, BF16) for w in weights_to_cast],
        compiler_params=_params(("parallel",)),
        name="prepare",
    )(positions.reshape(n_tok, 1), inv, sign(D_HEAD), sign(D_IDX), w_in_t, *weights_to_cast)
    return outs[:4], outs[4:N_PREP_OUT], outs[N_PREP_OUT:]


FFN_ROW_CHUNK = 128


def _ffn_kernel(x_ref, g_ref, wg_ref, wu_ref, wd_ref, g2_ref, *refs, emit_residual):
    if emit_residual:
        res_ref, normed_ref, xn_ref = refs
        acc_ref = res_ref
    else:
        normed_ref, xn_ref = refs
        acc_ref = normed_ref
    f = pl.program_id(1)
    n_row_chunks = x_ref.shape[0] // FFN_ROW_CHUNK

    def rows_of(r):
        return pl.ds(pl.multiple_of(r * FFN_ROW_CHUNK, FFN_ROW_CHUNK), FFN_ROW_CHUNK)

    @pl.when(f == 0)
    def _():
        def prologue(r, carry):
            xf = x_ref[rows_of(r), :]
            xn_ref[rows_of(r), :] = _rmsnorm(xf, g_ref[...]).astype(BF16)
            acc_ref[rows_of(r), :] = 2.0 * xf
            return carry
        lax.fori_loop(0, n_row_chunks, prologue, 0)

    xn = xn_ref[...]
    a = _dot(xn, wg_ref[...])
    b = _dot(xn, wu_ref[...])
    hidden = (a * jax.nn.sigmoid(a) * b).astype(BF16)
    acc_ref[...] += _dot(hidden, wd_ref[...])

    @pl.when(f == pl.num_programs(1) - 1)
    def _():
        def epilogue(r, carry):
            y = 0.5 * acc_ref[rows_of(r), :]
            if emit_residual:
                res_ref[rows_of(r), :] = y
            normed_ref[rows_of(r), :] = _rmsnorm(y, g2_ref[...]).astype(normed_ref.dtype)
            return carry
        lax.fori_loop(0, n_row_chunks, epilogue, 0)


FFN_TM = 1024
FFN_TF_BF16 = 512


def _ffn(x2d, g, wg, wu, wd, g2, *, emit_residual, tf, tm=FFN_TM):
    n_tok = x2d.shape[0]
    tile_map = lambda i, f: (i, 0)
    x_tile = pl.BlockSpec((tm, D_MODEL), tile_map)
    out_tile = pl.BlockSpec((tm, D_MODEL), tile_map)
    normed_tile = pl.BlockSpec((tm, D_MODEL), tile_map, pipeline_mode=pl.Buffered(1))
    row = pl.BlockSpec((1, D_MODEL), lambda i, f: (0, 0))
    if emit_residual:
        out_specs = [out_tile, normed_tile]
        out_shape = [jax.ShapeDtypeStruct((n_tok, D_MODEL), F32),
                     jax.ShapeDtypeStruct((n_tok, D_MODEL), BF16)]
    else:
        out_specs = out_tile
        out_shape = jax.ShapeDtypeStruct((n_tok, D_MODEL), F32)
    return pl.pallas_call(
        functools.partial(_ffn_kernel, emit_residual=emit_residual),
        grid=(n_tok // tm, D_FF // tf),
        in_specs=[x_tile, row,
                  pl.BlockSpec((D_MODEL, tf), lambda i, f: (0, f)),
                  pl.BlockSpec((D_MODEL, tf), lambda i, f: (0, f)),
                  pl.BlockSpec((tf, D_MODEL), lambda i, f: (f, 0)),
                  row],
        out_specs=out_specs,
        out_shape=out_shape,
        scratch_shapes=[pltpu.VMEM((tm, D_MODEL), BF16)],
        compiler_params=_params(("parallel", "arbitrary")),
        name="ffn_residual" if emit_residual else "ffn_final",
    )(x2d, g, wg, wu, wd, g2)


def _rope128(y, cos, sin):
    return y * cos + pltpu.roll(y, D_HEAD // 2, axis=1) * sin


def _rope64(y, cos, sin):
    lane = lax.broadcasted_iota(jnp.int32, y.shape, 1)
    first_half = (lane & (D_IDX - 1)) < D_IDX // 2
    partner = jnp.where(first_half,
                        pltpu.roll(y, LANES - D_IDX // 2, axis=1),
                        pltpu.roll(y, D_IDX // 2, axis=1))
    return y * cos + partner * sin


def _proj_kernel(h_ref, w_ref, cos_ref, sin_ref, o_ref, *, mode):
    y = _dot_nt(h_ref[...], w_ref[...])
    n_chunks = y.shape[1] // LANES
    if mode == "plain":
        o_ref[...] = y.astype(o_ref.dtype)
        return
    cos = cos_ref[...]
    sin = sin_ref[...]
    for c in range(n_chunks):
        yc = y[:, c * LANES:(c + 1) * LANES]
        if mode == "rope128":
            yc = _rope128(yc, cos, sin)
            if AQ_CHUNK0 <= c < AQ_CHUNK0 + H_ATT:
                yc = yc * (LOG2_E * D_HEAD ** -0.5)
        elif c < IDX_Q_COLS // LANES:
            yc = _rope64(yc, cos, sin) * (D_IDX ** -0.5)
        elif c == IDX_Q_COLS // LANES:
            yc = _rope64(yc, cos, sin)
        else:
            yc = yc * (H_IDX ** -0.5)
        o_ref[:, c * LANES:(c + 1) * LANES] = yc.astype(o_ref.dtype)


def _proj(h, w, cos, sin, *, mode, tm, out_dtype):
    n_tok, n_cols = h.shape[0], w.shape[0]
    tab = pl.BlockSpec((tm, LANES), lambda i: (i, 0))
    return pl.pallas_call(
        functools.partial(_proj_kernel, mode=mode),
        grid=(n_tok // tm,),
        in_specs=[pl.BlockSpec((tm, D_MODEL), lambda i: (i, 0)),
                  pl.BlockSpec((n_cols, D_MODEL), lambda i: (0, 0), pipeline_mode=pl.Buffered(1)),
                  tab, tab],
        out_specs=pl.BlockSpec((tm, n_cols), lambda i: (i, 0)),
        out_shape=jax.ShapeDtypeStruct((n_tok, n_cols), out_dtype),
        compiler_params=_params(("parallel",)),
        name="proj_" + mode,
    )(h, w, cos, sin)


RET_UNROLL = 8
RET_HEADS_PER_STEP = 2


def _retention_kernel(lg_ref, q_ref, k_ref, v_ref, g_ref, rn_ref, o_ref):
    C = RET_CHUNK
    n_chunks = q_ref.shape[0] // C
    row = lax.broadcasted_iota(jnp.int32, (C, C), 0).astype(F32)
    col = lax.broadcasted_iota(jnp.int32, (C, C), 1).astype(F32)
    diff = row - col
    scale = DK_RET ** -0.5
    heads = []
    for j in range(RET_HEADS_PER_STEP):
        lg = lg_ref[j]
        heads.append(dict(
            lanes=slice(j * LANES, (j + 1) * LANES),
            decay=jnp.where(diff >= 0, jnp.exp(jnp.maximum(diff, 0.0) * lg), 0.0) * scale,
            k_dec=jnp.exp((C - 1 - row) * lg) * scale,
            q_dec=jnp.exp((row + 1) * lg),
            g_chunk=jnp.exp(C * lg)))

    def body(n, states):
        sl = pl.ds(pl.multiple_of(n * C, C), C)
        new_states = []
        for hd, state in zip(heads, states):
            qc = q_ref[sl, hd["lanes"]]
            kc = k_ref[sl, hd["lanes"]]
            vc = v_ref[sl, hd["lanes"]]
            s = _dot_nt(qc, kc) * hd["decay"]
            intra = _dot(s.astype(BF16), vc)
            cross = _dot(qc, state.astype(BF16)) * hd["q_dec"]
            kv = _dot_tn((kc.astype(F32) * hd["k_dec"]).astype(BF16), vc)
            o = intra + cross
            mu = jnp.mean(o, axis=-1, keepdims=True)
            d = o - mu
            var = jnp.mean(d * d, axis=-1, keepdims=True)
            y = d * lax.rsqrt(var + NORM_EPS) * rn_ref[:, hd["lanes"]]
            gate = g_ref[sl, hd["lanes"]].astype(F32)
            o_ref[sl, hd["lanes"]] = (y * (gate * jax.nn.sigmoid(gate))).astype(o_ref.dtype)
            new_states.append(state * hd["g_chunk"] + kv)
        return tuple(new_states)

    init = tuple(jnp.zeros((DK_RET, DV_RET), F32) for _ in heads)
    lax.fori_loop(0, n_chunks, body, init, unroll=RET_UNROLL)


def _retention(rope_out, plain_out, ret_norm, batch, seq):
    n_tok = batch * seq
    lg = jnp.log1p(-jnp.exp2(-5.0 - jnp.arange(H_RET, dtype=F32)))
    lg = jnp.broadcast_to(lg[:, None, None], (H_RET, 1, LANES))
    width = RET_HEADS_PER_STEP * LANES
    n_groups = H_RET // RET_HEADS_PER_STEP
    heads = lambda off: pl.BlockSpec((seq, width), lambda b, h: (b, off + h))
    return pl.pallas_call(
        _retention_kernel,
        grid=(batch, n_groups),
        in_specs=[pl.BlockSpec((RET_HEADS_PER_STEP, 1, LANES), lambda b, h: (h, 0, 0)),
                  heads(0), heads(n_groups),
                  heads(0), heads(n_groups),
                  pl.BlockSpec((1, width), lambda b, h: (0, h))],
        out_specs=heads(0),
        out_shape=jax.ShapeDtypeStruct((n_tok, RET_W), BF16),
        compiler_params=_params(("parallel", "parallel")),
        name="retention",
    )(lg, rope_out, rope_out, plain_out, plain_out, ret_norm)


KEY_TILE = 512
SCORE_ROWS = 128
FLT_MAX_KEY = 0x00800000
KEY_BITS = 32
FLT_MIN = 1.1754943508222875e-38
ATT_HEADS_PER_CHAIN = 2
ATT_LOOKAHEAD = 2


def _order_key_to_float(u):
    s = u ^ INT_MIN
    return pltpu.bitcast(s ^ ((s >> 31) & jnp.int32(0x7FFFFFFF)), F32)


def _attn_kernel(iq_ref, ik_ref, iw_ref, aq_ref, ak_ref, av_ref, *rest, top_k, n_cast):
    o_ref, sc_ref, sc16_ref = rest[n_cast], rest[-2], rest[-1]
    for w32_ref, w16_ref in zip(rest[:n_cast], rest[n_cast + 1:-2]):
        w16_ref[...] = w32_ref[...].astype(BF16)

    qb = pl.program_id(1)
    t0 = qb * Q_BLOCK
    n_tiles = (t0 + Q_BLOCK + KEY_TILE - 1) // KEY_TILE

    w_t = iw_ref[...].astype(F32).T
    w_rows = [w_t[h:h + 1, :] for h in range(H_IDX)]

    lane = lax.broadcasted_iota(jnp.int32, (Q_BLOCK, LANES), 1)
    low = lane < D_IDX
    q_pairs = []
    for c in range(IDX_Q_COLS // LANES):
        qc = iq_ref[:, c * LANES:(c + 1) * LANES]
        zero = jnp.zeros_like(qc)
        q_pairs.append(jnp.concatenate([jnp.where(low, qc, zero), jnp.where(low, zero, qc)], axis=0))

    q_pos = t0 + lax.broadcasted_iota(jnp.int32, (SCORE_ROWS, Q_BLOCK), 1)
    key_off = lax.broadcasted_iota(jnp.int32, (SCORE_ROWS, Q_BLOCK), 0)

    def score_tile(kt, carry):
        for s in range(KEY_TILE // SCORE_ROWS):
            base = pl.multiple_of(kt * KEY_TILE + s * SCORE_ROWS, SCORE_ROWS)
            kk = ik_ref[pl.ds(base, SCORE_ROWS), :]
            acc = jnp.zeros((SCORE_ROWS, Q_BLOCK), F32)
            for c, qp in enumerate(q_pairs):
                z = _dot_nt(kk, qp)
                acc = acc + jnp.maximum(z[:, :Q_BLOCK], 0.0) * w_rows[2 * c]
                acc = acc + jnp.maximum(z[:, Q_BLOCK:], 0.0) * w_rows[2 * c + 1]
            causal = base + key_off <= q_pos
            score = jnp.where(causal, acc, -jnp.inf)
            sc_ref[pl.ds(base, SCORE_ROWS), :] = score
            sc16_ref[pl.ds(base, SCORE_ROWS), :] = score.astype(BF16)
        return carry

    lax.fori_loop(0, n_tiles, score_tile, 0)

    def count_tiles(ref, indicator):
        rows = SUBLANES * 4 // ref.dtype.itemsize

        def count_tile(kt, cnt):
            base = pl.multiple_of(kt * KEY_TILE, KEY_TILE)
            part = indicator(ref[pl.ds(base, KEY_TILE), :], base)
            part = part.reshape(KEY_TILE // rows, rows, Q_BLOCK)
            while part.shape[0] > 1:
                half = part.shape[0] // 2
                part = part[:half] + part[half:]
            return cnt + part[0].astype(F32)

        cnt = lax.fori_loop(0, n_tiles, count_tile, jnp.zeros((rows, Q_BLOCK), F32))
        return jnp.sum(cnt, axis=0, keepdims=True)

    def count_ge(ref, cand):
        one, zero = jnp.ones((), ref.dtype), jnp.zeros((), ref.dtype)
        return count_tiles(ref, lambda tile, base: jnp.where(tile >= cand, one, zero))

    def search_bits(ref, tau_u, first_bit, n_bits):
        def search_bit(it, tau_u):
            cand_u = tau_u | lax.shift_left(jnp.int32(1), first_bit - it)
            cand = _order_key_to_float(cand_u).astype(ref.dtype)
            return jnp.where(count_ge(ref, cand) >= top_k, cand_u, tau_u)
        return lax.fori_loop(0, n_bits, search_bit, tau_u)

    searched = t0 + Q_BLOCK > top_k
    half_bits = jnp.where(searched, KEY_BITS // 2, 0)
    rounded = search_bits(sc16_ref, jnp.zeros((1, Q_BLOCK), jnp.int32), KEY_BITS - 1, half_bits)
    fits = count_ge(sc_ref, _order_key_to_float(rounded)) >= top_k
    prefix = jnp.where(fits, rounded, rounded - (1 << KEY_BITS // 2))
    tau_u = search_bits(sc_ref, prefix, KEY_BITS // 2 - 1, half_bits)
    tau_u = jnp.where(searched, tau_u, FLT_MAX_KEY)
    tau = _order_key_to_float(tau_u)

    n_ge = count_ge(sc_ref, tau)
    surplus = jnp.max(n_ge) > top_k

    @pl.when(jnp.logical_and(searched, surplus))
    def _():
        key_row = lax.broadcasted_iota(jnp.int32, (KEY_TILE, Q_BLOCK), 0)

        def tied_before(cutoff):
            return lambda tile, base: jnp.where(
                tile == tau, jnp.where(base + key_row < cutoff, 1.0, 0.0), 0.0)

        n_eq = count_tiles(sc_ref, tied_before(jnp.int32(sc_ref.shape[0])))
        keep = top_k - (n_ge - n_eq)
        n_cut_bits = sc_ref.shape[0].bit_length()

        def cutoff_bit(it, cutoff):
            cand = cutoff | lax.shift_left(jnp.int32(1), n_cut_bits - 1 - it)
            return jnp.where(count_tiles(sc_ref, tied_before(cand)) <= keep, cand, cutoff)

        cutoff = lax.fori_loop(0, n_cut_bits, cutoff_bit, jnp.zeros((1, Q_BLOCK), jnp.int32))
        below = jnp.where(tau == 0.0, -FLT_MIN, _order_key_to_float(tau_u - 1))

        def demote(kt, carry):
            base = pl.multiple_of(kt * KEY_TILE, KEY_TILE)
            tile = sc_ref[pl.ds(base, KEY_TILE), :]
            demoted = jnp.where(base + key_row < cutoff, tile, below)
            sc_ref[pl.ds(base, KEY_TILE), :] = jnp.where(tile == tau, demoted, tile)
            return carry

        lax.fori_loop(0, n_tiles, demote, 0)

    n_chains = H_ATT // ATT_HEADS_PER_CHAIN
    width = ATT_HEADS_PER_CHAIN * Q_BLOCK
    q_chains = []
    for c in range(n_chains):
        heads = range(c * ATT_HEADS_PER_CHAIN, (c + 1) * ATT_HEADS_PER_CHAIN)
        q_chains.append(jnp.concatenate([aq_ref[:, h * D_HEAD:(h + 1) * D_HEAD] for h in heads], axis=0))

    def attend_tile(kt, carry):
        base = pl.multiple_of(kt * KEY_TILE, KEY_TILE)
        bias = jnp.where(sc_ref[pl.ds(base, KEY_TILE), :] >= tau, 0.0, NEG_BIG)
        bias = jnp.concatenate([bias] * ATT_HEADS_PER_CHAIN, axis=1)
        kv_of = lambda c: c * ATT_HEADS_PER_CHAIN // GROUP
        k_tiles = [ak_ref[pl.ds(base, KEY_TILE), g * D_HEAD:(g + 1) * D_HEAD] for g in range(H_KV)]
        v_tiles = [av_ref[pl.ds(base, KEY_TILE), g * D_HEAD:(g + 1) * D_HEAD] for g in range(H_KV)]
        score = lambda c: _dot_nt(k_tiles[kv_of(c)], q_chains[c]) + bias
        logits = [score(c) for c in range(min(ATT_LOOKAHEAD, n_chains))]
        new = []
        for c in range(n_chains):
            m_old, l_old, acc_old = carry[c]
            m_new = jnp.maximum(m_old, jnp.max(logits[c], axis=0, keepdims=True))
            alpha = jnp.exp2(m_old - m_new)
            p = jnp.exp2(logits[c] - m_new)
            l_new = alpha * l_old + jnp.sum(p, axis=0, keepdims=True)
            if c + ATT_LOOKAHEAD < n_chains:
                logits.append(score(c + ATT_LOOKAHEAD))
            acc_new = alpha * acc_old + _dot_tn(v_tiles[kv_of(c)], p.astype(BF16))
            new.append((m_new, l_new, acc_new))
        return tuple(new)

    init = tuple((jnp.full((1, width), NEG_BIG, F32), jnp.zeros((1, width), F32),
                  jnp.zeros((D_HEAD, width), F32)) for _ in range(n_chains))
    final = lax.fori_loop(0, n_tiles, attend_tile, init)
    for c in range(n_chains):
        _, l_fin, acc_fin = final[c]
        out_t = acc_fin / l_fin
        for j in range(ATT_HEADS_PER_CHAIN):
            hcol = (c * ATT_HEADS_PER_CHAIN + j) * D_HEAD
            o_ref[:, hcol:hcol + D_HEAD] = out_t[:, j * Q_BLOCK:(j + 1) * Q_BLOCK].T.astype(o_ref.dtype)


def _sparse_attention(idx_out, rope_out, plain_out, batch, seq, weights_to_cast):
    n_tok = batch * seq
    nb = seq // Q_BLOCK
    n_steps = batch * nb
    top_k = min(MAX_TOPK, seq // 4)
    kv_w = H_KV * D_HEAD
    step = lambda b, q: (b * nb + q, 0)
    slabs = [pl.BlockSpec((w.shape[0] // n_steps, w.shape[1]), step) for w in weights_to_cast]
    outs = pl.pallas_call(
        functools.partial(_attn_kernel, top_k=top_k, n_cast=len(weights_to_cast)),
        grid=(batch, nb),
        in_specs=[
            pl.BlockSpec((Q_BLOCK, IDX_Q_COLS), step),
            pl.BlockSpec((seq, LANES), lambda b, q: (b, IDX_Q_COLS // LANES)),
            pl.BlockSpec((Q_BLOCK, LANES), lambda b, q: (b * nb + q, IDX_Q_COLS // LANES + 1)),
            pl.BlockSpec((Q_BLOCK, ATT_W), lambda b, q: (b * nb + q, 2 * RET_W // ATT_W)),
            pl.BlockSpec((seq, kv_w), lambda b, q: (b, (2 * RET_W + ATT_W) // kv_w)),
            pl.BlockSpec((seq, kv_w), lambda b, q: (b, 2 * RET_W // kv_w)),
        ] + slabs,
        out_specs=[pl.BlockSpec((Q_BLOCK, ATT_W), step)] + slabs,
        out_shape=[jax.ShapeDtypeStruct((n_tok, ATT_W), BF16)]
                  + [jax.ShapeDtypeStruct(w.shape, BF16) for w in weights_to_cast],
        scratch_shapes=[pltpu.VMEM((seq, Q_BLOCK), F32), pltpu.VMEM((seq, Q_BLOCK), BF16)],
        compiler_params=_params(("parallel", "arbitrary")),
        name="sparse_attention",
    )(idx_out, idx_out, idx_out, rope_out, rope_out, plain_out, *weights_to_cast)
    return outs[0], outs[1:]


def _out_proj_kernel(x_ref, ro_ref, ao_ref, wr_ref, wa_ref, o_ref):
    o_ref[...] = x_ref[...] + _dot(ro_ref[...], wr_ref[...]) + _dot(ao_ref[...], wa_ref[...])


def _out_proj(x2d, ro, ao, w_out, tm=512):
    n_tok = x2d.shape[0]
    tile = pl.BlockSpec((tm, D_MODEL), lambda i: (i, 0))
    half = pl.BlockSpec((tm, RET_W), lambda i: (i, 0))
    w_ret = pl.BlockSpec((RET_W, D_MODEL), lambda i: (0, 0))
    w_att = pl.BlockSpec((ATT_W, D_MODEL), lambda i: (RET_W // ATT_W, 0))
    return pl.pallas_call(
        _out_proj_kernel,
        grid=(n_tok // tm,),
        in_specs=[tile, half, half, w_ret, w_att],
        out_specs=tile,
        out_shape=jax.ShapeDtypeStruct((n_tok, D_MODEL), F32),
        compiler_params=_params(("parallel",)),
        name="out_proj",
    )(x2d, ro, ao, w_out, w_out)


def _layer(x2d, tables, batch, seq, ffn1_norm, ffn1_w_gate, ffn1_w_up, ffn1_w_down, mix_norm,
           w_in_groups, ret_norm, w_out, ffn2_norm, ffn2_w_gate, ffn2_w_up, ffn2_w_down, final_norm, last):
    cos_a, sin_a, cos_b, sin_b = tables
    w_rope, w_plain, w_idx = w_in_groups
    row = lambda g: g.reshape(1, -1).astype(F32)

    x1, h = _ffn(x2d, row(ffn1_norm), ffn1_w_gate, ffn1_w_up, ffn1_w_down,
                 row(mix_norm), emit_residual=True, tf=FFN_TF_BF16)
    rope_out = _proj(h, w_rope, cos_a, sin_a, mode="rope128", tm=1024, out_dtype=BF16)
    plain_out = _proj(h, w_plain, cos_a, sin_a, mode="plain", tm=1024, out_dtype=BF16)
    idx_out = _proj(h, w_idx, cos_b, sin_b, mode="idx", tm=1024, out_dtype=BF16)
    ro = _retention(rope_out, plain_out, row(ret_norm), batch, seq)
    ao, (w_out16, wg16, wu16, wd16) = _sparse_attention(
        idx_out, rope_out, plain_out, batch, seq, [w_out, ffn2_w_gate, ffn2_w_up, ffn2_w_down])
    x2 = _out_proj(x1, ro, ao, w_out16)
    out = _ffn(x2, row(ffn2_norm), wg16, wu16, wd16, row(final_norm), emit_residual=not last,
               tf=FFN_TF_BF16)
    return out if last else out[0]


def kernel(x, positions, ffn1_norm, ffn1_w_gate, ffn1_w_up, ffn1_w_down, mix_norm, w_in, ret_norm,
           w_out, ffn2_norm, ffn2_w_gate, ffn2_w_up, ffn2_w_down, final_norm):
    batch, seq, _ = x.shape
    depth = w_in.shape[0]
    tables, w_in_groups, (wg16, wu16, wd16) = _prepare(
        positions, jnp.swapaxes(w_in, 1, 2), [ffn1_w_gate, ffn1_w_up, ffn1_w_down])
    x2d = x.reshape(batch * seq, D_MODEL)
    for l in range(depth):
        last = l == depth - 1
        groups_l = [w[l] for w in w_in_groups]
        x2d = _layer(x2d, tables, batch, seq, ffn1_norm[l], wg16[l], wu16[l], wd16[l],
                     mix_norm[l], groups_l, ret_norm[l], w_out[l], ffn2_norm[l], ffn2_w_gate[l],
                     ffn2_w_up[l], ffn2_w_down[l], final_norm, last)
    return x2d.reshape(batch, seq, D_MODEL)
```

```python
import functools

import jax
import jax.numpy as jnp
from jax import lax
from jax.experimental import pallas as pl
from jax.experimental.pallas import tpu as pltpu

D_MODEL = 2048
H_RET = 8
DK_RET = 128
DV_RET = 128
RET_CHUNK = 128
H_ATT = 8
H_KV = 2
D_HEAD = 128
H_IDX = 16
D_IDX = 64
MAX_TOPK = 256
Q_BLOCK = 256
D_FF = 5632
ROPE_THETA = 10000.0
NORM_EPS = 1e-6

RET_W = H_RET * DV_RET
ATT_W = H_ATT * D_HEAD
GROUP = H_ATT // H_KV

LANES = 128
SUBLANES = 8
VMEM_LIMIT = 60 * 1024 * 1024

ROPE_COLS = 2 * H_RET * DK_RET + ATT_W + H_KV * D_HEAD
PLAIN_COLS = 2 * RET_W + H_KV * D_HEAD
IDX_Q_COLS = H_IDX * D_IDX
IDX_COLS = IDX_Q_COLS + 2 * LANES
AQ_CHUNK0 = 2 * H_RET * DK_RET // LANES

INT_MIN = -2 ** 31
NEG_BIG = -1e30
LOG2_E = 1.4426950408889634

F32 = jnp.float32
BF16 = jnp.bfloat16


def _dot(a, b):
    return jnp.dot(a, b, preferred_element_type=F32)


def _dot_nt(a, b):
    return lax.dot_general(a, b, (((1,), (1,)), ((), ())), preferred_element_type=F32)


def _dot_tn(a, b):
    return lax.dot_general(a, b, (((0,), (0,)), ((), ())), preferred_element_type=F32)


def _rmsnorm(xf, g):
    ms = jnp.mean(xf * xf, axis=-1, keepdims=True)
    return xf * lax.rsqrt(ms + NORM_EPS) * g


def _params(sem):
    return pltpu.CompilerParams(dimension_semantics=sem, vmem_limit_bytes=VMEM_LIMIT)


_IN_SIZES = (H_RET * DK_RET, H_RET * DK_RET, RET_W, RET_W, ATT_W, H_KV * D_HEAD, H_KV * D_HEAD,
             H_IDX * D_IDX, D_IDX, H_IDX)
W_IN_SPLITS = tuple(sum(_IN_SIZES[:n]) for n in range(len(_IN_SIZES) + 1))


def _prep_kernel(pos_ref, inv_ref, sgn_a_ref, sgn_b_ref, w_in_ref,
                 cos_a_ref, sin_a_ref, cos_b_ref, sin_b_ref, w_rope_ref, w_plain_ref, w_idx_ref):
    ang = pos_ref[...].astype(F32) * inv_ref[...]
    lane = lax.broadcasted_iota(jnp.int32, ang.shape, 1)
    half, quarter = D_HEAD // 2, D_IDX // 2

    def table_a(t):
        return jnp.where(lane < half, t, pltpu.roll(t, half, axis=1))

    def table_b(t):
        upper = jnp.where(lane < half + quarter, t, pltpu.roll(t, quarter, axis=1))
        return jnp.where(lane >= half, upper, pltpu.roll(upper, half, axis=1))

    cos, sin = jnp.cos(ang), jnp.sin(ang)
    cos_a_ref[...] = table_a(cos)
    sin_a_ref[...] = table_a(sin) * sgn_a_ref[...]
    cos_b_ref[...] = table_b(cos)
    sin_b_ref[...] = table_b(sin) * sgn_b_ref[...]

    rq, rk, rv, rg, aq, ak, av, iq, ik, iw, end = W_IN_SPLITS
    for layer in range(w_in_ref.shape[0]):
        feats = lambda lo, hi: w_in_ref[layer, lo:hi, :].astype(BF16)
        w_rope_ref[layer, :rv - rq, :] = feats(rq, rv)
        w_rope_ref[layer, rv - rq:, :] = feats(aq, av)
        w_plain_ref[layer, :aq - rv, :] = feats(rv, aq)
        w_plain_ref[layer, aq - rv:, :] = feats(av, iq)
        w_idx_ref[layer, :ik - iq, :] = feats(iq, ik)
        key = feats(ik, iw)
        w_idx_ref[layer, ik - iq:ik - iq + D_IDX, :] = key
        w_idx_ref[layer, ik - iq + D_IDX:ik - iq + LANES, :] = key
        w_idx_ref[layer, ik - iq + LANES:ik - iq + LANES + H_IDX, :] = feats(iw, end)
        w_idx_ref[layer, ik - iq + LANES + H_IDX:, :] = jnp.zeros(
            (LANES - H_IDX, w_idx_ref.shape[2]), BF16)


def _prepare(positions, w_in_t):
    n_tok = positions.size
    tm = 1024
    n_steps = n_tok // tm
    depth, _, d_model = w_in_t.shape
    w_cols = d_model // n_steps
    lane = jnp.arange(LANES)

    def inv_freq(d):
        return ROPE_THETA ** (-jnp.arange(0, d, 2, dtype=F32) / d)

    unused = jnp.zeros((LANES - D_HEAD // 2 - D_IDX // 2,), F32)
    inv = jnp.concatenate([inv_freq(D_HEAD), inv_freq(D_IDX), unused])[None, :]

    def sign(d):
        return jnp.where(lane % d < d // 2, -1.0, 1.0).astype(F32)[None, :]

    row = pl.BlockSpec((1, LANES), lambda i: (0, 0))
    tab = pl.BlockSpec((tm, LANES), lambda i: (i, 0))
    out = jax.ShapeDtypeStruct((n_tok, LANES), F32)
    slab = lambda n_feats: pl.BlockSpec((depth, n_feats, w_cols), lambda i: (0, 0, i))
    w_out = lambda n_feats: jax.ShapeDtypeStruct((depth, n_feats, d_model), BF16)
    outs = pl.pallas_call(
        _prep_kernel,
        grid=(n_steps,),
        in_specs=[pl.BlockSpec((tm, 1), lambda i: (i, 0)), row, row, row, slab(w_in_t.shape[1])],
        out_specs=[tab, tab, tab, tab, slab(ROPE_COLS), slab(PLAIN_COLS), slab(IDX_COLS)],
        out_shape=[out, out, out, out, w_out(ROPE_COLS), w_out(PLAIN_COLS), w_out(IDX_COLS)],
        compiler_params=_params(("parallel",)),
        name="prepare",
    )(positions.reshape(n_tok, 1), inv, sign(D_HEAD), sign(D_IDX), w_in_t)
    return outs[:4], outs[4:]


FFN_ROW_CHUNK = 128


def _ffn_kernel(x_ref, g_ref, wg_ref, wu_ref, wd_ref, g2_ref, *refs, emit_residual):
    if emit_residual:
        res_ref, normed_ref, xn_ref = refs
        acc_ref = res_ref
    else:
        normed_ref, xn_ref = refs
        acc_ref = normed_ref
    f = pl.program_id(1)
    n_row_chunks = x_ref.shape[0] // FFN_ROW_CHUNK

    def rows_of(r):
        return pl.ds(pl.multiple_of(r * FFN_ROW_CHUNK, FFN_ROW_CHUNK), FFN_ROW_CHUNK)

    @pl.when(f == 0)
    def _():
        def prologue(r, carry):
            xf = x_ref[rows_of(r), :]
            xn_ref[rows_of(r), :] = _rmsnorm(xf, g_ref[...]).astype(BF16)
            acc_ref[rows_of(r), :] = 2.0 * xf
            return carry
        lax.fori_loop(0, n_row_chunks, prologue, 0)

    xn = xn_ref[...]
    a = _dot(xn, wg_ref[...].astype(BF16))
    b = _dot(xn, wu_ref[...].astype(BF16))
    hidden = (a * jax.nn.sigmoid(a) * b).astype(BF16)
    acc_ref[...] += _dot(hidden, wd_ref[...].astype(BF16))

    @pl.when(f == pl.num_programs(1) - 1)
    def _():
        def epilogue(r, carry):
            y = 0.5 * acc_ref[rows_of(r), :]
            if emit_residual:
                res_ref[rows_of(r), :] = y
            normed_ref[rows_of(r), :] = _rmsnorm(y, g2_ref[...]).astype(normed_ref.dtype)
            return carry
        lax.fori_loop(0, n_row_chunks, epilogue, 0)


FFN_TM = 1024
FFN_TF_F32 = 256
FFN_TF_BF16 = 512


def _ffn(x2d, g, wg, wu, wd, g2, *, emit_residual, tf, tm=FFN_TM):
    n_tok = x2d.shape[0]
    tile_map = lambda i, f: (i, 0)
    x_tile = pl.BlockSpec((tm, D_MODEL), tile_map)
    out_tile = pl.BlockSpec((tm, D_MODEL), tile_map)
    normed_tile = out_tile
    row = pl.BlockSpec((1, D_MODEL), lambda i, f: (0, 0))
    if emit_residual:
        out_specs = [out_tile, normed_tile]
        out_shape = [jax.ShapeDtypeStruct((n_tok, D_MODEL), F32),
                     jax.ShapeDtypeStruct((n_tok, D_MODEL), BF16)]
    else:
        out_specs = out_tile
        out_shape = jax.ShapeDtypeStruct((n_tok, D_MODEL), F32)
    return pl.pallas_call(
        functools.partial(_ffn_kernel, emit_residual=emit_residual),
        grid=(n_tok // tm, D_FF // tf),
        in_specs=[x_tile, row,
                  pl.BlockSpec((D_MODEL, tf), lambda i, f: (0, f)),
                  pl.BlockSpec((D_MODEL, tf), lambda i, f: (0, f)),
                  pl.BlockSpec((tf, D_MODEL), lambda i, f: (f, 0)),
                  row],
        out_specs=out_specs,
        out_shape=out_shape,
        scratch_shapes=[pltpu.VMEM((tm, D_MODEL), BF16)],
        compiler_params=_params(("parallel", "arbitrary")),
        name="ffn_residual" if emit_residual else "ffn_final",
    )(x2d, g, wg, wu, wd, g2)


def _rope128(y, cos, sin):
    return y * cos + pltpu.roll(y, D_HEAD // 2, axis=1) * sin


def _rope64(y, cos, sin):
    lane = lax.broadcasted_iota(jnp.int32, y.shape, 1)
    first_half = (lane & (D_IDX - 1)) < D_IDX // 2
    partner = jnp.where(first_half,
                        pltpu.roll(y, LANES - D_IDX // 2, axis=1),
                        pltpu.roll(y, D_IDX // 2, axis=1))
    return y * cos + partner * sin


def _proj_kernel(h_ref, w_ref, cos_ref, sin_ref, o_ref, *, mode):
    y = _dot_nt(h_ref[...], w_ref[...])
    n_chunks = y.shape[1] // LANES
    if mode == "plain":
        o_ref[...] = y.astype(o_ref.dtype)
        return
    cos = cos_ref[...]
    sin = sin_ref[...]
    for c in range(n_chunks):
        yc = y[:, c * LANES:(c + 1) * LANES]
        if mode == "rope128":
            yc = _rope128(yc, cos, sin)
            if AQ_CHUNK0 <= c < AQ_CHUNK0 + H_ATT:
                yc = yc * (LOG2_E * D_HEAD ** -0.5)
        elif c < IDX_Q_COLS // LANES:
            yc = _rope64(yc, cos, sin) * (D_IDX ** -0.5)
        elif c == IDX_Q_COLS // LANES:
            yc = _rope64(yc, cos, sin)
        else:
            yc = yc * (H_IDX ** -0.5)
        o_ref[:, c * LANES:(c + 1) * LANES] = yc.astype(o_ref.dtype)


def _proj(h, w, cos, sin, *, mode, tm, out_dtype):
    n_tok, n_cols = h.shape[0], w.shape[0]
    tab = pl.BlockSpec((tm, LANES), lambda i: (i, 0))
    return pl.pallas_call(
        functools.partial(_proj_kernel, mode=mode),
        grid=(n_tok // tm,),
        in_specs=[pl.BlockSpec((tm, D_MODEL), lambda i: (i, 0)),
                  pl.BlockSpec((n_cols, D_MODEL), lambda i: (0, 0), pipeline_mode=pl.Buffered(1)),
                  tab, tab],
        out_specs=pl.BlockSpec((tm, n_cols), lambda i: (i, 0)),
        out_shape=jax.ShapeDtypeStruct((n_tok, n_cols), out_dtype),
        compiler_params=_params(("parallel",)),
        name="proj_" + mode,
    )(h, w, cos, sin)


RET_UNROLL = 8
RET_HEADS_PER_STEP = 2


def _retention_kernel(lg_ref, q_ref, k_ref, v_ref, g_ref, rn_ref, o_ref):
    C = RET_CHUNK
    n_chunks = q_ref.shape[0] // C
    row = lax.broadcasted_iota(jnp.int32, (C, C), 0).astype(F32)
    col = lax.broadcasted_iota(jnp.int32, (C, C), 1).astype(F32)
    diff = row - col
    scale = DK_RET ** -0.5
    heads = []
    for j in range(RET_HEADS_PER_STEP):
        lg = lg_ref[j]
        heads.append(dict(
            lanes=slice(j * LANES, (j + 1) * LANES),
            decay=jnp.where(diff >= 0, jnp.exp(jnp.maximum(diff, 0.0) * lg), 0.0) * scale,
            k_dec=jnp.exp((C - 1 - row) * lg) * scale,
            q_dec=jnp.exp((row + 1) * lg),
            g_chunk=jnp.exp(C * lg)))

    def body(n, states):
        sl = pl.ds(pl.multiple_of(n * C, C), C)
        new_states = []
        for hd, state in zip(heads, states):
            qc = q_ref[sl, hd["lanes"]]
            kc = k_ref[sl, hd["lanes"]]
            vc = v_ref[sl, hd["lanes"]]
            s = _dot_nt(qc, kc) * hd["decay"]
            intra = _dot(s.astype(BF16), vc)
            cross = _dot(qc, state.astype(BF16)) * hd["q_dec"]
            kv = _dot_tn((kc.astype(F32) * hd["k_dec"]).astype(BF16), vc)
            o = intra + cross
            mu = jnp.mean(o, axis=-1, keepdims=True)
            d = o - mu
            var = jnp.mean(d * d, axis=-1, keepdims=True)
            y = d * lax.rsqrt(var + NORM_EPS) * rn_ref[:, hd["lanes"]]
            gate = g_ref[sl, hd["lanes"]].astype(F32)
            o_ref[sl, hd["lanes"]] = (y * (gate * jax.nn.sigmoid(gate))).astype(o_ref.dtype)
            new_states.append(state * hd["g_chunk"] + kv)
        return tuple(new_states)

    init = tuple(jnp.zeros((DK_RET, DV_RET), F32) for _ in heads)
    lax.fori_loop(0, n_chunks, body, init, unroll=RET_UNROLL)


def _retention(rope_out, plain_out, ret_norm, batch, seq):
    n_tok = batch * seq
    lg = jnp.log1p(-jnp.exp2(-5.0 - jnp.arange(H_RET, dtype=F32)))
    lg = jnp.broadcast_to(lg[:, None, None], (H_RET, 1, LANES))
    width = RET_HEADS_PER_STEP * LANES
    n_groups = H_RET // RET_HEADS_PER_STEP
    heads = lambda off: pl.BlockSpec((seq, width), lambda b, h: (b, off + h))
    return pl.pallas_call(
        _retention_kernel,
        grid=(batch, n_groups),
        in_specs=[pl.BlockSpec((RET_HEADS_PER_STEP, 1, LANES), lambda b, h: (h, 0, 0)),
                  heads(0), heads(n_groups),
                  heads(0), heads(n_groups),
                  pl.BlockSpec((1, width), lambda b, h: (0, h))],
        out_specs=heads(0),
        out_shape=jax.ShapeDtypeStruct((n_tok, RET_W), BF16),
        compiler_params=_params(("parallel", "parallel")),
        name="retention",
    )(lg, rope_out, rope_out, plain_out, plain_out, ret_norm)


KEY_TILE = 512
SCORE_ROWS = 128
FLT_MAX_KEY = 0x00800000
KEY_BITS = 32
FLT_MIN = 1.1754943508222875e-38
ATT_HEADS_PER_CHAIN = 2
ATT_LOOKAHEAD = 2


def _order_key_to_float(u):
    s = u ^ INT_MIN
    return pltpu.bitcast(s ^ ((s >> 31) & jnp.int32(0x7FFFFFFF)), F32)


def _attn_kernel(iq_ref, ik_ref, iw_ref, aq_ref, ak_ref, av_ref, *rest, top_k, n_cast):
    o_ref, sc_ref, sc16_ref = rest[n_cast], rest[-2], rest[-1]
    for w32_ref, w16_ref in zip(rest[:n_cast], rest[n_cast + 1:-2]):
        w16_ref[...] = w32_ref[...].astype(BF16)

    qb = pl.program_id(1)
    t0 = qb * Q_BLOCK
    n_tiles = (t0 + Q_BLOCK + KEY_TILE - 1) // KEY_TILE

    w_t = iw_ref[...].astype(F32).T
    w_rows = [w_t[h:h + 1, :] for h in range(H_IDX)]

    lane = lax.broadcasted_iota(jnp.int32, (Q_BLOCK, LANES), 1)
    low = lane < D_IDX
    q_pairs = []
    for c in range(IDX_Q_COLS // LANES):
        qc = iq_ref[:, c * LANES:(c + 1) * LANES]
        zero = jnp.zeros_like(qc)
        q_pairs.append(jnp.concatenate([jnp.where(low, qc, zero), jnp.where(low, zero, qc)], axis=0))

    q_pos = t0 + lax.broadcasted_iota(jnp.int32, (SCORE_ROWS, Q_BLOCK), 1)
    key_off = lax.broadcasted_iota(jnp.int32, (SCORE_ROWS, Q_BLOCK), 0)

    def score_tile(kt, carry):
        for s in range(KEY_TILE // SCORE_ROWS):
            base = pl.multiple_of(kt * KEY_TILE + s * SCORE_ROWS, SCORE_ROWS)
            kk = ik_ref[pl.ds(base, SCORE_ROWS), :]
            acc = jnp.zeros((SCORE_ROWS, Q_BLOCK), F32)
            for c, qp in enumerate(q_pairs):
                z = _dot_nt(kk, qp)
                acc = acc + jnp.maximum(z[:, :Q_BLOCK], 0.0) * w_rows[2 * c]
                acc = acc + jnp.maximum(z[:, Q_BLOCK:], 0.0) * w_rows[2 * c + 1]
            causal = base + key_off <= q_pos
            score = jnp.where(causal, acc, -jnp.inf)
            sc_ref[pl.ds(base, SCORE_ROWS), :] = score
            sc16_ref[pl.ds(base, SCORE_ROWS), :] = score.astype(BF16)
        return carry

    lax.fori_loop(0, n_tiles, score_tile, 0)

    def count_tiles(ref, indicator):
        rows = SUBLANES * 4 // ref.dtype.itemsize

        def count_tile(kt, cnt):
            base = pl.multiple_of(kt * KEY_TILE, KEY_TILE)
            part = indicator(ref[pl.ds(base, KEY_TILE), :], base)
            part = part.reshape(KEY_TILE // rows, rows, Q_BLOCK)
            while part.shape[0] > 1:
                half = part.shape[0] // 2
                part = part[:half] + part[half:]
            return cnt + part[0].astype(F32)

        cnt = lax.fori_loop(0, n_tiles, count_tile, jnp.zeros((rows, Q_BLOCK), F32))
        return jnp.sum(cnt, axis=0, keepdims=True)

    def count_ge(ref, cand):
        one, zero = jnp.ones((), ref.dtype), jnp.zeros((), ref.dtype)
        return count_tiles(ref, lambda tile, base: jnp.where(tile >= cand, one, zero))

    def search_bits(ref, tau_u, first_bit, n_bits):
        def search_bit(it, tau_u):
            cand_u = tau_u | lax.shift_left(jnp.int32(1), first_bit - it)
            cand = _order_key_to_float(cand_u).astype(ref.dtype)
            return jnp.where(count_ge(ref, cand) >= top_k, cand_u, tau_u)
        return lax.fori_loop(0, n_bits, search_bit, tau_u)

    searched = t0 + Q_BLOCK > top_k
    half_bits = jnp.where(searched, KEY_BITS // 2, 0)
    rounded = search_bits(sc16_ref, jnp.zeros((1, Q_BLOCK), jnp.int32), KEY_BITS - 1, half_bits)
    fits = count_ge(sc_ref, _order_key_to_float(rounded)) >= top_k
    prefix = jnp.where(fits, rounded, rounded - (1 << KEY_BITS // 2))
    tau_u = search_bits(sc_ref, prefix, KEY_BITS // 2 - 1, half_bits)
    tau_u = jnp.where(searched, tau_u, FLT_MAX_KEY)
    tau = _order_key_to_float(tau_u)

    n_ge = count_ge(sc_ref, tau)
    surplus = jnp.max(n_ge) > top_k

    @pl.when(jnp.logical_and(searched, surplus))
    def _():
        key_row = lax.broadcasted_iota(jnp.int32, (KEY_TILE, Q_BLOCK), 0)

        def tied_before(cutoff):
            return lambda tile, base: jnp.where(
                tile == tau, jnp.where(base + key_row < cutoff, 1.0, 0.0), 0.0)

        n_eq = count_tiles(sc_ref, tied_before(jnp.int32(sc_ref.shape[0])))
        keep = top_k - (n_ge - n_eq)
        n_cut_bits = sc_ref.shape[0].bit_length()

        def cutoff_bit(it, cutoff):
            cand = cutoff | lax.shift_left(jnp.int32(1), n_cut_bits - 1 - it)
            return jnp.where(count_tiles(sc_ref, tied_before(cand)) <= keep, cand, cutoff)

        cutoff = lax.fori_loop(0, n_cut_bits, cutoff_bit, jnp.zeros((1, Q_BLOCK), jnp.int32))
        below = jnp.where(tau == 0.0, -FLT_MIN, _order_key_to_float(tau_u - 1))

        def demote(kt, carry):
            base = pl.multiple_of(kt * KEY_TILE, KEY_TILE)
            tile = sc_ref[pl.ds(base, KEY_TILE), :]
            demoted = jnp.where(base + key_row < cutoff, tile, below)
            sc_ref[pl.ds(base, KEY_TILE), :] = jnp.where(tile == tau, demoted, tile)
            return carry

        lax.fori_loop(0, n_tiles, demote, 0)

    n_chains = H_ATT // ATT_HEADS_PER_CHAIN
    width = ATT_HEADS_PER_CHAIN * Q_BLOCK
    q_chains = []
    for c in range(n_chains):
        heads = range(c * ATT_HEADS_PER_CHAIN, (c + 1) * ATT_HEADS_PER_CHAIN)
        q_chains.append(jnp.concatenate([aq_ref[:, h * D_HEAD:(h + 1) * D_HEAD] for h in heads], axis=0))

    def attend_tile(kt, carry):
        base = pl.multiple_of(kt * KEY_TILE, KEY_TILE)
        bias = jnp.where(sc_ref[pl.ds(base, KEY_TILE), :] >= tau, 0.0, NEG_BIG)
        bias = jnp.concatenate([bias] * ATT_HEADS_PER_CHAIN, axis=1)
        kv_of = lambda c: c * ATT_HEADS_PER_CHAIN // GROUP
        k_tiles = [ak_ref[pl.ds(base, KEY_TILE), g * D_HEAD:(g + 1) * D_HEAD] for g in range(H_KV)]
        v_tiles = [av_ref[pl.ds(base, KEY_TILE), g * D_HEAD:(g + 1) * D_HEAD] for g in range(H_KV)]
        score = lambda c: _dot_nt(k_tiles[kv_of(c)], q_chains[c]) + bias
        logits = [score(c) for c in range(min(ATT_LOOKAHEAD, n_chains))]
        new = []
        for c in range(n_chains):
            m_old, l_old, acc_old = carry[c]
            m_new = jnp.maximum(m_old, jnp.max(logits[c], axis=0, keepdims=True))
            alpha = jnp.exp2(m_old - m_new)
            p = jnp.exp2(logits[c] - m_new)
            l_new = alpha * l_old + jnp.sum(p, axis=0, keepdims=True)
            if c + ATT_LOOKAHEAD < n_chains:
                logits.append(score(c + ATT_LOOKAHEAD))
            acc_new = alpha * acc_old + _dot_tn(v_tiles[kv_of(c)], p.astype(BF16))
            new.append((m_new, l_new, acc_new))
        return tuple(new)

    init = tuple((jnp.full((1, width), NEG_BIG, F32), jnp.zeros((1, width), F32),
                  jnp.zeros((D_HEAD, width), F32)) for _ in range(n_chains))
    final = lax.fori_loop(0, n_tiles, attend_tile, init)
    for c in range(n_chains):
        _, l_fin, acc_fin = final[c]
        out_t = acc_fin / l_fin
        for j in range(ATT_HEADS_PER_CHAIN):
            hcol = (c * ATT_HEADS_PER_CHAIN + j) * D_HEAD
            o_ref[:, hcol:hcol + D_HEAD] = out_t[:, j * Q_BLOCK:(j + 1) * Q_BLOCK].T.astype(o_ref.dtype)


def _sparse_attention(idx_out, rope_out, plain_out, batch, seq, weights_to_cast):
    n_tok = batch * seq
    nb = seq // Q_BLOCK
    n_steps = batch * nb
    top_k = min(MAX_TOPK, seq // 4)
    kv_w = H_KV * D_HEAD
    step = lambda b, q: (b * nb + q, 0)
    slabs = [pl.BlockSpec((w.shape[0] // n_steps, w.shape[1]), step) for w in weights_to_cast]
    outs = pl.pallas_call(
        functools.partial(_attn_kernel, top_k=top_k, n_cast=len(weights_to_cast)),
        grid=(batch, nb),
        in_specs=[
            pl.BlockSpec((Q_BLOCK, IDX_Q_COLS), step),
            pl.BlockSpec((seq, LANES), lambda b, q: (b, IDX_Q_COLS // LANES)),
            pl.BlockSpec((Q_BLOCK, LANES), lambda b, q: (b * nb + q, IDX_Q_COLS // LANES + 1)),
            pl.BlockSpec((Q_BLOCK, ATT_W), lambda b, q: (b * nb + q, 2 * RET_W // ATT_W)),
            pl.BlockSpec((seq, kv_w), lambda b, q: (b, (2 * RET_W + ATT_W) // kv_w)),
            pl.BlockSpec((seq, kv_w), lambda b, q: (b, 2 * RET_W // kv_w)),
        ] + slabs,
        out_specs=[pl.BlockSpec((Q_BLOCK, ATT_W), step)] + slabs,
        out_shape=[jax.ShapeDtypeStruct((n_tok, ATT_W), BF16)]
                  + [jax.ShapeDtypeStruct(w.shape, BF16) for w in weights_to_cast],
        scratch_shapes=[pltpu.VMEM((seq, Q_BLOCK), F32), pltpu.VMEM((seq, Q_BLOCK), BF16)],
        compiler_params=_params(("parallel", "arbitrary")),
        name="sparse_attention",
    )(idx_out, idx_out, idx_out, rope_out, rope_out, plain_out, *weights_to_cast)
    return outs[0], outs[1:]


def _out_proj_kernel(x_ref, ro_ref, ao_ref, wr_ref, wa_ref, o_ref):
    o_ref[...] = x_ref[...] + _dot(ro_ref[...], wr_ref[...]) + _dot(ao_ref[...], wa_ref[...])


def _out_proj(x2d, ro, ao, w_out, tm=512):
    n_tok = x2d.shape[0]
    tile = pl.BlockSpec((tm, D_MODEL), lambda i: (i, 0))
    half = pl.BlockSpec((tm, RET_W), lambda i: (i, 0))
    w_ret = pl.BlockSpec((RET_W, D_MODEL), lambda i: (0, 0))
    w_att = pl.BlockSpec((ATT_W, D_MODEL), lambda i: (RET_W // ATT_W, 0))
    return pl.pallas_call(
        _out_proj_kernel,
        grid=(n_tok // tm,),
        in_specs=[tile, half, half, w_ret, w_att],
        out_specs=tile,
        out_shape=jax.ShapeDtypeStruct((n_tok, D_MODEL), F32),
        compiler_params=_params(("parallel",)),
        name="out_proj",
    )(x2d, ro, ao, w_out, w_out)


def _layer(x2d, tables, batch, seq, ffn1_norm, ffn1_w_gate, ffn1_w_up, ffn1_w_down, mix_norm,
           w_in_groups, ret_norm, w_out, ffn2_norm, ffn2_w_gate, ffn2_w_up, ffn2_w_down, final_norm, last):
    cos_a, sin_a, cos_b, sin_b = tables
    w_rope, w_plain, w_idx = w_in_groups
    row = lambda g: g.reshape(1, -1).astype(F32)

    x1, h = _ffn(x2d, row(ffn1_norm), ffn1_w_gate, ffn1_w_up, ffn1_w_down,
                 row(mix_norm), emit_residual=True, tf=FFN_TF_F32)
    rope_out = _proj(h, w_rope, cos_a, sin_a, mode="rope128", tm=512, out_dtype=BF16)
    plain_out = _proj(h, w_plain, cos_a, sin_a, mode="plain", tm=512, out_dtype=BF16)
    idx_out = _proj(h, w_idx, cos_b, sin_b, mode="idx", tm=1024, out_dtype=BF16)
    ro = _retention(rope_out, plain_out, row(ret_norm), batch, seq)
    ao, (w_out16, wg16, wu16, wd16) = _sparse_attention(
        idx_out, rope_out, plain_out, batch, seq, [w_out, ffn2_w_gate, ffn2_w_up, ffn2_w_down])
    x2 = _out_proj(x1, ro, ao, w_out16)
    out = _ffn(x2, row(ffn2_norm), wg16, wu16, wd16, row(final_norm), emit_residual=not last,
               tf=FFN_TF_BF16)
    return out if last else out[0]


def kernel(x, positions, ffn1_norm, ffn1_w_gate, ffn1_w_up, ffn1_w_down, mix_norm, w_in, ret_norm,
           w_out, ffn2_norm, ffn2_w_gate, ffn2_w_up, ffn2_w_down, final_norm):
    batch, seq, _ = x.shape
    depth = w_in.shape[0]
    tables, w_in_groups = _prepare(positions, jnp.swapaxes(w_in, 1, 2))
    x2d = x.reshape(batch * seq, D_MODEL)
    for l in range(depth):
        last = l == depth - 1
        groups_l = [w[l] for w in w_in_groups]
        x2d = _layer(x2d, tables, batch, seq, ffn1_norm[l], ffn1_w_gate[l], ffn1_w_up[l], ffn1_w_down[l],
                     mix_norm[l], groups_l, ret_norm[l], w_out[l], ffn2_norm[l], ffn2_w_gate[l],
                     ffn2_w_up[l], ffn2_w_down[l], final_norm, last)
    return x2d.reshape(batch, seq, D_MODEL)
```

```python
import functools

import jax
import jax.numpy as jnp
from jax import lax
from jax.experimental import pallas as pl
from jax.experimental.pallas import tpu as pltpu

D_MODEL = 2048
H_RET = 8
DK_RET = 128
DV_RET = 128
RET_CHUNK = 128
H_ATT = 8
H_KV = 2
D_HEAD = 128
H_IDX = 16
D_IDX = 64
MAX_TOPK = 256
Q_BLOCK = 256
D_FF = 5632
ROPE_THETA = 10000.0
NORM_EPS = 1e-6

RET_W = H_RET * DV_RET
ATT_W = H_ATT * D_HEAD
GROUP = H_ATT // H_KV

LANES = 128
SUBLANES = 8
VMEM_LIMIT = 60 * 1024 * 1024

ROPE_COLS = 2 * H_RET * DK_RET + ATT_W + H_KV * D_HEAD
PLAIN_COLS = 2 * RET_W + H_KV * D_HEAD
IDX_Q_COLS = H_IDX * D_IDX
IDX_COLS = IDX_Q_COLS + 2 * LANES
AQ_CHUNK0 = 2 * H_RET * DK_RET // LANES

INT_MIN = -2 ** 31
NEG_BIG = -1e30
LOG2_E = 1.4426950408889634

F32 = jnp.float32
BF16 = jnp.bfloat16


def _dot(a, b):
    return jnp.dot(a, b, preferred_element_type=F32)


def _dot_nt(a, b):
    return lax.dot_general(a, b, (((1,), (1,)), ((), ())), preferred_element_type=F32)


def _dot_tn(a, b):
    return lax.dot_general(a, b, (((0,), (0,)), ((), ())), preferred_element_type=F32)


def _rmsnorm(xf, g):
    ms = jnp.mean(xf * xf, axis=-1, keepdims=True)
    return xf * lax.rsqrt(ms + NORM_EPS) * g


def _params(sem):
    return pltpu.CompilerParams(dimension_semantics=sem, vmem_limit_bytes=VMEM_LIMIT)


_IN_SIZES = (H_RET * DK_RET, H_RET * DK_RET, RET_W, RET_W, ATT_W, H_KV * D_HEAD, H_KV * D_HEAD,
             H_IDX * D_IDX, D_IDX, H_IDX)
W_IN_SPLITS = tuple(sum(_IN_SIZES[:n]) for n in range(len(_IN_SIZES) + 1))


def _prep_kernel(pos_ref, inv_ref, sgn_a_ref, sgn_b_ref, w_in_ref,
                 cos_a_ref, sin_a_ref, cos_b_ref, sin_b_ref, w_rope_ref, w_plain_ref, w_idx_ref):
    ang = pos_ref[...].astype(F32) * inv_ref[...]
    lane = lax.broadcasted_iota(jnp.int32, ang.shape, 1)
    half, quarter = D_HEAD // 2, D_IDX // 2

    def table_a(t):
        return jnp.where(lane < half, t, pltpu.roll(t, half, axis=1))

    def table_b(t):
        upper = jnp.where(lane < half + quarter, t, pltpu.roll(t, quarter, axis=1))
        return jnp.where(lane >= half, upper, pltpu.roll(upper, half, axis=1))

    cos, sin = jnp.cos(ang), jnp.sin(ang)
    cos_a_ref[...] = table_a(cos)
    sin_a_ref[...] = table_a(sin) * sgn_a_ref[...]
    cos_b_ref[...] = table_b(cos)
    sin_b_ref[...] = table_b(sin) * sgn_b_ref[...]

    rq, rk, rv, rg, aq, ak, av, iq, ik, iw, end = W_IN_SPLITS
    for layer in range(w_in_ref.shape[0]):
        feats = lambda lo, hi: w_in_ref[layer, lo:hi, :].astype(BF16)
        w_rope_ref[layer, :rv - rq, :] = feats(rq, rv)
        w_rope_ref[layer, rv - rq:, :] = feats(aq, av)
        w_plain_ref[layer, :aq - rv, :] = feats(rv, aq)
        w_plain_ref[layer, aq - rv:, :] = feats(av, iq)
        w_idx_ref[layer, :ik - iq, :] = feats(iq, ik)
        key = feats(ik, iw)
        w_idx_ref[layer, ik - iq:ik - iq + D_IDX, :] = key
        w_idx_ref[layer, ik - iq + D_IDX:ik - iq + LANES, :] = key
        w_idx_ref[layer, ik - iq + LANES:ik - iq + LANES + H_IDX, :] = feats(iw, end)
        w_idx_ref[layer, ik - iq + LANES + H_IDX:, :] = jnp.zeros(
            (LANES - H_IDX, w_idx_ref.shape[2]), BF16)


def _prepare(positions, w_in_t):
    n_tok = positions.size
    tm = 2048
    n_steps = n_tok // tm
    depth, _, d_model = w_in_t.shape
    w_cols = d_model // n_steps
    lane = jnp.arange(LANES)

    def inv_freq(d):
        return ROPE_THETA ** (-jnp.arange(0, d, 2, dtype=F32) / d)

    unused = jnp.zeros((LANES - D_HEAD // 2 - D_IDX // 2,), F32)
    inv = jnp.concatenate([inv_freq(D_HEAD), inv_freq(D_IDX), unused])[None, :]

    def sign(d):
        return jnp.where(lane % d < d // 2, -1.0, 1.0).astype(F32)[None, :]

    row = pl.BlockSpec((1, LANES), lambda i: (0, 0))
    tab = pl.BlockSpec((tm, LANES), lambda i: (i, 0))
    out = jax.ShapeDtypeStruct((n_tok, LANES), F32)
    slab = lambda n_feats: pl.BlockSpec((depth, n_feats, w_cols), lambda i: (0, 0, i))
    w_out = lambda n_feats: jax.ShapeDtypeStruct((depth, n_feats, d_model), BF16)
    outs = pl.pallas_call(
        _prep_kernel,
        grid=(n_steps,),
        in_specs=[pl.BlockSpec((tm, 1), lambda i: (i, 0)), row, row, row, slab(w_in_t.shape[1])],
        out_specs=[tab, tab, tab, tab, slab(ROPE_COLS), slab(PLAIN_COLS), slab(IDX_COLS)],
        out_shape=[out, out, out, out, w_out(ROPE_COLS), w_out(PLAIN_COLS), w_out(IDX_COLS)],
        compiler_params=_params(("parallel",)),
        name="prepare",
    )(positions.reshape(n_tok, 1), inv, sign(D_HEAD), sign(D_IDX), w_in_t)
    return outs[:4], outs[4:]


FFN_ROW_CHUNK = 128


def _ffn_kernel(x_ref, g_ref, wg_ref, wu_ref, wd_ref, g2_ref, *refs, emit_residual):
    if emit_residual:
        res_ref, normed_ref, xn_ref = refs
        acc_ref = res_ref
    else:
        normed_ref, xn_ref = refs
        acc_ref = normed_ref
    f = pl.program_id(1)
    n_row_chunks = x_ref.shape[0] // FFN_ROW_CHUNK

    def rows_of(r):
        return pl.ds(pl.multiple_of(r * FFN_ROW_CHUNK, FFN_ROW_CHUNK), FFN_ROW_CHUNK)

    @pl.when(f == 0)
    def _():
        def prologue(r, carry):
            xf = x_ref[rows_of(r), :]
            xn_ref[rows_of(r), :] = _rmsnorm(xf, g_ref[...]).astype(BF16)
            acc_ref[rows_of(r), :] = 2.0 * xf
            return carry
        lax.fori_loop(0, n_row_chunks, prologue, 0)

    xn = xn_ref[...]
    a = _dot(xn, wg_ref[...].astype(BF16))
    b = _dot(xn, wu_ref[...].astype(BF16))
    hidden = (a * jax.nn.sigmoid(a) * b).astype(BF16)
    acc_ref[...] += _dot(hidden, wd_ref[...].astype(BF16))

    @pl.when(f == pl.num_programs(1) - 1)
    def _():
        def epilogue(r, carry):
            y = 0.5 * acc_ref[rows_of(r), :]
            if emit_residual:
                res_ref[rows_of(r), :] = y
            normed_ref[rows_of(r), :] = _rmsnorm(y, g2_ref[...]).astype(normed_ref.dtype)
            return carry
        lax.fori_loop(0, n_row_chunks, epilogue, 0)


FFN_TM = 1024
FFN_TF_F32 = 256
FFN_TF_BF16 = 512


def _ffn(x2d, g, wg, wu, wd, g2, *, emit_residual, tf, tm=FFN_TM):
    n_tok = x2d.shape[0]
    tile_map = lambda i, f: (i, 0)
    x_tile = pl.BlockSpec((tm, D_MODEL), tile_map)
    out_tile = pl.BlockSpec((tm, D_MODEL), tile_map)
    normed_tile = out_tile
    row = pl.BlockSpec((1, D_MODEL), lambda i, f: (0, 0))
    if emit_residual:
        out_specs = [out_tile, normed_tile]
        out_shape = [jax.ShapeDtypeStruct((n_tok, D_MODEL), F32),
                     jax.ShapeDtypeStruct((n_tok, D_MODEL), BF16)]
    else:
        out_specs = out_tile
        out_shape = jax.ShapeDtypeStruct((n_tok, D_MODEL), F32)
    def column_tile(w):
        if w.ndim == 3:
            return pl.BlockSpec((None, D_MODEL, tf), lambda i, f: (f, 0, 0))
        return pl.BlockSpec((D_MODEL, tf), lambda i, f: (0, f))

    return pl.pallas_call(
        functools.partial(_ffn_kernel, emit_residual=emit_residual),
        grid=(n_tok // tm, D_FF // tf),
        in_specs=[x_tile, row, column_tile(wg), column_tile(wu),
                  pl.BlockSpec((tf, D_MODEL), lambda i, f: (f, 0)),
                  row],
        out_specs=out_specs,
        out_shape=out_shape,
        scratch_shapes=[pltpu.VMEM((tm, D_MODEL), BF16)],
        compiler_params=_params(("parallel", "arbitrary")),
        name="ffn_residual" if emit_residual else "ffn_final",
    )(x2d, g, wg, wu, wd, g2)


def _rope128(y, cos, sin):
    return y * cos + pltpu.roll(y, D_HEAD // 2, axis=1) * sin


def _rope64(y, cos, sin):
    lane = lax.broadcasted_iota(jnp.int32, y.shape, 1)
    first_half = (lane & (D_IDX - 1)) < D_IDX // 2
    partner = jnp.where(first_half,
                        pltpu.roll(y, LANES - D_IDX // 2, axis=1),
                        pltpu.roll(y, D_IDX // 2, axis=1))
    return y * cos + partner * sin


def _proj_kernel(h_ref, w_ref, cos_ref, sin_ref, o_ref, *, mode):
    y = _dot_nt(h_ref[...], w_ref[...])
    n_chunks = y.shape[1] // LANES
    if mode == "plain":
        o_ref[...] = y.astype(o_ref.dtype)
        return
    cos = cos_ref[...]
    sin = sin_ref[...]
    for c in range(n_chunks):
        yc = y[:, c * LANES:(c + 1) * LANES]
        if mode == "rope128":
            yc = _rope128(yc, cos, sin)
            if AQ_CHUNK0 <= c < AQ_CHUNK0 + H_ATT:
                yc = yc * (LOG2_E * D_HEAD ** -0.5)
        elif c < IDX_Q_COLS // LANES:
            yc = _rope64(yc, cos, sin) * (D_IDX ** -0.5)
        elif c == IDX_Q_COLS // LANES:
            yc = _rope64(yc, cos, sin)
        else:
            yc = yc * (H_IDX ** -0.5)
        o_ref[:, c * LANES:(c + 1) * LANES] = yc.astype(o_ref.dtype)


def _proj(h, w, cos, sin, *, mode, tm, out_dtype):
    n_tok, n_cols = h.shape[0], w.shape[0]
    tab = pl.BlockSpec((tm, LANES), lambda i: (i, 0))
    return pl.pallas_call(
        functools.partial(_proj_kernel, mode=mode),
        grid=(n_tok // tm,),
        in_specs=[pl.BlockSpec((tm, D_MODEL), lambda i: (i, 0)),
                  pl.BlockSpec((n_cols, D_MODEL), lambda i: (0, 0), pipeline_mode=pl.Buffered(1)),
                  tab, tab],
        out_specs=pl.BlockSpec((tm, n_cols), lambda i: (i, 0)),
        out_shape=jax.ShapeDtypeStruct((n_tok, n_cols), out_dtype),
        compiler_params=_params(("parallel",)),
        name="proj_" + mode,
    )(h, w, cos, sin)


RET_UNROLL = 8
RET_HEADS_PER_STEP = 2


def _retention_kernel(lg_ref, q_ref, k_ref, v_ref, g_ref, rn_ref, o_ref):
    C = RET_CHUNK
    n_chunks = q_ref.shape[0] // C
    row = lax.broadcasted_iota(jnp.int32, (C, C), 0).astype(F32)
    col = lax.broadcasted_iota(jnp.int32, (C, C), 1).astype(F32)
    diff = row - col
    scale = DK_RET ** -0.5
    heads = []
    for j in range(RET_HEADS_PER_STEP):
        lg = lg_ref[j]
        heads.append(dict(
            lanes=slice(j * LANES, (j + 1) * LANES),
            decay=jnp.where(diff >= 0, jnp.exp(jnp.maximum(diff, 0.0) * lg), 0.0) * scale,
            k_dec=jnp.exp((C - 1 - row) * lg) * scale,
            q_dec=jnp.exp((row + 1) * lg),
            g_chunk=jnp.exp(C * lg)))

    def body(n, states):
        sl = pl.ds(pl.multiple_of(n * C, C), C)
        new_states = []
        for hd, state in zip(heads, states):
            qc = q_ref[sl, hd["lanes"]]
            kc = k_ref[sl, hd["lanes"]]
            vc = v_ref[sl, hd["lanes"]]
            s = _dot_nt(qc, kc) * hd["decay"]
            intra = _dot(s.astype(BF16), vc)
            cross = _dot(qc, state.astype(BF16)) * hd["q_dec"]
            kv = _dot_tn((kc.astype(F32) * hd["k_dec"]).astype(BF16), vc)
            o = intra + cross
            mu = jnp.mean(o, axis=-1, keepdims=True)
            d = o - mu
            var = jnp.mean(d * d, axis=-1, keepdims=True)
            y = d * lax.rsqrt(var + NORM_EPS) * rn_ref[:, hd["lanes"]]
            gate = g_ref[sl, hd["lanes"]].astype(F32)
            o_ref[sl, hd["lanes"]] = (y * (gate * jax.nn.sigmoid(gate))).astype(o_ref.dtype)
            new_states.append(state * hd["g_chunk"] + kv)
        return tuple(new_states)

    init = tuple(jnp.zeros((DK_RET, DV_RET), F32) for _ in heads)
    lax.fori_loop(0, n_chunks, body, init, unroll=RET_UNROLL)


def _retention(rope_out, plain_out, ret_norm, batch, seq):
    n_tok = batch * seq
    lg = jnp.log1p(-jnp.exp2(-5.0 - jnp.arange(H_RET, dtype=F32)))
    lg = jnp.broadcast_to(lg[:, None, None], (H_RET, 1, LANES))
    width = RET_HEADS_PER_STEP * LANES
    n_groups = H_RET // RET_HEADS_PER_STEP
    heads = lambda off: pl.BlockSpec((seq, width), lambda b, h: (b, off + h))
    return pl.pallas_call(
        _retention_kernel,
        grid=(batch, n_groups),
        in_specs=[pl.BlockSpec((RET_HEADS_PER_STEP, 1, LANES), lambda b, h: (h, 0, 0)),
                  heads(0), heads(n_groups),
                  heads(0), heads(n_groups),
                  pl.BlockSpec((1, width), lambda b, h: (0, h))],
        out_specs=heads(0),
        out_shape=jax.ShapeDtypeStruct((n_tok, RET_W), BF16),
        compiler_params=_params(("parallel", "parallel")),
        name="retention",
    )(lg, rope_out, rope_out, plain_out, plain_out, ret_norm)


KEY_TILE = 512
SCORE_ROWS = 128
FLT_MAX_KEY = 0x00800000
KEY_BITS = 32
FLT_MIN = 1.1754943508222875e-38
ATT_HEADS_PER_CHAIN = 2
ATT_LOOKAHEAD = 2


def _order_key_to_float(u):
    s = u ^ INT_MIN
    return pltpu.bitcast(s ^ ((s >> 31) & jnp.int32(0x7FFFFFFF)), F32)


def _attn_kernel(iq_ref, ik_ref, iw_ref, aq_ref, ak_ref, av_ref, *rest, top_k, n_cast):
    o_ref, sc_ref, sc16_ref = rest[n_cast], rest[-2], rest[-1]
    for w32_ref, w16_ref in zip(rest[:n_cast], rest[n_cast + 1:-2]):
        if len(w16_ref.shape) == 2:
            w16_ref[...] = w32_ref[...].astype(BF16)
        else:
            width = w16_ref.shape[2]
            for t in range(w16_ref.shape[0]):
                w16_ref[t] = w32_ref[:, t * width:(t + 1) * width].astype(BF16)

    qb = pl.program_id(1)
    t0 = qb * Q_BLOCK
    n_tiles = (t0 + Q_BLOCK + KEY_TILE - 1) // KEY_TILE

    w_t = iw_ref[...].astype(F32).T
    w_rows = [w_t[h:h + 1, :] for h in range(H_IDX)]

    lane = lax.broadcasted_iota(jnp.int32, (Q_BLOCK, LANES), 1)
    low = lane < D_IDX
    q_pairs = []
    for c in range(IDX_Q_COLS // LANES):
        qc = iq_ref[:, c * LANES:(c + 1) * LANES]
        zero = jnp.zeros_like(qc)
        q_pairs.append(jnp.concatenate([jnp.where(low, qc, zero), jnp.where(low, zero, qc)], axis=0))

    q_pos = t0 + lax.broadcasted_iota(jnp.int32, (SCORE_ROWS, Q_BLOCK), 1)
    key_off = lax.broadcasted_iota(jnp.int32, (SCORE_ROWS, Q_BLOCK), 0)

    def score_tile(kt, carry):
        for s in range(KEY_TILE // SCORE_ROWS):
            base = pl.multiple_of(kt * KEY_TILE + s * SCORE_ROWS, SCORE_ROWS)
            kk = ik_ref[pl.ds(base, SCORE_ROWS), :]
            acc = jnp.zeros((SCORE_ROWS, Q_BLOCK), F32)
            for c, qp in enumerate(q_pairs):
                z = _dot_nt(kk, qp)
                acc = acc + jnp.maximum(z[:, :Q_BLOCK], 0.0) * w_rows[2 * c]
                acc = acc + jnp.maximum(z[:, Q_BLOCK:], 0.0) * w_rows[2 * c + 1]
            causal = base + key_off <= q_pos
            score = jnp.where(causal, acc, -jnp.inf)
            sc_ref[pl.ds(base, SCORE_ROWS), :] = score
            sc16_ref[pl.ds(base, SCORE_ROWS), :] = score.astype(BF16)
        return carry

    lax.fori_loop(0, n_tiles, score_tile, 0)

    def count_tiles(ref, indicator):
        rows = SUBLANES * 4 // ref.dtype.itemsize

        def count_tile(kt, cnt):
            base = pl.multiple_of(kt * KEY_TILE, KEY_TILE)
            part = indicator(ref[pl.ds(base, KEY_TILE), :], base)
            part = part.reshape(KEY_TILE // rows, rows, Q_BLOCK)
            while part.shape[0] > 1:
                half = part.shape[0] // 2
                part = part[:half] + part[half:]
            return cnt + part[0].astype(F32)

        cnt = lax.fori_loop(0, n_tiles, count_tile, jnp.zeros((rows, Q_BLOCK), F32))
        return jnp.sum(cnt, axis=0, keepdims=True)

    def count_ge(ref, cand):
        one, zero = jnp.ones((), ref.dtype), jnp.zeros((), ref.dtype)
        return count_tiles(ref, lambda tile, base: jnp.where(tile >= cand, one, zero))

    def search_bits(ref, tau_u, first_bit, n_bits):
        def search_bit(it, tau_u):
            cand_u = tau_u | lax.shift_left(jnp.int32(1), first_bit - it)
            cand = _order_key_to_float(cand_u).astype(ref.dtype)
            return jnp.where(count_ge(ref, cand) >= top_k, cand_u, tau_u)
        return lax.fori_loop(0, n_bits, search_bit, tau_u)

    searched = t0 + Q_BLOCK > top_k
    half_bits = jnp.where(searched, KEY_BITS // 2, 0)
    rounded = search_bits(sc16_ref, jnp.zeros((1, Q_BLOCK), jnp.int32), KEY_BITS - 1, half_bits)
    fits = count_ge(sc_ref, _order_key_to_float(rounded)) >= top_k
    prefix = jnp.where(fits, rounded, rounded - (1 << KEY_BITS // 2))
    tau_u = search_bits(sc_ref, prefix, KEY_BITS // 2 - 1, half_bits)
    tau_u = jnp.where(searched, tau_u, FLT_MAX_KEY)
    tau = _order_key_to_float(tau_u)

    n_ge = count_ge(sc_ref, tau)
    surplus = jnp.max(n_ge) > top_k

    @pl.when(jnp.logical_and(searched, surplus))
    def _():
        key_row = lax.broadcasted_iota(jnp.int32, (KEY_TILE, Q_BLOCK), 0)

        def tied_before(cutoff):
            return lambda tile, base: jnp.where(
                tile == tau, jnp.where(base + key_row < cutoff, 1.0, 0.0), 0.0)

        n_eq = count_tiles(sc_ref, tied_before(jnp.int32(sc_ref.shape[0])))
        keep = top_k - (n_ge - n_eq)
        n_cut_bits = sc_ref.shape[0].bit_length()

        def cutoff_bit(it, cutoff):
            cand = cutoff | lax.shift_left(jnp.int32(1), n_cut_bits - 1 - it)
            return jnp.where(count_tiles(sc_ref, tied_before(cand)) <= keep, cand, cutoff)

        cutoff = lax.fori_loop(0, n_cut_bits, cutoff_bit, jnp.zeros((1, Q_BLOCK), jnp.int32))
        below = jnp.where(tau == 0.0, -FLT_MIN, _order_key_to_float(tau_u - 1))

        def demote(kt, carry):
            base = pl.multiple_of(kt * KEY_TILE, KEY_TILE)
            tile = sc_ref[pl.ds(base, KEY_TILE), :]
            demoted = jnp.where(base + key_row < cutoff, tile, below)
            sc_ref[pl.ds(base, KEY_TILE), :] = jnp.where(tile == tau, demoted, tile)
            return carry

        lax.fori_loop(0, n_tiles, demote, 0)

    n_chains = H_ATT // ATT_HEADS_PER_CHAIN
    width = ATT_HEADS_PER_CHAIN * Q_BLOCK
    q_chains = []
    for c in range(n_chains):
        heads = range(c * ATT_HEADS_PER_CHAIN, (c + 1) * ATT_HEADS_PER_CHAIN)
        q_chains.append(jnp.concatenate([aq_ref[:, h * D_HEAD:(h + 1) * D_HEAD] for h in heads], axis=0))

    def attend_tile(kt, carry):
        base = pl.multiple_of(kt * KEY_TILE, KEY_TILE)
        bias = jnp.where(sc_ref[pl.ds(base, KEY_TILE), :] >= tau, 0.0, NEG_BIG)
        bias = jnp.concatenate([bias] * ATT_HEADS_PER_CHAIN, axis=1)
        kv_of = lambda c: c * ATT_HEADS_PER_CHAIN // GROUP
        k_tiles = [ak_ref[pl.ds(base, KEY_TILE), g * D_HEAD:(g + 1) * D_HEAD] for g in range(H_KV)]
        v_tiles = [av_ref[pl.ds(base, KEY_TILE), g * D_HEAD:(g + 1) * D_HEAD] for g in range(H_KV)]
        score = lambda c: _dot_nt(k_tiles[kv_of(c)], q_chains[c]) + bias
        logits = [score(c) for c in range(min(ATT_LOOKAHEAD, n_chains))]
        new = []
        for c in range(n_chains):
            m_old, l_old, acc_old = carry[c]
            m_new = jnp.maximum(m_old, jnp.max(logits[c], axis=0, keepdims=True))
            alpha = jnp.exp2(m_old - m_new)
            p = jnp.exp2(logits[c] - m_new)
            l_new = alpha * l_old + jnp.sum(p, axis=0, keepdims=True)
            if c + ATT_LOOKAHEAD < n_chains:
                logits.append(score(c + ATT_LOOKAHEAD))
            acc_new = alpha * acc_old + _dot_tn(v_tiles[kv_of(c)], p.astype(BF16))
            new.append((m_new, l_new, acc_new))
        return tuple(new)

    init = tuple((jnp.full((1, width), NEG_BIG, F32), jnp.zeros((1, width), F32),
                  jnp.zeros((D_HEAD, width), F32)) for _ in range(n_chains))
    final = lax.fori_loop(0, n_tiles, attend_tile, init)
    for c in range(n_chains):
        _, l_fin, acc_fin = final[c]
        out_t = acc_fin / l_fin
        for j in range(ATT_HEADS_PER_CHAIN):
            hcol = (c * ATT_HEADS_PER_CHAIN + j) * D_HEAD
            o_ref[:, hcol:hcol + D_HEAD] = out_t[:, j * Q_BLOCK:(j + 1) * Q_BLOCK].T.astype(o_ref.dtype)


def _sparse_attention(idx_out, rope_out, plain_out, batch, seq, weights_to_cast, column_tiles):
    n_tok = batch * seq
    nb = seq // Q_BLOCK
    n_steps = batch * nb
    top_k = min(MAX_TOPK, seq // 4)
    kv_w = H_KV * D_HEAD
    step = lambda b, q: (b * nb + q, 0)
    slabs = [pl.BlockSpec((w.shape[0] // n_steps, w.shape[1]), step) for w in weights_to_cast]
    cast_specs, cast_shapes = [], []
    for w, width in zip(weights_to_cast, column_tiles):
        rows = w.shape[0] // n_steps
        if width is None:
            cast_specs.append(pl.BlockSpec((rows, w.shape[1]), step))
            cast_shapes.append(jax.ShapeDtypeStruct(w.shape, BF16))
        else:
            n_t = w.shape[1] // width
            cast_specs.append(pl.BlockSpec((n_t, rows, width), lambda b, q: (0, b * nb + q, 0)))
            cast_shapes.append(jax.ShapeDtypeStruct((n_t, w.shape[0], width), BF16))
    outs = pl.pallas_call(
        functools.partial(_attn_kernel, top_k=top_k, n_cast=len(weights_to_cast)),
        grid=(batch, nb),
        in_specs=[
            pl.BlockSpec((Q_BLOCK, IDX_Q_COLS), step),
            pl.BlockSpec((seq, LANES), lambda b, q: (b, IDX_Q_COLS // LANES)),
            pl.BlockSpec((Q_BLOCK, LANES), lambda b, q: (b * nb + q, IDX_Q_COLS // LANES + 1)),
            pl.BlockSpec((Q_BLOCK, ATT_W), lambda b, q: (b * nb + q, 2 * RET_W // ATT_W)),
            pl.BlockSpec((seq, kv_w), lambda b, q: (b, (2 * RET_W + ATT_W) // kv_w)),
            pl.BlockSpec((seq, kv_w), lambda b, q: (b, 2 * RET_W // kv_w)),
        ] + slabs,
        out_specs=[pl.BlockSpec((Q_BLOCK, ATT_W), step)] + cast_specs,
        out_shape=[jax.ShapeDtypeStruct((n_tok, ATT_W), BF16)] + cast_shapes,
        scratch_shapes=[pltpu.VMEM((seq, Q_BLOCK), F32), pltpu.VMEM((seq, Q_BLOCK), BF16)],
        compiler_params=_params(("parallel", "arbitrary")),
        name="sparse_attention",
    )(idx_out, idx_out, idx_out, rope_out, rope_out, plain_out, *weights_to_cast)
    return outs[0], outs[1:]


def _out_proj_kernel(x_ref, ro_ref, ao_ref, wr_ref, wa_ref, o_ref):
    o_ref[...] = x_ref[...] + _dot(ro_ref[...], wr_ref[...]) + _dot(ao_ref[...], wa_ref[...])


def _out_proj(x2d, ro, ao, w_out, tm=512):
    n_tok = x2d.shape[0]
    tile = pl.BlockSpec((tm, D_MODEL), lambda i: (i, 0))
    half = pl.BlockSpec((tm, RET_W), lambda i: (i, 0))
    w_ret = pl.BlockSpec((RET_W, D_MODEL), lambda i: (0, 0))
    w_att = pl.BlockSpec((ATT_W, D_MODEL), lambda i: (RET_W // ATT_W, 0))
    return pl.pallas_call(
        _out_proj_kernel,
        grid=(n_tok // tm,),
        in_specs=[tile, half, half, w_ret, w_att],
        out_specs=tile,
        out_shape=jax.ShapeDtypeStruct((n_tok, D_MODEL), F32),
        compiler_params=_params(("parallel",)),
        name="out_proj",
    )(x2d, ro, ao, w_out, w_out)


def _layer(x2d, tables, batch, seq, ffn1_norm, ffn1_w_gate, ffn1_w_up, ffn1_w_down, mix_norm,
           w_in_groups, ret_norm, w_out, ffn2_norm, ffn2_w_gate, ffn2_w_up, ffn2_w_down, final_norm, last):
    cos_a, sin_a, cos_b, sin_b = tables
    w_rope, w_plain, w_idx = w_in_groups
    row = lambda g: g.reshape(1, -1).astype(F32)

    x1, h = _ffn(x2d, row(ffn1_norm), ffn1_w_gate, ffn1_w_up, ffn1_w_down,
                 row(mix_norm), emit_residual=True, tf=FFN_TF_F32)
    rope_out = _proj(h, w_rope, cos_a, sin_a, mode="rope128", tm=512, out_dtype=BF16)
    plain_out = _proj(h, w_plain, cos_a, sin_a, mode="plain", tm=512, out_dtype=BF16)
    idx_out = _proj(h, w_idx, cos_b, sin_b, mode="idx", tm=1024, out_dtype=BF16)
    ro = _retention(rope_out, plain_out, row(ret_norm), batch, seq)
    ao, (w_out16, wg16, wu16, wd16) = _sparse_attention(
        idx_out, rope_out, plain_out, batch, seq, [w_out, ffn2_w_gate, ffn2_w_up, ffn2_w_down],
        column_tiles=[None, FFN_TF_BF16, FFN_TF_BF16, None])
    x2 = _out_proj(x1, ro, ao, w_out16)
    out = _ffn(x2, row(ffn2_norm), wg16, wu16, wd16, row(final_norm), emit_residual=not last,
               tf=FFN_TF_BF16)
    return out if last else out[0]


def kernel(x, positions, ffn1_norm, ffn1_w_gate, ffn1_w_up, ffn1_w_down, mix_norm, w_in, ret_norm,
           w_out, ffn2_norm, ffn2_w_gate, ffn2_w_up, ffn2_w_down, final_norm):
    batch, seq, _ = x.shape
    depth = w_in.shape[0]
    tables, w_in_groups = _prepare(positions, jnp.swapaxes(w_in, 1, 2))
    x2d = x.reshape(batch * seq, D_MODEL)
    for l in range(depth):
        last = l == depth - 1
        groups_l = [w[l] for w in w_in_groups]
        x2d = _layer(x2d, tables, batch, seq, ffn1_norm[l], ffn1_w_gate[l], ffn1_w_up[l], ffn1_w_down[l],
                     mix_norm[l], groups_l, ret_norm[l], w_out[l], ffn2_norm[l], ffn2_w_gate[l],
                     ffn2_w_up[l], ffn2_w_down[l], final_norm, last)
    return x2d.reshape(batch, seq, D_MODEL)
```

```python
import functools

import jax
import jax.numpy as jnp
from jax import lax
from jax.experimental import pallas as pl
from jax.experimental.pallas import tpu as pltpu

D_MODEL = 2048
H_RET = 8
DK_RET = 128
DV_RET = 128
RET_CHUNK = 128
H_ATT = 8
H_KV = 2
D_HEAD = 128
H_IDX = 16
D_IDX = 64
MAX_TOPK = 256
Q_BLOCK = 256
D_FF = 5632
ROPE_THETA = 10000.0
NORM_EPS = 1e-6

RET_W = H_RET * DV_RET
ATT_W = H_ATT * D_HEAD
GROUP = H_ATT // H_KV

LANES = 128
SUBLANES = 8
VMEM_LIMIT = 60 * 1024 * 1024

ROPE_COLS = 2 * H_RET * DK_RET + ATT_W + H_KV * D_HEAD
PLAIN_COLS = 2 * RET_W + H_KV * D_HEAD
IDX_Q_COLS = H_IDX * D_IDX
IDX_COLS = IDX_Q_COLS + 2 * LANES
AQ_CHUNK0 = 2 * H_RET * DK_RET // LANES

INT_MIN = -2 ** 31
NEG_BIG = -1e30
LOG2_E = 1.4426950408889634

F32 = jnp.float32
BF16 = jnp.bfloat16


def _dot(a, b):
    return jnp.dot(a, b, preferred_element_type=F32)


def _dot_nt(a, b):
    return lax.dot_general(a, b, (((1,), (1,)), ((), ())), preferred_element_type=F32)


def _dot_tn(a, b):
    return lax.dot_general(a, b, (((0,), (0,)), ((), ())), preferred_element_type=F32)


def _rmsnorm(xf, g):
    ms = jnp.mean(xf * xf, axis=-1, keepdims=True)
    return xf * lax.rsqrt(ms + NORM_EPS) * g


def _params(sem):
    return pltpu.CompilerParams(dimension_semantics=sem, vmem_limit_bytes=VMEM_LIMIT)


_IN_SIZES = (H_RET * DK_RET, H_RET * DK_RET, RET_W, RET_W, ATT_W, H_KV * D_HEAD, H_KV * D_HEAD,
             H_IDX * D_IDX, D_IDX, H_IDX)
W_IN_SPLITS = tuple(sum(_IN_SIZES[:n]) for n in range(len(_IN_SIZES) + 1))


def _prep_kernel(pos_ref, inv_ref, sgn_a_ref, sgn_b_ref, w_in_ref,
                 cos_a_ref, sin_a_ref, cos_b_ref, sin_b_ref, w_rope_ref, w_plain_ref, w_idx_ref):
    ang = pos_ref[...].astype(F32) * inv_ref[...]
    lane = lax.broadcasted_iota(jnp.int32, ang.shape, 1)
    half, quarter = D_HEAD // 2, D_IDX // 2

    def table_a(t):
        return jnp.where(lane < half, t, pltpu.roll(t, half, axis=1))

    def table_b(t):
        upper = jnp.where(lane < half + quarter, t, pltpu.roll(t, quarter, axis=1))
        return jnp.where(lane >= half, upper, pltpu.roll(upper, half, axis=1))

    cos, sin = jnp.cos(ang), jnp.sin(ang)
    cos_a_ref[...] = table_a(cos)
    sin_a_ref[...] = table_a(sin) * sgn_a_ref[...]
    cos_b_ref[...] = table_b(cos)
    sin_b_ref[...] = table_b(sin) * sgn_b_ref[...]

    rq, rk, rv, rg, aq, ak, av, iq, ik, iw, end = W_IN_SPLITS
    for layer in range(w_in_ref.shape[0]):
        feats = lambda lo, hi: w_in_ref[layer, lo:hi, :].astype(BF16)
        w_rope_ref[layer, :rv - rq, :] = feats(rq, rv)
        w_rope_ref[layer, rv - rq:, :] = feats(aq, av)
        w_plain_ref[layer, :aq - rv, :] = feats(rv, aq)
        w_plain_ref[layer, aq - rv:, :] = feats(av, iq)
        w_idx_ref[layer, :ik - iq, :] = feats(iq, ik)
        key = feats(ik, iw)
        w_idx_ref[layer, ik - iq:ik - iq + D_IDX, :] = key
        w_idx_ref[layer, ik - iq + D_IDX:ik - iq + LANES, :] = key
        w_idx_ref[layer, ik - iq + LANES:ik - iq + LANES + H_IDX, :] = feats(iw, end)
        w_idx_ref[layer, ik - iq + LANES + H_IDX:, :] = jnp.zeros(
            (LANES - H_IDX, w_idx_ref.shape[2]), BF16)


def _prepare(positions, w_in_t):
    n_tok = positions.size
    tm = 1024
    n_steps = n_tok // tm
    depth, _, d_model = w_in_t.shape
    w_cols = d_model // n_steps
    lane = jnp.arange(LANES)

    def inv_freq(d):
        return ROPE_THETA ** (-jnp.arange(0, d, 2, dtype=F32) / d)

    unused = jnp.zeros((LANES - D_HEAD // 2 - D_IDX // 2,), F32)
    inv = jnp.concatenate([inv_freq(D_HEAD), inv_freq(D_IDX), unused])[None, :]

    def sign(d):
        return jnp.where(lane % d < d // 2, -1.0, 1.0).astype(F32)[None, :]

    row = pl.BlockSpec((1, LANES), lambda i: (0, 0))
    tab = pl.BlockSpec((tm, LANES), lambda i: (i, 0))
    out = jax.ShapeDtypeStruct((n_tok, LANES), F32)
    slab = lambda n_feats: pl.BlockSpec((depth, n_feats, w_cols), lambda i: (0, 0, i))
    w_out = lambda n_feats: jax.ShapeDtypeStruct((depth, n_feats, d_model), BF16)
    outs = pl.pallas_call(
        _prep_kernel,
        grid=(n_steps,),
        in_specs=[pl.BlockSpec((tm, 1), lambda i: (i, 0)), row, row, row, slab(w_in_t.shape[1])],
        out_specs=[tab, tab, tab, tab, slab(ROPE_COLS), slab(PLAIN_COLS), slab(IDX_COLS)],
        out_shape=[out, out, out, out, w_out(ROPE_COLS), w_out(PLAIN_COLS), w_out(IDX_COLS)],
        compiler_params=_params(("parallel",)),
        name="prepare",
    )(positions.reshape(n_tok, 1), inv, sign(D_HEAD), sign(D_IDX), w_in_t)
    return outs[:4], outs[4:]


FFN_ROW_CHUNK = 128


def _ffn_kernel(x_ref, g_ref, wg_ref, wu_ref, wd_ref, g2_ref, *refs, emit_residual):
    if emit_residual:
        res_ref, normed_ref, xn_ref = refs
        acc_ref = res_ref
    else:
        normed_ref, xn_ref = refs
        acc_ref = normed_ref
    f = pl.program_id(1)
    n_row_chunks = x_ref.shape[0] // FFN_ROW_CHUNK

    def rows_of(r):
        return pl.ds(pl.multiple_of(r * FFN_ROW_CHUNK, FFN_ROW_CHUNK), FFN_ROW_CHUNK)

    @pl.when(f == 0)
    def _():
        def prologue(r, carry):
            xf = x_ref[rows_of(r), :]
            xn_ref[rows_of(r), :] = _rmsnorm(xf, g_ref[...]).astype(BF16)
            acc_ref[rows_of(r), :] = 2.0 * xf
            return carry
        lax.fori_loop(0, n_row_chunks, prologue, 0)

    xn = xn_ref[...]
    a = _dot(xn, wg_ref[...].astype(BF16))
    b = _dot(xn, wu_ref[...].astype(BF16))
    hidden = (a * jax.nn.sigmoid(a) * b).astype(BF16)
    acc_ref[...] += _dot(hidden, wd_ref[...].astype(BF16))

    @pl.when(f == pl.num_programs(1) - 1)
    def _():
        def epilogue(r, carry):
            y = 0.5 * acc_ref[rows_of(r), :]
            if emit_residual:
                res_ref[rows_of(r), :] = y
            normed_ref[rows_of(r), :] = _rmsnorm(y, g2_ref[...]).astype(normed_ref.dtype)
            return carry
        lax.fori_loop(0, n_row_chunks, epilogue, 0)


FFN_TM = 1024
FFN_TF_F32 = 256
FFN_TF_BF16 = 512


def _ffn(x2d, g, wg, wu, wd, g2, *, emit_residual, tf, tm=FFN_TM):
    n_tok = x2d.shape[0]
    tile_map = lambda i, f: (i, 0)
    x_tile = pl.BlockSpec((tm, D_MODEL), tile_map)
    out_tile = pl.BlockSpec((tm, D_MODEL), tile_map)
    normed_tile = out_tile
    row = pl.BlockSpec((1, D_MODEL), lambda i, f: (0, 0))
    if emit_residual:
        out_specs = [out_tile, normed_tile]
        out_shape = [jax.ShapeDtypeStruct((n_tok, D_MODEL), F32),
                     jax.ShapeDtypeStruct((n_tok, D_MODEL), BF16)]
    else:
        out_specs = out_tile
        out_shape = jax.ShapeDtypeStruct((n_tok, D_MODEL), F32)
    return pl.pallas_call(
        functools.partial(_ffn_kernel, emit_residual=emit_residual),
        grid=(n_tok // tm, D_FF // tf),
        in_specs=[x_tile, row,
                  pl.BlockSpec((D_MODEL, tf), lambda i, f: (0, f)),
                  pl.BlockSpec((D_MODEL, tf), lambda i, f: (0, f)),
                  pl.BlockSpec((tf, D_MODEL), lambda i, f: (f, 0)),
                  row],
        out_specs=out_specs,
        out_shape=out_shape,
        scratch_shapes=[pltpu.VMEM((tm, D_MODEL), BF16)],
        compiler_params=_params(("parallel", "arbitrary")),
        name="ffn_residual" if emit_residual else "ffn_final",
    )(x2d, g, wg, wu, wd, g2)


def _rope128(y, cos, sin):
    return y * cos + pltpu.roll(y, D_HEAD // 2, axis=1) * sin


def _rope64(y, cos, sin):
    lane = lax.broadcasted_iota(jnp.int32, y.shape, 1)
    first_half = (lane & (D_IDX - 1)) < D_IDX // 2
    partner = jnp.where(first_half,
                        pltpu.roll(y, LANES - D_IDX // 2, axis=1),
                        pltpu.roll(y, D_IDX // 2, axis=1))
    return y * cos + partner * sin


def _proj_kernel(h_ref, w_ref, cos_ref, sin_ref, o_ref, *, mode):
    y = _dot_nt(h_ref[...], w_ref[...])
    n_chunks = y.shape[1] // LANES
    if mode == "plain":
        o_ref[...] = y.astype(o_ref.dtype)
        return
    cos = cos_ref[...]
    sin = sin_ref[...]
    for c in range(n_chunks):
        yc = y[:, c * LANES:(c + 1) * LANES]
        if mode == "rope128":
            yc = _rope128(yc, cos, sin)
            if AQ_CHUNK0 <= c < AQ_CHUNK0 + H_ATT:
                yc = yc * (LOG2_E * D_HEAD ** -0.5)
        elif c < IDX_Q_COLS // LANES:
            yc = _rope64(yc, cos, sin) * (D_IDX ** -0.5)
        elif c == IDX_Q_COLS // LANES:
            yc = _rope64(yc, cos, sin)
        else:
            yc = yc * (H_IDX ** -0.5)
        o_ref[:, c * LANES:(c + 1) * LANES] = yc.astype(o_ref.dtype)


def _proj(h, w, cos, sin, *, mode, tm, out_dtype):
    n_tok, n_cols = h.shape[0], w.shape[0]
    tab = pl.BlockSpec((tm, LANES), lambda i: (i, 0))
    return pl.pallas_call(
        functools.partial(_proj_kernel, mode=mode),
        grid=(n_tok // tm,),
        in_specs=[pl.BlockSpec((tm, D_MODEL), lambda i: (i, 0)),
                  pl.BlockSpec((n_cols, D_MODEL), lambda i: (0, 0), pipeline_mode=pl.Buffered(1)),
                  tab, tab],
        out_specs=pl.BlockSpec((tm, n_cols), lambda i: (i, 0)),
        out_shape=jax.ShapeDtypeStruct((n_tok, n_cols), out_dtype),
        compiler_params=_params(("parallel",)),
        name="proj_" + mode,
    )(h, w, cos, sin)


RET_UNROLL = 8
RET_HEADS_PER_STEP = 2


def _retention_kernel(lg_ref, q_ref, k_ref, v_ref, g_ref, rn_ref, o_ref):
    C = RET_CHUNK
    n_chunks = q_ref.shape[0] // C
    row = lax.broadcasted_iota(jnp.int32, (C, C), 0).astype(F32)
    col = lax.broadcasted_iota(jnp.int32, (C, C), 1).astype(F32)
    diff = row - col
    scale = DK_RET ** -0.5
    heads = []
    for j in range(RET_HEADS_PER_STEP):
        lg = lg_ref[j]
        heads.append(dict(
            lanes=slice(j * LANES, (j + 1) * LANES),
            decay=jnp.where(diff >= 0, jnp.exp(jnp.maximum(diff, 0.0) * lg), 0.0) * scale,
            k_dec=jnp.exp((C - 1 - row) * lg) * scale,
            q_dec=jnp.exp((row + 1) * lg),
            g_chunk=jnp.exp(C * lg)))

    def body(n, states):
        sl = pl.ds(pl.multiple_of(n * C, C), C)
        new_states = []
        for hd, state in zip(heads, states):
            qc = q_ref[sl, hd["lanes"]]
            kc = k_ref[sl, hd["lanes"]]
            vc = v_ref[sl, hd["lanes"]]
            s = _dot_nt(qc, kc) * hd["decay"]
            intra = _dot(s.astype(BF16), vc)
            cross = _dot(qc, state.astype(BF16)) * hd["q_dec"]
            kv = _dot_tn((kc.astype(F32) * hd["k_dec"]).astype(BF16), vc)
            o = intra + cross
            mu = jnp.mean(o, axis=-1, keepdims=True)
            d = o - mu
            var = jnp.mean(d * d, axis=-1, keepdims=True)
            y = d * lax.rsqrt(var + NORM_EPS) * rn_ref[:, hd["lanes"]]
            gate = g_ref[sl, hd["lanes"]].astype(F32)
            o_ref[sl, hd["lanes"]] = (y * (gate * jax.nn.sigmoid(gate))).astype(o_ref.dtype)
            new_states.append(state * hd["g_chunk"] + kv)
        return tuple(new_states)

    init = tuple(jnp.zeros((DK_RET, DV_RET), F32) for _ in heads)
    lax.fori_loop(0, n_chunks, body, init, unroll=RET_UNROLL)


def _retention(rope_out, plain_out, ret_norm, batch, seq):
    n_tok = batch * seq
    lg = jnp.log1p(-jnp.exp2(-5.0 - jnp.arange(H_RET, dtype=F32)))
    lg = jnp.broadcast_to(lg[:, None, None], (H_RET, 1, LANES))
    width = RET_HEADS_PER_STEP * LANES
    n_groups = H_RET // RET_HEADS_PER_STEP
    heads = lambda off: pl.BlockSpec((seq, width), lambda b, h: (b, off + h))
    return pl.pallas_call(
        _retention_kernel,
        grid=(batch, n_groups),
        in_specs=[pl.BlockSpec((RET_HEADS_PER_STEP, 1, LANES), lambda b, h: (h, 0, 0)),
                  heads(0), heads(n_groups),
                  heads(0), heads(n_groups),
                  pl.BlockSpec((1, width), lambda b, h: (0, h))],
        out_specs=heads(0),
        out_shape=jax.ShapeDtypeStruct((n_tok, RET_W), BF16),
        compiler_params=_params(("parallel", "parallel")),
        name="retention",
    )(lg, rope_out, rope_out, plain_out, plain_out, ret_norm)


KEY_TILE = 512
SCORE_ROWS = 128
FLT_MAX_KEY = 0x00800000
KEY_BITS = 32
FLT_MIN = 1.1754943508222875e-38
ATT_HEADS_PER_CHAIN = 2
ATT_LOOKAHEAD = 2


def _order_key_to_float(u):
    s = u ^ INT_MIN
    return pltpu.bitcast(s ^ ((s >> 31) & jnp.int32(0x7FFFFFFF)), F32)


def _attn_kernel(iq_ref, ik_ref, iw_ref, aq_ref, ak_ref, av_ref, *rest, top_k, n_cast):
    o_ref, sc_ref, sc16_ref = rest[n_cast], rest[-2], rest[-1]
    for w32_ref, w16_ref in zip(rest[:n_cast], rest[n_cast + 1:-2]):
        w16_ref[...] = w32_ref[...].astype(BF16)

    qb = pl.program_id(1)
    t0 = qb * Q_BLOCK
    n_tiles = (t0 + Q_BLOCK + KEY_TILE - 1) // KEY_TILE

    w_t = iw_ref[...].astype(F32).T
    w_rows = [w_t[h:h + 1, :] for h in range(H_IDX)]

    lane = lax.broadcasted_iota(jnp.int32, (Q_BLOCK, LANES), 1)
    low = lane < D_IDX
    q_pairs = []
    for c in range(IDX_Q_COLS // LANES):
        qc = iq_ref[:, c * LANES:(c + 1) * LANES]
        zero = jnp.zeros_like(qc)
        q_pairs.append(jnp.concatenate([jnp.where(low, qc, zero), jnp.where(low, zero, qc)], axis=0))

    q_pos = t0 + lax.broadcasted_iota(jnp.int32, (SCORE_ROWS, Q_BLOCK), 1)
    key_off = lax.broadcasted_iota(jnp.int32, (SCORE_ROWS, Q_BLOCK), 0)

    def score_tile(kt, carry):
        for s in range(KEY_TILE // SCORE_ROWS):
            base = pl.multiple_of(kt * KEY_TILE + s * SCORE_ROWS, SCORE_ROWS)
            kk = ik_ref[pl.ds(base, SCORE_ROWS), :]
            acc = jnp.zeros((SCORE_ROWS, Q_BLOCK), F32)
            for c, qp in enumerate(q_pairs):
                z = _dot_nt(kk, qp)
                acc = acc + jnp.maximum(z[:, :Q_BLOCK], 0.0) * w_rows[2 * c]
                acc = acc + jnp.maximum(z[:, Q_BLOCK:], 0.0) * w_rows[2 * c + 1]
            causal = base + key_off <= q_pos
            score = jnp.where(causal, acc, -jnp.inf)
            sc_ref[pl.ds(base, SCORE_ROWS), :] = score
            sc16_ref[pl.ds(base, SCORE_ROWS), :] = score.astype(BF16)
        return carry

    lax.fori_loop(0, n_tiles, score_tile, 0)

    def count_tiles(ref, indicator):
        rows = SUBLANES * 4 // ref.dtype.itemsize

        def count_tile(kt, cnt):
            base = pl.multiple_of(kt * KEY_TILE, KEY_TILE)
            part = indicator(ref[pl.ds(base, KEY_TILE), :], base)
            part = part.reshape(KEY_TILE // rows, rows, Q_BLOCK)
            while part.shape[0] > 1:
                half = part.shape[0] // 2
                part = part[:half] + part[half:]
            return cnt + part[0].astype(F32)

        cnt = lax.fori_loop(0, n_tiles, count_tile, jnp.zeros((rows, Q_BLOCK), F32))
        return jnp.sum(cnt, axis=0, keepdims=True)

    def count_ge(ref, cand):
        one, zero = jnp.ones((), ref.dtype), jnp.zeros((), ref.dtype)
        return count_tiles(ref, lambda tile, base: jnp.where(tile >= cand, one, zero))

    def search_bits(ref, tau_u, n_at_tau, first_bit, n_bits):
        def search_bit(it, carry):
            tau_u, n_at_tau = carry
            cand_u = tau_u | lax.shift_left(jnp.int32(1), first_bit - it)
            n_cand = count_ge(ref, _order_key_to_float(cand_u).astype(ref.dtype))
            accept = n_cand >= top_k
            return jnp.where(accept, cand_u, tau_u), jnp.where(accept, n_cand, n_at_tau)
        return lax.fori_loop(0, n_bits, search_bit, (tau_u, n_at_tau))

    searched = t0 + Q_BLOCK > top_k
    half_bits = jnp.where(searched, KEY_BITS // 2, 0)
    no_count = jnp.full((1, Q_BLOCK), jnp.inf, F32)
    rounded, _ = search_bits(sc16_ref, jnp.zeros((1, Q_BLOCK), jnp.int32), no_count,
                             KEY_BITS - 1, half_bits)
    n_rounded = count_ge(sc_ref, _order_key_to_float(rounded))
    fits = n_rounded >= top_k
    prefix = jnp.where(fits, rounded, rounded - (1 << KEY_BITS // 2))
    tau_u, n_ge = search_bits(sc_ref, prefix, jnp.where(fits, n_rounded, jnp.inf),
                              KEY_BITS // 2 - 1, half_bits)
    tau_u = jnp.where(searched, tau_u, FLT_MAX_KEY)
    tau = _order_key_to_float(tau_u)

    surplus = jnp.max(n_ge) > top_k

    @pl.when(jnp.logical_and(searched, surplus))
    def _():
        key_row = lax.broadcasted_iota(jnp.int32, (KEY_TILE, Q_BLOCK), 0)
        n_ge = count_ge(sc_ref, tau)

        def tied_before(cutoff):
            return lambda tile, base: jnp.where(
                tile == tau, jnp.where(base + key_row < cutoff, 1.0, 0.0), 0.0)

        n_eq = count_tiles(sc_ref, tied_before(jnp.int32(sc_ref.shape[0])))
        keep = top_k - (n_ge - n_eq)
        n_cut_bits = sc_ref.shape[0].bit_length()

        def cutoff_bit(it, cutoff):
            cand = cutoff | lax.shift_left(jnp.int32(1), n_cut_bits - 1 - it)
            return jnp.where(count_tiles(sc_ref, tied_before(cand)) <= keep, cand, cutoff)

        cutoff = lax.fori_loop(0, n_cut_bits, cutoff_bit, jnp.zeros((1, Q_BLOCK), jnp.int32))
        below = jnp.where(tau == 0.0, -FLT_MIN, _order_key_to_float(tau_u - 1))

        def demote(kt, carry):
            base = pl.multiple_of(kt * KEY_TILE, KEY_TILE)
            tile = sc_ref[pl.ds(base, KEY_TILE), :]
            demoted = jnp.where(base + key_row < cutoff, tile, below)
            sc_ref[pl.ds(base, KEY_TILE), :] = jnp.where(tile == tau, demoted, tile)
            return carry

        lax.fori_loop(0, n_tiles, demote, 0)

    n_chains = H_ATT // ATT_HEADS_PER_CHAIN
    width = ATT_HEADS_PER_CHAIN * Q_BLOCK
    q_chains = []
    for c in range(n_chains):
        heads = range(c * ATT_HEADS_PER_CHAIN, (c + 1) * ATT_HEADS_PER_CHAIN)
        q_chains.append(jnp.concatenate([aq_ref[:, h * D_HEAD:(h + 1) * D_HEAD] for h in heads], axis=0))

    def attend_tile(kt, carry):
        base = pl.multiple_of(kt * KEY_TILE, KEY_TILE)
        bias = jnp.where(sc_ref[pl.ds(base, KEY_TILE), :] >= tau, 0.0, NEG_BIG)
        bias = jnp.concatenate([bias] * ATT_HEADS_PER_CHAIN, axis=1)
        kv_of = lambda c: c * ATT_HEADS_PER_CHAIN // GROUP
        k_tiles = [ak_ref[pl.ds(base, KEY_TILE), g * D_HEAD:(g + 1) * D_HEAD] for g in range(H_KV)]
        v_tiles = [av_ref[pl.ds(base, KEY_TILE), g * D_HEAD:(g + 1) * D_HEAD] for g in range(H_KV)]
        score = lambda c: _dot_nt(k_tiles[kv_of(c)], q_chains[c]) + bias
        logits = [score(c) for c in range(min(ATT_LOOKAHEAD, n_chains))]
        new = []
        for c in range(n_chains):
            m_old, l_old, acc_old = carry[c]
            m_new = jnp.maximum(m_old, jnp.max(logits[c], axis=0, keepdims=True))
            alpha = jnp.exp2(m_old - m_new)
            p = jnp.exp2(logits[c] - m_new)
            l_new = alpha * l_old + jnp.sum(p, axis=0, keepdims=True)
            if c + ATT_LOOKAHEAD < n_chains:
                logits.append(score(c + ATT_LOOKAHEAD))
            acc_new = alpha * acc_old + _dot_tn(v_tiles[kv_of(c)], p.astype(BF16))
            new.append((m_new, l_new, acc_new))
        return tuple(new)

    init = tuple((jnp.full((1, width), NEG_BIG, F32), jnp.zeros((1, width), F32),
                  jnp.zeros((D_HEAD, width), F32)) for _ in range(n_chains))
    final = lax.fori_loop(0, n_tiles, attend_tile, init)
    for c in range(n_chains):
        _, l_fin, acc_fin = final[c]
        out_t = acc_fin / l_fin
        for j in range(ATT_HEADS_PER_CHAIN):
            hcol = (c * ATT_HEADS_PER_CHAIN + j) * D_HEAD
            o_ref[:, hcol:hcol + D_HEAD] = out_t[:, j * Q_BLOCK:(j + 1) * Q_BLOCK].T.astype(o_ref.dtype)


def _sparse_attention(idx_out, rope_out, plain_out, batch, seq, weights_to_cast):
    n_tok = batch * seq
    nb = seq // Q_BLOCK
    n_steps = batch * nb
    top_k = min(MAX_TOPK, seq // 4)
    kv_w = H_KV * D_HEAD
    step = lambda b, q: (b * nb + q, 0)
    slabs = [pl.BlockSpec((w.shape[0] // n_steps, w.shape[1]), step) for w in weights_to_cast]
    outs = pl.pallas_call(
        functools.partial(_attn_kernel, top_k=top_k, n_cast=len(weights_to_cast)),
        grid=(batch, nb),
        in_specs=[
            pl.BlockSpec((Q_BLOCK, IDX_Q_COLS), step),
            pl.BlockSpec((seq, LANES), lambda b, q: (b, IDX_Q_COLS // LANES)),
            pl.BlockSpec((Q_BLOCK, LANES), lambda b, q: (b * nb + q, IDX_Q_COLS // LANES + 1)),
            pl.BlockSpec((Q_BLOCK, ATT_W), lambda b, q: (b * nb + q, 2 * RET_W // ATT_W)),
            pl.BlockSpec((seq, kv_w), lambda b, q: (b, (2 * RET_W + ATT_W) // kv_w)),
            pl.BlockSpec((seq, kv_w), lambda b, q: (b, 2 * RET_W // kv_w)),
        ] + slabs,
        out_specs=[pl.BlockSpec((Q_BLOCK, ATT_W), step)] + slabs,
        out_shape=[jax.ShapeDtypeStruct((n_tok, ATT_W), BF16)]
                  + [jax.ShapeDtypeStruct(w.shape, BF16) for w in weights_to_cast],
        scratch_shapes=[pltpu.VMEM((seq, Q_BLOCK), F32), pltpu.VMEM((seq, Q_BLOCK), BF16)],
        compiler_params=_params(("parallel", "arbitrary")),
        name="sparse_attention",
    )(idx_out, idx_out, idx_out, rope_out, rope_out, plain_out, *weights_to_cast)
    return outs[0], outs[1:]


def _out_proj_kernel(x_ref, ro_ref, ao_ref, wr_ref, wa_ref, o_ref):
    o_ref[...] = x_ref[...] + _dot(ro_ref[...], wr_ref[...]) + _dot(ao_ref[...], wa_ref[...])


def _out_proj(x2d, ro, ao, w_out, tm=512):
    n_tok = x2d.shape[0]
    tile = pl.BlockSpec((tm, D_MODEL), lambda i: (i, 0))
    half = pl.BlockSpec((tm, RET_W), lambda i: (i, 0))
    w_ret = pl.BlockSpec((RET_W, D_MODEL), lambda i: (0, 0))
    w_att = pl.BlockSpec((ATT_W, D_MODEL), lambda i: (RET_W // ATT_W, 0))
    return pl.pallas_call(
        _out_proj_kernel,
        grid=(n_tok // tm,),
        in_specs=[tile, half, half, w_ret, w_att],
        out_specs=tile,
        out_shape=jax.ShapeDtypeStruct((n_tok, D_MODEL), F32),
        compiler_params=_params(("parallel",)),
        name="out_proj",
    )(x2d, ro, ao, w_out, w_out)


def _layer(x2d, tables, batch, seq, ffn1_norm, ffn1_w_gate, ffn1_w_up, ffn1_w_down, mix_norm,
           w_in_groups, ret_norm, w_out, ffn2_norm, ffn2_w_gate, ffn2_w_up, ffn2_w_down, final_norm, last):
    cos_a, sin_a, cos_b, sin_b = tables
    w_rope, w_plain, w_idx = w_in_groups
    row = lambda g: g.reshape(1, -1).astype(F32)

    x1, h = _ffn(x2d, row(ffn1_norm), ffn1_w_gate, ffn1_w_up, ffn1_w_down,
                 row(mix_norm), emit_residual=True, tf=FFN_TF_F32)
    rope_out = _proj(h, w_rope, cos_a, sin_a, mode="rope128", tm=512, out_dtype=BF16)
    plain_out = _proj(h, w_plain, cos_a, sin_a, mode="plain", tm=512, out_dtype=BF16)
    idx_out = _proj(h, w_idx, cos_b, sin_b, mode="idx", tm=1024, out_dtype=BF16)
    ro = _retention(rope_out, plain_out, row(ret_norm), batch, seq)
    ao, (w_out16, wg16, wu16, wd16) = _sparse_attention(
        idx_out, rope_out, plain_out, batch, seq, [w_out, ffn2_w_gate, ffn2_w_up, ffn2_w_down])
    x2 = _out_proj(x1, ro, ao, w_out16)
    out = _ffn(x2, row(ffn2_norm), wg16, wu16, wd16, row(final_norm), emit_residual=not last,
               tf=FFN_TF_BF16)
    return out if last else out[0]


def kernel(x, positions, ffn1_norm, ffn1_w_gate, ffn1_w_up, ffn1_w_down, mix_norm, w_in, ret_norm,
           w_out, ffn2_norm, ffn2_w_gate, ffn2_w_up, ffn2_w_down, final_norm):
    batch, seq, _ = x.shape
    depth = w_in.shape[0]
    tables, w_in_groups = _prepare(positions, jnp.swapaxes(w_in, 1, 2))
    x2d = x.reshape(batch * seq, D_MODEL)
    for l in range(depth):
        last = l == depth - 1
        groups_l = [w[l] for w in w_in_groups]
        x2d = _layer(x2d, tables, batch, seq, ffn1_norm[l], ffn1_w_gate[l], ffn1_w_up[l], ffn1_w_down[l],
                     mix_norm[l], groups_l, ret_norm[l], w_out[l], ffn2_norm[l], ffn2_w_gate[l],
                     ffn2_w_up[l], ffn2_w_down[l], final_norm, last)
    return x2d.reshape(batch, seq, D_MODEL)
```

```python
import functools

import jax
import jax.numpy as jnp
from jax import lax
from jax.experimental import pallas as pl
from jax.experimental.pallas import tpu as pltpu

D_MODEL = 2048
H_RET = 8
DK_RET = 128
DV_RET = 128
RET_CHUNK = 256
H_ATT = 8
H_KV = 2
D_HEAD = 128
H_IDX = 16
D_IDX = 64
MAX_TOPK = 256
Q_BLOCK = 256
D_FF = 5632
ROPE_THETA = 10000.0
NORM_EPS = 1e-6

RET_W = H_RET * DV_RET
ATT_W = H_ATT * D_HEAD
GROUP = H_ATT // H_KV

LANES = 128
SUBLANES = 8
VMEM_LIMIT = 60 * 1024 * 1024

ROPE_COLS = 2 * H_RET * DK_RET + ATT_W + H_KV * D_HEAD
PLAIN_COLS = 2 * RET_W + H_KV * D_HEAD
IDX_Q_COLS = H_IDX * D_IDX
IDX_COLS = IDX_Q_COLS + 2 * LANES
AQ_CHUNK0 = 2 * H_RET * DK_RET // LANES

INT_MIN = -2 ** 31
NEG_BIG = -1e30
LOG2_E = 1.4426950408889634

F32 = jnp.float32
BF16 = jnp.bfloat16


def _dot(a, b):
    return jnp.dot(a, b, preferred_element_type=F32)


def _dot_nt(a, b):
    return lax.dot_general(a, b, (((1,), (1,)), ((), ())), preferred_element_type=F32)


def _dot_tn(a, b):
    return lax.dot_general(a, b, (((0,), (0,)), ((), ())), preferred_element_type=F32)


def _rmsnorm(xf, g):
    ms = jnp.mean(xf * xf, axis=-1, keepdims=True)
    return xf * lax.rsqrt(ms + NORM_EPS) * g


def _params(sem):
    return pltpu.CompilerParams(dimension_semantics=sem, vmem_limit_bytes=VMEM_LIMIT)


_IN_SIZES = (H_RET * DK_RET, H_RET * DK_RET, RET_W, RET_W, ATT_W, H_KV * D_HEAD, H_KV * D_HEAD,
             H_IDX * D_IDX, D_IDX, H_IDX)
W_IN_SPLITS = tuple(sum(_IN_SIZES[:n]) for n in range(len(_IN_SIZES) + 1))


def _prep_kernel(pos_ref, inv_ref, sgn_a_ref, sgn_b_ref, w_in_ref,
                 cos_a_ref, sin_a_ref, cos_b_ref, sin_b_ref, w_rope_ref, w_plain_ref, w_idx_ref):
    ang = pos_ref[...].astype(F32) * inv_ref[...]
    lane = lax.broadcasted_iota(jnp.int32, ang.shape, 1)
    half, quarter = D_HEAD // 2, D_IDX // 2

    def table_a(t):
        return jnp.where(lane < half, t, pltpu.roll(t, half, axis=1))

    def table_b(t):
        upper = jnp.where(lane < half + quarter, t, pltpu.roll(t, quarter, axis=1))
        return jnp.where(lane >= half, upper, pltpu.roll(upper, half, axis=1))

    cos, sin = jnp.cos(ang), jnp.sin(ang)
    cos_a_ref[...] = table_a(cos)
    sin_a_ref[...] = table_a(sin) * sgn_a_ref[...]
    cos_b_ref[...] = table_b(cos)
    sin_b_ref[...] = table_b(sin) * sgn_b_ref[...]

    rq, rk, rv, rg, aq, ak, av, iq, ik, iw, end = W_IN_SPLITS
    for layer in range(w_in_ref.shape[0]):
        feats = lambda lo, hi: w_in_ref[layer, lo:hi, :].astype(BF16)
        w_rope_ref[layer, :rv - rq, :] = feats(rq, rv)
        w_rope_ref[layer, rv - rq:, :] = feats(aq, av)
        w_plain_ref[layer, :aq - rv, :] = feats(rv, aq)
        w_plain_ref[layer, aq - rv:, :] = feats(av, iq)
        w_idx_ref[layer, :ik - iq, :] = feats(iq, ik)
        key = feats(ik, iw)
        w_idx_ref[layer, ik - iq:ik - iq + D_IDX, :] = key
        w_idx_ref[layer, ik - iq + D_IDX:ik - iq + LANES, :] = key
        w_idx_ref[layer, ik - iq + LANES:ik - iq + LANES + H_IDX, :] = feats(iw, end)
        w_idx_ref[layer, ik - iq + LANES + H_IDX:, :] = jnp.zeros(
            (LANES - H_IDX, w_idx_ref.shape[2]), BF16)


def _prepare(positions, w_in_t):
    n_tok = positions.size
    tm = 1024
    n_steps = n_tok // tm
    depth, _, d_model = w_in_t.shape
    w_cols = d_model // n_steps
    lane = jnp.arange(LANES)

    def inv_freq(d):
        return ROPE_THETA ** (-jnp.arange(0, d, 2, dtype=F32) / d)

    unused = jnp.zeros((LANES - D_HEAD // 2 - D_IDX // 2,), F32)
    inv = jnp.concatenate([inv_freq(D_HEAD), inv_freq(D_IDX), unused])[None, :]

    def sign(d):
        return jnp.where(lane % d < d // 2, -1.0, 1.0).astype(F32)[None, :]

    row = pl.BlockSpec((1, LANES), lambda i: (0, 0))
    tab = pl.BlockSpec((tm, LANES), lambda i: (i, 0))
    out = jax.ShapeDtypeStruct((n_tok, LANES), F32)
    slab = lambda n_feats: pl.BlockSpec((depth, n_feats, w_cols), lambda i: (0, 0, i))
    w_out = lambda n_feats: jax.ShapeDtypeStruct((depth, n_feats, d_model), BF16)
    outs = pl.pallas_call(
        _prep_kernel,
        grid=(n_steps,),
        in_specs=[pl.BlockSpec((tm, 1), lambda i: (i, 0)), row, row, row, slab(w_in_t.shape[1])],
        out_specs=[tab, tab, tab, tab, slab(ROPE_COLS), slab(PLAIN_COLS), slab(IDX_COLS)],
        out_shape=[out, out, out, out, w_out(ROPE_COLS), w_out(PLAIN_COLS), w_out(IDX_COLS)],
        compiler_params=_params(("parallel",)),
        name="prepare",
    )(positions.reshape(n_tok, 1), inv, sign(D_HEAD), sign(D_IDX), w_in_t)
    return outs[:4], outs[4:]


FFN_ROW_CHUNK = 128


def _ffn_kernel(x_ref, g_ref, wg_ref, wu_ref, wd_ref, g2_ref, *refs, emit_residual):
    if emit_residual:
        res_ref, normed_ref, xn_ref = refs
        acc_ref = res_ref
    else:
        normed_ref, xn_ref = refs
        acc_ref = normed_ref
    f = pl.program_id(1)
    n_row_chunks = x_ref.shape[0] // FFN_ROW_CHUNK

    def rows_of(r):
        return pl.ds(pl.multiple_of(r * FFN_ROW_CHUNK, FFN_ROW_CHUNK), FFN_ROW_CHUNK)

    @pl.when(f == 0)
    def _():
        def prologue(r, carry):
            xf = x_ref[rows_of(r), :]
            xn_ref[rows_of(r), :] = _rmsnorm(xf, g_ref[...]).astype(BF16)
            acc_ref[rows_of(r), :] = 2.0 * xf
            return carry
        lax.fori_loop(0, n_row_chunks, prologue, 0)

    xn = xn_ref[...]
    a = _dot(xn, wg_ref[...].astype(BF16))
    b = _dot(xn, wu_ref[...].astype(BF16))
    hidden = (a * jax.nn.sigmoid(a) * b).astype(BF16)
    acc_ref[...] += _dot(hidden, wd_ref[...].astype(BF16))

    @pl.when(f == pl.num_programs(1) - 1)
    def _():
        def epilogue(r, carry):
            y = 0.5 * acc_ref[rows_of(r), :]
            if emit_residual:
                res_ref[rows_of(r), :] = y
            normed_ref[rows_of(r), :] = _rmsnorm(y, g2_ref[...]).astype(normed_ref.dtype)
            return carry
        lax.fori_loop(0, n_row_chunks, epilogue, 0)


FFN_TM = 1024
FFN_TF_F32 = 256
FFN_TF_BF16 = 512


def _ffn(x2d, g, wg, wu, wd, g2, *, emit_residual, tf, tm=FFN_TM):
    n_tok = x2d.shape[0]
    tile_map = lambda i, f: (i, 0)
    x_tile = pl.BlockSpec((tm, D_MODEL), tile_map)
    out_tile = pl.BlockSpec((tm, D_MODEL), tile_map)
    normed_tile = out_tile
    row = pl.BlockSpec((1, D_MODEL), lambda i, f: (0, 0))
    if emit_residual:
        out_specs = [out_tile, normed_tile]
        out_shape = [jax.ShapeDtypeStruct((n_tok, D_MODEL), F32),
                     jax.ShapeDtypeStruct((n_tok, D_MODEL), BF16)]
    else:
        out_specs = out_tile
        out_shape = jax.ShapeDtypeStruct((n_tok, D_MODEL), F32)
    return pl.pallas_call(
        functools.partial(_ffn_kernel, emit_residual=emit_residual),
        grid=(n_tok // tm, D_FF // tf),
        in_specs=[x_tile, row,
                  pl.BlockSpec((D_MODEL, tf), lambda i, f: (0, f)),
                  pl.BlockSpec((D_MODEL, tf), lambda i, f: (0, f)),
                  pl.BlockSpec((tf, D_MODEL), lambda i, f: (f, 0)),
                  row],
        out_specs=out_specs,
        out_shape=out_shape,
        scratch_shapes=[pltpu.VMEM((tm, D_MODEL), BF16)],
        compiler_params=_params(("parallel", "arbitrary")),
        name="ffn_residual" if emit_residual else "ffn_final",
    )(x2d, g, wg, wu, wd, g2)


def _rope128(y, cos, sin):
    return y * cos + pltpu.roll(y, D_HEAD // 2, axis=1) * sin


def _rope64(y, cos, sin):
    lane = lax.broadcasted_iota(jnp.int32, y.shape, 1)
    first_half = (lane & (D_IDX - 1)) < D_IDX // 2
    partner = jnp.where(first_half,
                        pltpu.roll(y, LANES - D_IDX // 2, axis=1),
                        pltpu.roll(y, D_IDX // 2, axis=1))
    return y * cos + partner * sin


def _proj_kernel(h_ref, w_ref, cos_ref, sin_ref, o_ref, *, mode):
    y = _dot_nt(h_ref[...], w_ref[...])
    n_chunks = y.shape[1] // LANES
    if mode == "plain":
        o_ref[...] = y.astype(o_ref.dtype)
        return
    cos = cos_ref[...]
    sin = sin_ref[...]
    for c in range(n_chunks):
        yc = y[:, c * LANES:(c + 1) * LANES]
        if mode == "rope128":
            yc = _rope128(yc, cos, sin)
            if AQ_CHUNK0 <= c < AQ_CHUNK0 + H_ATT:
                yc = yc * (LOG2_E * D_HEAD ** -0.5)
        elif c < IDX_Q_COLS // LANES:
            yc = _rope64(yc, cos, sin) * (D_IDX ** -0.5)
        elif c == IDX_Q_COLS // LANES:
            yc = _rope64(yc, cos, sin)
        else:
            yc = yc * (H_IDX ** -0.5)
        o_ref[:, c * LANES:(c + 1) * LANES] = yc.astype(o_ref.dtype)


def _proj(h, w, cos, sin, *, mode, tm, out_dtype):
    n_tok, n_cols = h.shape[0], w.shape[0]
    tab = pl.BlockSpec((tm, LANES), lambda i: (i, 0))
    return pl.pallas_call(
        functools.partial(_proj_kernel, mode=mode),
        grid=(n_tok // tm,),
        in_specs=[pl.BlockSpec((tm, D_MODEL), lambda i: (i, 0)),
                  pl.BlockSpec((n_cols, D_MODEL), lambda i: (0, 0), pipeline_mode=pl.Buffered(1)),
                  tab, tab],
        out_specs=pl.BlockSpec((tm, n_cols), lambda i: (i, 0)),
        out_shape=jax.ShapeDtypeStruct((n_tok, n_cols), out_dtype),
        compiler_params=_params(("parallel",)),
        name="proj_" + mode,
    )(h, w, cos, sin)


RET_UNROLL = 8
RET_HEADS_PER_STEP = 2


def _retention_kernel(lg_ref, q_ref, k_ref, v_ref, g_ref, rn_ref, o_ref):
    C = RET_CHUNK
    n_chunks = q_ref.shape[0] // C
    diff = (lax.broadcasted_iota(jnp.int32, (C, C), 0)
            - lax.broadcasted_iota(jnp.int32, (C, C), 1)).astype(F32)
    row = lax.broadcasted_iota(jnp.int32, (C, LANES), 0).astype(F32)
    scale = DK_RET ** -0.5
    heads = []
    for j in range(RET_HEADS_PER_STEP):
        lg = lg_ref[j]
        lg_wide = jnp.concatenate([lg] * (C // LANES), axis=1)
        heads.append(dict(
            lanes=slice(j * LANES, (j + 1) * LANES),
            decay=jnp.where(diff >= 0, jnp.exp(jnp.maximum(diff, 0.0) * lg_wide), 0.0) * scale,
            k_dec=jnp.exp((C - 1 - row) * lg) * scale,
            q_dec=jnp.exp((row + 1) * lg),
            g_chunk=jnp.exp(C * lg)))

    def body(n, states):
        sl = pl.ds(pl.multiple_of(n * C, C), C)
        new_states = []
        for hd, state in zip(heads, states):
            qc = q_ref[sl, hd["lanes"]]
            kc = k_ref[sl, hd["lanes"]]
            vc = v_ref[sl, hd["lanes"]]
            s = _dot_nt(qc, kc) * hd["decay"]
            intra = _dot(s.astype(BF16), vc)
            cross = _dot(qc, state.astype(BF16)) * hd["q_dec"]
            kv = _dot_tn((kc.astype(F32) * hd["k_dec"]).astype(BF16), vc)
            o = intra + cross
            mu = jnp.mean(o, axis=-1, keepdims=True)
            d = o - mu
            var = jnp.mean(d * d, axis=-1, keepdims=True)
            y = d * lax.rsqrt(var + NORM_EPS) * rn_ref[:, hd["lanes"]]
            gate = g_ref[sl, hd["lanes"]].astype(F32)
            o_ref[sl, hd["lanes"]] = (y * (gate * jax.nn.sigmoid(gate))).astype(o_ref.dtype)
            new_states.append(state * hd["g_chunk"] + kv)
        return tuple(new_states)

    init = tuple(jnp.zeros((DK_RET, DV_RET), F32) for _ in heads)
    lax.fori_loop(0, n_chunks, body, init, unroll=RET_UNROLL)


def _retention(rope_out, plain_out, ret_norm, batch, seq):
    n_tok = batch * seq
    lg = jnp.log1p(-jnp.exp2(-5.0 - jnp.arange(H_RET, dtype=F32)))
    lg = jnp.broadcast_to(lg[:, None, None], (H_RET, 1, LANES))
    width = RET_HEADS_PER_STEP * LANES
    n_groups = H_RET // RET_HEADS_PER_STEP
    heads = lambda off: pl.BlockSpec((seq, width), lambda b, h: (b, off + h))
    return pl.pallas_call(
        _retention_kernel,
        grid=(batch, n_groups),
        in_specs=[pl.BlockSpec((RET_HEADS_PER_STEP, 1, LANES), lambda b, h: (h, 0, 0)),
                  heads(0), heads(n_groups),
                  heads(0), heads(n_groups),
                  pl.BlockSpec((1, width), lambda b, h: (0, h))],
        out_specs=heads(0),
        out_shape=jax.ShapeDtypeStruct((n_tok, RET_W), BF16),
        compiler_params=_params(("parallel", "parallel")),
        name="retention",
    )(lg, rope_out, rope_out, plain_out, plain_out, ret_norm)


KEY_TILE = 512
SCORE_ROWS = 128
FLT_MAX_KEY = 0x00800000
KEY_BITS = 32
FLT_MIN = 1.1754943508222875e-38
ATT_HEADS_PER_CHAIN = 2
ATT_LOOKAHEAD = 2


def _order_key_to_float(u):
    s = u ^ INT_MIN
    return pltpu.bitcast(s ^ ((s >> 31) & jnp.int32(0x7FFFFFFF)), F32)


def _attn_kernel(iq_ref, ik_ref, iw_ref, aq_ref, ak_ref, av_ref, *rest, top_k, n_cast):
    o_ref, sc_ref, sc16_ref = rest[n_cast], rest[-2], rest[-1]
    for w32_ref, w16_ref in zip(rest[:n_cast], rest[n_cast + 1:-2]):
        w16_ref[...] = w32_ref[...].astype(BF16)

    qb = pl.program_id(1)
    t0 = qb * Q_BLOCK
    n_tiles = (t0 + Q_BLOCK + KEY_TILE - 1) // KEY_TILE

    w_t = iw_ref[...].astype(F32).T
    w_rows = [w_t[h:h + 1, :] for h in range(H_IDX)]

    lane = lax.broadcasted_iota(jnp.int32, (Q_BLOCK, LANES), 1)
    low = lane < D_IDX
    q_pairs = []
    for c in range(IDX_Q_COLS // LANES):
        qc = iq_ref[:, c * LANES:(c + 1) * LANES]
        zero = jnp.zeros_like(qc)
        q_pairs.append(jnp.concatenate([jnp.where(low, qc, zero), jnp.where(low, zero, qc)], axis=0))

    q_pos = t0 + lax.broadcasted_iota(jnp.int32, (SCORE_ROWS, Q_BLOCK), 1)
    key_off = lax.broadcasted_iota(jnp.int32, (SCORE_ROWS, Q_BLOCK), 0)

    def score_tile(kt, carry):
        for s in range(KEY_TILE // SCORE_ROWS):
            base = pl.multiple_of(kt * KEY_TILE + s * SCORE_ROWS, SCORE_ROWS)
            kk = ik_ref[pl.ds(base, SCORE_ROWS), :]
            acc = jnp.zeros((SCORE_ROWS, Q_BLOCK), F32)
            for c, qp in enumerate(q_pairs):
                z = _dot_nt(kk, qp)
                acc = acc + jnp.maximum(z[:, :Q_BLOCK], 0.0) * w_rows[2 * c]
                acc = acc + jnp.maximum(z[:, Q_BLOCK:], 0.0) * w_rows[2 * c + 1]
            causal = base + key_off <= q_pos
            score = jnp.where(causal, acc, -jnp.inf)
            sc_ref[pl.ds(base, SCORE_ROWS), :] = score
            sc16_ref[pl.ds(base, SCORE_ROWS), :] = score.astype(BF16)
        return carry

    lax.fori_loop(0, n_tiles, score_tile, 0)

    def count_tiles(ref, indicator):
        rows = SUBLANES * 4 // ref.dtype.itemsize

        def count_tile(kt, cnt):
            base = pl.multiple_of(kt * KEY_TILE, KEY_TILE)
            part = indicator(ref[pl.ds(base, KEY_TILE), :], base)
            part = part.reshape(KEY_TILE // rows, rows, Q_BLOCK)
            while part.shape[0] > 1:
                half = part.shape[0] // 2
                part = part[:half] + part[half:]
            return cnt + part[0].astype(F32)

        cnt = lax.fori_loop(0, n_tiles, count_tile, jnp.zeros((rows, Q_BLOCK), F32))
        return jnp.sum(cnt, axis=0, keepdims=True)

    def count_ge(ref, cand):
        one, zero = jnp.ones((), ref.dtype), jnp.zeros((), ref.dtype)
        return count_tiles(ref, lambda tile, base: jnp.where(tile >= cand, one, zero))

    def search_bits(ref, tau_u, n_at_tau, first_bit, n_bits):
        def search_bit(it, carry):
            tau_u, n_at_tau = carry
            cand_u = tau_u | lax.shift_left(jnp.int32(1), first_bit - it)
            n_cand = count_ge(ref, _order_key_to_float(cand_u).astype(ref.dtype))
            accept = n_cand >= top_k
            return jnp.where(accept, cand_u, tau_u), jnp.where(accept, n_cand, n_at_tau)
        return lax.fori_loop(0, n_bits, search_bit, (tau_u, n_at_tau))

    searched = t0 + Q_BLOCK > top_k
    half_bits = jnp.where(searched, KEY_BITS // 2, 0)
    no_count = jnp.full((1, Q_BLOCK), jnp.inf, F32)
    rounded, _ = search_bits(sc16_ref, jnp.zeros((1, Q_BLOCK), jnp.int32), no_count,
                             KEY_BITS - 1, half_bits)
    n_rounded = count_ge(sc_ref, _order_key_to_float(rounded))
    fits = n_rounded >= top_k
    prefix = jnp.where(fits, rounded, rounded - (1 << KEY_BITS // 2))
    tau_u, n_ge = search_bits(sc_ref, prefix, jnp.where(fits, n_rounded, jnp.inf),
                              KEY_BITS // 2 - 1, half_bits)
    tau_u = jnp.where(searched, tau_u, FLT_MAX_KEY)
    tau = _order_key_to_float(tau_u)

    surplus = jnp.max(n_ge) > top_k

    @pl.when(jnp.logical_and(searched, surplus))
    def _():
        key_row = lax.broadcasted_iota(jnp.int32, (KEY_TILE, Q_BLOCK), 0)
        n_ge = count_ge(sc_ref, tau)

        def tied_before(cutoff):
            return lambda tile, base: jnp.where(
                tile == tau, jnp.where(base + key_row < cutoff, 1.0, 0.0), 0.0)

        n_eq = count_tiles(sc_ref, tied_before(jnp.int32(sc_ref.shape[0])))
        keep = top_k - (n_ge - n_eq)
        n_cut_bits = sc_ref.shape[0].bit_length()

        def cutoff_bit(it, cutoff):
            cand = cutoff | lax.shift_left(jnp.int32(1), n_cut_bits - 1 - it)
            return jnp.where(count_tiles(sc_ref, tied_before(cand)) <= keep, cand, cutoff)

        cutoff = lax.fori_loop(0, n_cut_bits, cutoff_bit, jnp.zeros((1, Q_BLOCK), jnp.int32))
        below = jnp.where(tau == 0.0, -FLT_MIN, _order_key_to_float(tau_u - 1))

        def demote(kt, carry):
            base = pl.multiple_of(kt * KEY_TILE, KEY_TILE)
            tile = sc_ref[pl.ds(base, KEY_TILE), :]
            demoted = jnp.where(base + key_row < cutoff, tile, below)
            sc_ref[pl.ds(base, KEY_TILE), :] = jnp.where(tile == tau, demoted, tile)
            return carry

        lax.fori_loop(0, n_tiles, demote, 0)

    n_chains = H_ATT // ATT_HEADS_PER_CHAIN
    width = ATT_HEADS_PER_CHAIN * Q_BLOCK
    q_chains = []
    for c in range(n_chains):
        heads = range(c * ATT_HEADS_PER_CHAIN, (c + 1) * ATT_HEADS_PER_CHAIN)
        q_chains.append(jnp.concatenate([aq_ref[:, h * D_HEAD:(h + 1) * D_HEAD] for h in heads], axis=0))

    def attend_tile(kt, carry):
        base = pl.multiple_of(kt * KEY_TILE, KEY_TILE)
        bias = jnp.where(sc_ref[pl.ds(base, KEY_TILE), :] >= tau, 0.0, NEG_BIG)
        bias = jnp.concatenate([bias] * ATT_HEADS_PER_CHAIN, axis=1)
        kv_of = lambda c: c * ATT_HEADS_PER_CHAIN // GROUP
        k_tiles = [ak_ref[pl.ds(base, KEY_TILE), g * D_HEAD:(g + 1) * D_HEAD] for g in range(H_KV)]
        v_tiles = [av_ref[pl.ds(base, KEY_TILE), g * D_HEAD:(g + 1) * D_HEAD] for g in range(H_KV)]
        score = lambda c: _dot_nt(k_tiles[kv_of(c)], q_chains[c]) + bias
        logits = [score(c) for c in range(min(ATT_LOOKAHEAD, n_chains))]
        new = []
        for c in range(n_chains):
            m_old, l_old, acc_old = carry[c]
            m_new = jnp.maximum(m_old, jnp.max(logits[c], axis=0, keepdims=True))
            alpha = jnp.exp2(m_old - m_new)
            p = jnp.exp2(logits[c] - m_new)
            l_new = alpha * l_old + jnp.sum(p, axis=0, keepdims=True)
            if c + ATT_LOOKAHEAD < n_chains:
                logits.append(score(c + ATT_LOOKAHEAD))
            acc_new = alpha * acc_old + _dot_tn(v_tiles[kv_of(c)], p.astype(BF16))
            new.append((m_new, l_new, acc_new))
        return tuple(new)

    init = tuple((jnp.full((1, width), NEG_BIG, F32), jnp.zeros((1, width), F32),
                  jnp.zeros((D_HEAD, width), F32)) for _ in range(n_chains))
    final = lax.fori_loop(0, n_tiles, attend_tile, init)
    for c in range(n_chains):
        _, l_fin, acc_fin = final[c]
        out_t = acc_fin / l_fin
        for j in range(ATT_HEADS_PER_CHAIN):
            hcol = (c * ATT_HEADS_PER_CHAIN + j) * D_HEAD
            o_ref[:, hcol:hcol + D_HEAD] = out_t[:, j * Q_BLOCK:(j + 1) * Q_BLOCK].T.astype(o_ref.dtype)


def _sparse_attention(idx_out, rope_out, plain_out, batch, seq, weights_to_cast):
    n_tok = batch * seq
    nb = seq // Q_BLOCK
    n_steps = batch * nb
    top_k = min(MAX_TOPK, seq // 4)
    kv_w = H_KV * D_HEAD
    step = lambda b, q: (b * nb + q, 0)
    slabs = [pl.BlockSpec((w.shape[0] // n_steps, w.shape[1]), step) for w in weights_to_cast]
    outs = pl.pallas_call(
        functools.partial(_attn_kernel, top_k=top_k, n_cast=len(weights_to_cast)),
        grid=(batch, nb),
        in_specs=[
            pl.BlockSpec((Q_BLOCK, IDX_Q_COLS), step),
            pl.BlockSpec((seq, LANES), lambda b, q: (b, IDX_Q_COLS // LANES)),
            pl.BlockSpec((Q_BLOCK, LANES), lambda b, q: (b * nb + q, IDX_Q_COLS // LANES + 1)),
            pl.BlockSpec((Q_BLOCK, ATT_W), lambda b, q: (b * nb + q, 2 * RET_W // ATT_W)),
            pl.BlockSpec((seq, kv_w), lambda b, q: (b, (2 * RET_W + ATT_W) // kv_w)),
            pl.BlockSpec((seq, kv_w), lambda b, q: (b, 2 * RET_W // kv_w)),
        ] + slabs,
        out_specs=[pl.BlockSpec((Q_BLOCK, ATT_W), step)] + slabs,
        out_shape=[jax.ShapeDtypeStruct((n_tok, ATT_W), BF16)]
                  + [jax.ShapeDtypeStruct(w.shape, BF16) for w in weights_to_cast],
        scratch_shapes=[pltpu.VMEM((seq, Q_BLOCK), F32), pltpu.VMEM((seq, Q_BLOCK), BF16)],
        compiler_params=_params(("parallel", "arbitrary")),
        name="sparse_attention",
    )(idx_out, idx_out, idx_out, rope_out, rope_out, plain_out, *weights_to_cast)
    return outs[0], outs[1:]


def _out_proj_kernel(x_ref, ro_ref, ao_ref, wr_ref, wa_ref, o_ref):
    o_ref[...] = x_ref[...] + _dot(ro_ref[...], wr_ref[...]) + _dot(ao_ref[...], wa_ref[...])


def _out_proj(x2d, ro, ao, w_out, tm=512):
    n_tok = x2d.shape[0]
    tile = pl.BlockSpec((tm, D_MODEL), lambda i: (i, 0))
    half = pl.BlockSpec((tm, RET_W), lambda i: (i, 0))
    w_ret = pl.BlockSpec((RET_W, D_MODEL), lambda i: (0, 0))
    w_att = pl.BlockSpec((ATT_W, D_MODEL), lambda i: (RET_W // ATT_W, 0))
    return pl.pallas_call(
        _out_proj_kernel,
        grid=(n_tok // tm,),
        in_specs=[tile, half, half, w_ret, w_att],
        out_specs=tile,
        out_shape=jax.ShapeDtypeStruct((n_tok, D_MODEL), F32),
        compiler_params=_params(("parallel",)),
        name="out_proj",
    )(x2d, ro, ao, w_out, w_out)


def _layer(x2d, tables, batch, seq, ffn1_norm, ffn1_w_gate, ffn1_w_up, ffn1_w_down, mix_norm,
           w_in_groups, ret_norm, w_out, ffn2_norm, ffn2_w_gate, ffn2_w_up, ffn2_w_down, final_norm, last):
    cos_a, sin_a, cos_b, sin_b = tables
    w_rope, w_plain, w_idx = w_in_groups
    row = lambda g: g.reshape(1, -1).astype(F32)

    x1, h = _ffn(x2d, row(ffn1_norm), ffn1_w_gate, ffn1_w_up, ffn1_w_down,
                 row(mix_norm), emit_residual=True, tf=FFN_TF_F32)
    rope_out = _proj(h, w_rope, cos_a, sin_a, mode="rope128", tm=512, out_dtype=BF16)
    plain_out = _proj(h, w_plain, cos_a, sin_a, mode="plain", tm=512, out_dtype=BF16)
    idx_out = _proj(h, w_idx, cos_b, sin_b, mode="idx", tm=1024, out_dtype=BF16)
    ro = _retention(rope_out, plain_out, row(ret_norm), batch, seq)
    ao, (w_out16, wg16, wu16, wd16) = _sparse_attention(
        idx_out, rope_out, plain_out, batch, seq, [w_out, ffn2_w_gate, ffn2_w_up, ffn2_w_down])
    x2 = _out_proj(x1, ro, ao, w_out16)
    out = _ffn(x2, row(ffn2_norm), wg16, wu16, wd16, row(final_norm), emit_residual=not last,
               tf=FFN_TF_BF16)
    return out if last else out[0]


def kernel(x, positions, ffn1_norm, ffn1_w_gate, ffn1_w_up, ffn1_w_down, mix_norm, w_in, ret_norm,
           w_out, ffn2_norm, ffn2_w_gate, ffn2_w_up, ffn2_w_down, final_norm):
    batch, seq, _ = x.shape
    depth = w_in.shape[0]
    tables, w_in_groups = _prepare(positions, jnp.swapaxes(w_in, 1, 2))
    x2d = x.reshape(batch * seq, D_MODEL)
    for l in range(depth):
        last = l == depth - 1
        groups_l = [w[l] for w in w_in_groups]
        x2d = _layer(x2d, tables, batch, seq, ffn1_norm[l], ffn1_w_gate[l], ffn1_w_up[l], ffn1_w_down[l],
                     mix_norm[l], groups_l, ret_norm[l], w_out[l], ffn2_norm[l], ffn2_w_gate[l],
                     ffn2_w_up[l], ffn2_w_down[l], final_norm, last)
    return x2d.reshape(batch, seq, D_MODEL)
```

```python
import functools

import jax
import jax.numpy as jnp
from jax import lax
from jax.experimental import pallas as pl
from jax.experimental.pallas import tpu as pltpu

D_MODEL = 2048
H_RET = 8
DK_RET = 128
DV_RET = 128
RET_CHUNK = 256
H_ATT = 8
H_KV = 2
D_HEAD = 128
H_IDX = 16
D_IDX = 64
MAX_TOPK = 256
Q_BLOCK = 256
D_FF = 5632
ROPE_THETA = 10000.0
NORM_EPS = 1e-6

RET_W = H_RET * DV_RET
ATT_W = H_ATT * D_HEAD
GROUP = H_ATT // H_KV

LANES = 128
SUBLANES = 8
VMEM_LIMIT = 60 * 1024 * 1024

ROPE_COLS = 2 * H_RET * DK_RET + ATT_W + H_KV * D_HEAD
PLAIN_COLS = 2 * RET_W + H_KV * D_HEAD
IDX_Q_COLS = H_IDX * D_IDX
IDX_COLS = IDX_Q_COLS + 2 * LANES
AQ_CHUNK0 = 2 * H_RET * DK_RET // LANES

INT_MIN = -2 ** 31
NEG_BIG = -1e30
LOG2_E = 1.4426950408889634

F32 = jnp.float32
BF16 = jnp.bfloat16


def _dot(a, b):
    return jnp.dot(a, b, preferred_element_type=F32)


def _dot_nt(a, b):
    return lax.dot_general(a, b, (((1,), (1,)), ((), ())), preferred_element_type=F32)


def _dot_tn(a, b):
    return lax.dot_general(a, b, (((0,), (0,)), ((), ())), preferred_element_type=F32)


def _rmsnorm(xf, g):
    ms = jnp.mean(xf * xf, axis=-1, keepdims=True)
    return xf * lax.rsqrt(ms + NORM_EPS) * g


def _params(sem):
    return pltpu.CompilerParams(dimension_semantics=sem, vmem_limit_bytes=VMEM_LIMIT)


_IN_SIZES = (H_RET * DK_RET, H_RET * DK_RET, RET_W, RET_W, ATT_W, H_KV * D_HEAD, H_KV * D_HEAD,
             H_IDX * D_IDX, D_IDX, H_IDX)
W_IN_SPLITS = tuple(sum(_IN_SIZES[:n]) for n in range(len(_IN_SIZES) + 1))


def _prep_kernel(pos_ref, inv_ref, sgn_a_ref, sgn_b_ref, w_in_ref,
                 cos_a_ref, sin_a_ref, cos_b_ref, sin_b_ref, w_rope_ref, w_plain_ref, w_idx_ref):
    ang = pos_ref[...].astype(F32) * inv_ref[...]
    lane = lax.broadcasted_iota(jnp.int32, ang.shape, 1)
    half, quarter = D_HEAD // 2, D_IDX // 2

    def table_a(t):
        return jnp.where(lane < half, t, pltpu.roll(t, half, axis=1))

    def table_b(t):
        upper = jnp.where(lane < half + quarter, t, pltpu.roll(t, quarter, axis=1))
        return jnp.where(lane >= half, upper, pltpu.roll(upper, half, axis=1))

    cos, sin = jnp.cos(ang), jnp.sin(ang)
    cos_a_ref[...] = table_a(cos)
    sin_a_ref[...] = table_a(sin) * sgn_a_ref[...]
    cos_b_ref[...] = table_b(cos)
    sin_b_ref[...] = table_b(sin) * sgn_b_ref[...]

    rq, rk, rv, rg, aq, ak, av, iq, ik, iw, end = W_IN_SPLITS
    for layer in range(w_in_ref.shape[0]):
        feats = lambda lo, hi: w_in_ref[layer, lo:hi, :].astype(BF16)
        w_rope_ref[layer, :rv - rq, :] = feats(rq, rv)
        w_rope_ref[layer, rv - rq:, :] = feats(aq, av)
        w_plain_ref[layer, :aq - rv, :] = feats(rv, aq)
        w_plain_ref[layer, aq - rv:, :] = feats(av, iq)
        w_idx_ref[layer, :ik - iq, :] = feats(iq, ik)
        key = feats(ik, iw)
        w_idx_ref[layer, ik - iq:ik - iq + D_IDX, :] = key
        w_idx_ref[layer, ik - iq + D_IDX:ik - iq + LANES, :] = key
        w_idx_ref[layer, ik - iq + LANES:ik - iq + LANES + H_IDX, :] = feats(iw, end)
        w_idx_ref[layer, ik - iq + LANES + H_IDX:, :] = jnp.zeros(
            (LANES - H_IDX, w_idx_ref.shape[2]), BF16)


def _prepare(positions, w_in_t):
    n_tok = positions.size
    tm = 1024
    n_steps = n_tok // tm
    depth, _, d_model = w_in_t.shape
    w_cols = d_model // n_steps
    lane = jnp.arange(LANES)

    def inv_freq(d):
        return ROPE_THETA ** (-jnp.arange(0, d, 2, dtype=F32) / d)

    unused = jnp.zeros((LANES - D_HEAD // 2 - D_IDX // 2,), F32)
    inv = jnp.concatenate([inv_freq(D_HEAD), inv_freq(D_IDX), unused])[None, :]

    def sign(d):
        return jnp.where(lane % d < d // 2, -1.0, 1.0).astype(F32)[None, :]

    row = pl.BlockSpec((1, LANES), lambda i: (0, 0))
    tab = pl.BlockSpec((tm, LANES), lambda i: (i, 0))
    out = jax.ShapeDtypeStruct((n_tok, LANES), F32)
    slab = lambda n_feats: pl.BlockSpec((depth, n_feats, w_cols), lambda i: (0, 0, i))
    w_out = lambda n_feats: jax.ShapeDtypeStruct((depth, n_feats, d_model), BF16)
    outs = pl.pallas_call(
        _prep_kernel,
        grid=(n_steps,),
        in_specs=[pl.BlockSpec((tm, 1), lambda i: (i, 0)), row, row, row, slab(w_in_t.shape[1])],
        out_specs=[tab, tab, tab, tab, slab(ROPE_COLS), slab(PLAIN_COLS), slab(IDX_COLS)],
        out_shape=[out, out, out, out, w_out(ROPE_COLS), w_out(PLAIN_COLS), w_out(IDX_COLS)],
        compiler_params=_params(("parallel",)),
        name="prepare",
    )(positions.reshape(n_tok, 1), inv, sign(D_HEAD), sign(D_IDX), w_in_t)
    return outs[:4], outs[4:]


FFN_ROW_CHUNK = 128


def _ffn_kernel(x_ref, g_ref, wg_ref, wu_ref, wd_ref, g2_ref, *refs, emit_residual):
    if emit_residual:
        res_ref, normed_ref, xn_ref = refs
        acc_ref = res_ref
    else:
        normed_ref, xn_ref = refs
        acc_ref = normed_ref
    f = pl.program_id(1)
    n_row_chunks = x_ref.shape[0] // FFN_ROW_CHUNK

    def rows_of(r):
        return pl.ds(pl.multiple_of(r * FFN_ROW_CHUNK, FFN_ROW_CHUNK), FFN_ROW_CHUNK)

    @pl.when(f == 0)
    def _():
        def prologue(r, carry):
            xf = x_ref[rows_of(r), :]
            xn_ref[rows_of(r), :] = _rmsnorm(xf, g_ref[...]).astype(BF16)
            acc_ref[rows_of(r), :] = 2.0 * xf
            return carry
        lax.fori_loop(0, n_row_chunks, prologue, 0)

    xn = xn_ref[...]
    a = _dot(xn, wg_ref[...].astype(BF16))
    b = _dot(xn, wu_ref[...].astype(BF16))
    hidden = (a * jax.nn.sigmoid(a) * b).astype(BF16)
    acc_ref[...] += _dot(hidden, wd_ref[...].astype(BF16))

    @pl.when(f == pl.num_programs(1) - 1)
    def _():
        def epilogue(r, carry):
            y = 0.5 * acc_ref[rows_of(r), :]
            if emit_residual:
                res_ref[rows_of(r), :] = y
            normed_ref[rows_of(r), :] = _rmsnorm(y, g2_ref[...]).astype(normed_ref.dtype)
            return carry
        lax.fori_loop(0, n_row_chunks, epilogue, 0)


FFN_TM = 1024
FFN_TF_F32 = 256
FFN_TF_BF16 = 512


def _ffn(x2d, g, wg, wu, wd, g2, *, emit_residual, tf, tm=FFN_TM):
    n_tok = x2d.shape[0]
    tile_map = lambda i, f: (i, 0)
    x_tile = pl.BlockSpec((tm, D_MODEL), tile_map)
    out_tile = pl.BlockSpec((tm, D_MODEL), tile_map)
    normed_tile = out_tile
    row = pl.BlockSpec((1, D_MODEL), lambda i, f: (0, 0))
    if emit_residual:
        out_specs = [out_tile, normed_tile]
        out_shape = [jax.ShapeDtypeStruct((n_tok, D_MODEL), F32),
                     jax.ShapeDtypeStruct((n_tok, D_MODEL), BF16)]
    else:
        out_specs = out_tile
        out_shape = jax.ShapeDtypeStruct((n_tok, D_MODEL), F32)
    return pl.pallas_call(
        functools.partial(_ffn_kernel, emit_residual=emit_residual),
        grid=(n_tok // tm, D_FF // tf),
        in_specs=[x_tile, row,
                  pl.BlockSpec((D_MODEL, tf), lambda i, f: (0, f)),
                  pl.BlockSpec((D_MODEL, tf), lambda i, f: (0, f)),
                  pl.BlockSpec((tf, D_MODEL), lambda i, f: (f, 0)),
                  row],
        out_specs=out_specs,
        out_shape=out_shape,
        scratch_shapes=[pltpu.VMEM((tm, D_MODEL), BF16)],
        compiler_params=_params(("parallel", "arbitrary")),
        name="ffn_residual" if emit_residual else "ffn_final",
    )(x2d, g, wg, wu, wd, g2)


def _rope128(y, cos, sin):
    return y * cos + pltpu.roll(y, D_HEAD // 2, axis=1) * sin


def _rope64(y, cos, sin):
    lane = lax.broadcasted_iota(jnp.int32, y.shape, 1)
    first_half = (lane & (D_IDX - 1)) < D_IDX // 2
    partner = jnp.where(first_half,
                        pltpu.roll(y, LANES - D_IDX // 2, axis=1),
                        pltpu.roll(y, D_IDX // 2, axis=1))
    return y * cos + partner * sin


def _proj_kernel(h_ref, w_ref, cos_ref, sin_ref, o_ref, *, mode):
    y = _dot_nt(h_ref[...], w_ref[...])
    n_chunks = y.shape[1] // LANES
    if mode == "plain":
        o_ref[...] = y.astype(o_ref.dtype)
        return
    cos = cos_ref[...]
    sin = sin_ref[...]
    for c in range(n_chunks):
        yc = y[:, c * LANES:(c + 1) * LANES]
        if mode == "rope128":
            yc = _rope128(yc, cos, sin)
            if AQ_CHUNK0 <= c < AQ_CHUNK0 + H_ATT:
                yc = yc * (LOG2_E * D_HEAD ** -0.5)
        elif c < IDX_Q_COLS // LANES:
            yc = _rope64(yc, cos, sin) * (D_IDX ** -0.5)
        elif c == IDX_Q_COLS // LANES:
            yc = _rope64(yc, cos, sin)
        else:
            yc = yc * (H_IDX ** -0.5)
        o_ref[:, c * LANES:(c + 1) * LANES] = yc.astype(o_ref.dtype)


def _proj(h, w, cos, sin, *, mode, tm, out_dtype):
    n_tok, n_cols = h.shape[0], w.shape[0]
    tab = pl.BlockSpec((tm, LANES), lambda i: (i, 0))
    return pl.pallas_call(
        functools.partial(_proj_kernel, mode=mode),
        grid=(n_tok // tm,),
        in_specs=[pl.BlockSpec((tm, D_MODEL), lambda i: (i, 0)),
                  pl.BlockSpec((n_cols, D_MODEL), lambda i: (0, 0), pipeline_mode=pl.Buffered(1)),
                  tab, tab],
        out_specs=pl.BlockSpec((tm, n_cols), lambda i: (i, 0)),
        out_shape=jax.ShapeDtypeStruct((n_tok, n_cols), out_dtype),
        compiler_params=_params(("parallel",)),
        name="proj_" + mode,
    )(h, w, cos, sin)


RET_UNROLL = 8
RET_HEADS_PER_STEP = 2


def _retention_kernel(lg_ref, q_ref, k_ref, v_ref, g_ref, rn_ref, o_ref):
    C = RET_CHUNK
    n_chunks = q_ref.shape[0] // C
    diff = (lax.broadcasted_iota(jnp.int32, (C, C), 0)
            - lax.broadcasted_iota(jnp.int32, (C, C), 1)).astype(F32)
    row = lax.broadcasted_iota(jnp.int32, (C, LANES), 0).astype(F32)
    scale = DK_RET ** -0.5
    heads = []
    for j in range(RET_HEADS_PER_STEP):
        lg = lg_ref[j]
        lg_wide = jnp.concatenate([lg] * (C // LANES), axis=1)
        heads.append(dict(
            lanes=slice(j * LANES, (j + 1) * LANES),
            decay=jnp.where(diff >= 0, jnp.exp(jnp.maximum(diff, 0.0) * lg_wide), 0.0) * scale,
            k_dec=jnp.exp((C - 1 - row) * lg) * scale,
            q_dec=jnp.exp((row + 1) * lg),
            g_chunk=jnp.exp(C * lg)))

    def body(n, states):
        sl = pl.ds(pl.multiple_of(n * C, C), C)
        new_states = []
        for hd, state in zip(heads, states):
            qc = q_ref[sl, hd["lanes"]]
            kc = k_ref[sl, hd["lanes"]]
            vc = v_ref[sl, hd["lanes"]]
            s = _dot_nt(qc, kc) * hd["decay"]
            intra = _dot(s.astype(BF16), vc)
            cross = _dot(qc, state.astype(BF16)) * hd["q_dec"]
            kv = _dot_tn((kc.astype(F32) * hd["k_dec"]).astype(BF16), vc)
            o = intra + cross
            mu = jnp.mean(o, axis=-1, keepdims=True)
            d = o - mu
            var = jnp.mean(d * d, axis=-1, keepdims=True)
            y = d * lax.rsqrt(var + NORM_EPS) * rn_ref[:, hd["lanes"]]
            gate = g_ref[sl, hd["lanes"]].astype(F32)
            o_ref[sl, hd["lanes"]] = (y * (gate * jax.nn.sigmoid(gate))).astype(o_ref.dtype)
            new_states.append(state * hd["g_chunk"] + kv)
        return tuple(new_states)

    init = tuple(jnp.zeros((DK_RET, DV_RET), F32) for _ in heads)
    lax.fori_loop(0, n_chunks, body, init, unroll=RET_UNROLL)


def _retention(rope_out, plain_out, ret_norm, batch, seq):
    n_tok = batch * seq
    lg = jnp.log1p(-jnp.exp2(-5.0 - jnp.arange(H_RET, dtype=F32)))
    lg = jnp.broadcast_to(lg[:, None, None], (H_RET, 1, LANES))
    width = RET_HEADS_PER_STEP * LANES
    n_groups = H_RET // RET_HEADS_PER_STEP
    heads = lambda off: pl.BlockSpec((seq, width), lambda b, h: (b, off + h))
    return pl.pallas_call(
        _retention_kernel,
        grid=(batch, n_groups),
        in_specs=[pl.BlockSpec((RET_HEADS_PER_STEP, 1, LANES), lambda b, h: (h, 0, 0)),
                  heads(0), heads(n_groups),
                  heads(0), heads(n_groups),
                  pl.BlockSpec((1, width), lambda b, h: (0, h))],
        out_specs=heads(0),
        out_shape=jax.ShapeDtypeStruct((n_tok, RET_W), BF16),
        compiler_params=_params(("parallel", "parallel")),
        name="retention",
    )(lg, rope_out, rope_out, plain_out, plain_out, ret_norm)


KEY_TILE = 512
SCORE_ROWS = 128
FLT_MAX_KEY = 0x00800000
KEY_BITS = 32
FLT_MIN = 1.1754943508222875e-38
ATT_HEADS_PER_CHAIN = 2
ATT_LOOKAHEAD = 2


def _order_key_to_float(u):
    s = u ^ INT_MIN
    return pltpu.bitcast(s ^ ((s >> 31) & jnp.int32(0x7FFFFFFF)), F32)


def _attn_kernel(iq_ref, ik_ref, iw_ref, aq_ref, ak_ref, av_ref, *rest, top_k, n_cast):
    o_ref, sc_ref, sc16_ref = rest[n_cast], rest[-2], rest[-1]
    for w32_ref, w16_ref in zip(rest[:n_cast], rest[n_cast + 1:-2]):
        w16_ref[...] = w32_ref[...].astype(BF16)

    qb = pl.program_id(1)
    t0 = qb * Q_BLOCK
    n_tiles = (t0 + Q_BLOCK + KEY_TILE - 1) // KEY_TILE

    w_t = iw_ref[...].astype(F32).T
    w_rows = [w_t[h:h + 1, :] for h in range(H_IDX)]

    lane = lax.broadcasted_iota(jnp.int32, (Q_BLOCK, LANES), 1)
    low = lane < D_IDX
    q_pairs = []
    for c in range(IDX_Q_COLS // LANES):
        qc = iq_ref[:, c * LANES:(c + 1) * LANES]
        zero = jnp.zeros_like(qc)
        q_pairs.append(jnp.concatenate([jnp.where(low, qc, zero), jnp.where(low, zero, qc)], axis=0))

    q_pos = t0 + lax.broadcasted_iota(jnp.int32, (SCORE_ROWS, Q_BLOCK), 1)
    key_off = lax.broadcasted_iota(jnp.int32, (SCORE_ROWS, Q_BLOCK), 0)

    def score_tile(kt, carry):
        for s in range(KEY_TILE // SCORE_ROWS):
            base = pl.multiple_of(kt * KEY_TILE + s * SCORE_ROWS, SCORE_ROWS)
            kk = ik_ref[pl.ds(base, SCORE_ROWS), :]
            acc = jnp.zeros((SCORE_ROWS, Q_BLOCK), F32)
            for c, qp in enumerate(q_pairs):
                z = _dot_nt(kk, qp)
                acc = acc + jnp.maximum(z[:, :Q_BLOCK], 0.0) * w_rows[2 * c]
                acc = acc + jnp.maximum(z[:, Q_BLOCK:], 0.0) * w_rows[2 * c + 1]
            causal = base + key_off <= q_pos
            score = jnp.where(causal, acc, -jnp.inf)
            sc_ref[pl.ds(base, SCORE_ROWS), :] = score
            sc16_ref[pl.ds(base, SCORE_ROWS), :] = score.astype(BF16)
        return carry

    lax.fori_loop(0, n_tiles, score_tile, 0)

    def count_tiles(ref, indicator):
        rows = SUBLANES * 4 // ref.dtype.itemsize

        def count_tile(kt, cnt):
            base = pl.multiple_of(kt * KEY_TILE, KEY_TILE)
            part = indicator(ref[pl.ds(base, KEY_TILE), :], base)
            part = part.reshape(KEY_TILE // rows, rows, Q_BLOCK)
            while part.shape[0] > 1:
                half = part.shape[0] // 2
                part = part[:half] + part[half:]
            return cnt + part[0].astype(F32)

        cnt = lax.fori_loop(0, n_tiles, count_tile, jnp.zeros((rows, Q_BLOCK), F32))
        return jnp.sum(cnt, axis=0, keepdims=True)

    def count_ge(ref, cand):
        one, zero = jnp.ones((), ref.dtype), jnp.zeros((), ref.dtype)
        return count_tiles(ref, lambda tile, base: jnp.where(tile >= cand, one, zero))

    def search_bits(ref, tau_u, n_at_tau, first_bit, n_bits):
        def search_bit(it, carry):
            tau_u, n_at_tau = carry
            cand_u = tau_u | lax.shift_left(jnp.int32(1), first_bit - it)
            n_cand = count_ge(ref, _order_key_to_float(cand_u).astype(ref.dtype))
            accept = n_cand >= top_k
            return jnp.where(accept, cand_u, tau_u), jnp.where(accept, n_cand, n_at_tau)
        return lax.fori_loop(0, n_bits, search_bit, (tau_u, n_at_tau))

    searched = t0 + Q_BLOCK > top_k
    half_bits = jnp.where(searched, KEY_BITS // 2, 0)
    unknown = float(sc_ref.shape[0] + 1)
    no_count = jnp.full((1, Q_BLOCK), unknown, F32)
    rounded, _ = search_bits(sc16_ref, jnp.zeros((1, Q_BLOCK), jnp.int32), no_count,
                             KEY_BITS - 1, half_bits)
    n_rounded = count_ge(sc_ref, _order_key_to_float(rounded))
    fits = n_rounded >= top_k
    prefix = jnp.where(fits, rounded, rounded - (1 << KEY_BITS // 2))
    tau_u, n_ge = search_bits(sc_ref, prefix, jnp.where(fits, n_rounded, unknown),
                              KEY_BITS // 2 - 1, half_bits)
    tau_u = jnp.where(searched, tau_u, FLT_MAX_KEY)
    tau = _order_key_to_float(tau_u)

    surplus = jnp.max(n_ge) > top_k

    @pl.when(jnp.logical_and(searched, surplus))
    def _():
        key_row = lax.broadcasted_iota(jnp.int32, (KEY_TILE, Q_BLOCK), 0)
        n_ge = count_ge(sc_ref, tau)

        def tied_before(cutoff):
            return lambda tile, base: jnp.where(
                tile == tau, jnp.where(base + key_row < cutoff, 1.0, 0.0), 0.0)

        n_eq = count_tiles(sc_ref, tied_before(jnp.int32(sc_ref.shape[0])))
        keep = top_k - (n_ge - n_eq)
        n_cut_bits = sc_ref.shape[0].bit_length()

        def cutoff_bit(it, cutoff):
            cand = cutoff | lax.shift_left(jnp.int32(1), n_cut_bits - 1 - it)
            return jnp.where(count_tiles(sc_ref, tied_before(cand)) <= keep, cand, cutoff)

        cutoff = lax.fori_loop(0, n_cut_bits, cutoff_bit, jnp.zeros((1, Q_BLOCK), jnp.int32))
        below = jnp.where(tau == 0.0, -FLT_MIN, _order_key_to_float(tau_u - 1))

        def demote(kt, carry):
            base = pl.multiple_of(kt * KEY_TILE, KEY_TILE)
            tile = sc_ref[pl.ds(base, KEY_TILE), :]
            demoted = jnp.where(base + key_row < cutoff, tile, below)
            sc_ref[pl.ds(base, KEY_TILE), :] = jnp.where(tile == tau, demoted, tile)
            return carry

        lax.fori_loop(0, n_tiles, demote, 0)

    n_chains = H_ATT // ATT_HEADS_PER_CHAIN
    width = ATT_HEADS_PER_CHAIN * Q_BLOCK
    q_chains = []
    for c in range(n_chains):
        heads = range(c * ATT_HEADS_PER_CHAIN, (c + 1) * ATT_HEADS_PER_CHAIN)
        q_chains.append(jnp.concatenate([aq_ref[:, h * D_HEAD:(h + 1) * D_HEAD] for h in heads], axis=0))

    def attend_tile(kt, carry):
        base = pl.multiple_of(kt * KEY_TILE, KEY_TILE)
        bias = jnp.where(sc_ref[pl.ds(base, KEY_TILE), :] >= tau, 0.0, NEG_BIG)
        bias = jnp.concatenate([bias] * ATT_HEADS_PER_CHAIN, axis=1)
        kv_of = lambda c: c * ATT_HEADS_PER_CHAIN // GROUP
        k_tiles = [ak_ref[pl.ds(base, KEY_TILE), g * D_HEAD:(g + 1) * D_HEAD] for g in range(H_KV)]
        v_tiles = [av_ref[pl.ds(base, KEY_TILE), g * D_HEAD:(g + 1) * D_HEAD] for g in range(H_KV)]
        score = lambda c: _dot_nt(k_tiles[kv_of(c)], q_chains[c]) + bias
        logits = [score(c) for c in range(min(ATT_LOOKAHEAD, n_chains))]
        new = []
        for c in range(n_chains):
            m_old, l_old, acc_old = carry[c]
            m_new = jnp.maximum(m_old, jnp.max(logits[c], axis=0, keepdims=True))
            alpha = jnp.exp2(m_old - m_new)
            p = jnp.exp2(logits[c] - m_new)
            l_new = alpha * l_old + jnp.sum(p, axis=0, keepdims=True)
            if c + ATT_LOOKAHEAD < n_chains:
                logits.append(score(c + ATT_LOOKAHEAD))
            acc_new = alpha * acc_old + _dot_tn(v_tiles[kv_of(c)], p.astype(BF16))
            new.append((m_new, l_new, acc_new))
        return tuple(new)

    init = tuple((jnp.full((1, width), NEG_BIG, F32), jnp.zeros((1, width), F32),
                  jnp.zeros((D_HEAD, width), F32)) for _ in range(n_chains))
    final = lax.fori_loop(0, n_tiles, attend_tile, init)
    for c in range(n_chains):
        _, l_fin, acc_fin = final[c]
        out_t = acc_fin / l_fin
        for j in range(ATT_HEADS_PER_CHAIN):
            hcol = (c * ATT_HEADS_PER_CHAIN + j) * D_HEAD
            o_ref[:, hcol:hcol + D_HEAD] = out_t[:, j * Q_BLOCK:(j + 1) * Q_BLOCK].T.astype(o_ref.dtype)


def _sparse_attention(idx_out, rope_out, plain_out, batch, seq, weights_to_cast):
    n_tok = batch * seq
    nb = seq // Q_BLOCK
    n_steps = batch * nb
    top_k = min(MAX_TOPK, seq // 4)
    kv_w = H_KV * D_HEAD
    step = lambda b, q: (b * nb + q, 0)
    slabs = [pl.BlockSpec((w.shape[0] // n_steps, w.shape[1]), step) for w in weights_to_cast]
    outs = pl.pallas_call(
        functools.partial(_attn_kernel, top_k=top_k, n_cast=len(weights_to_cast)),
        grid=(batch, nb),
        in_specs=[
            pl.BlockSpec((Q_BLOCK, IDX_Q_COLS), step),
            pl.BlockSpec((seq, LANES), lambda b, q: (b, IDX_Q_COLS // LANES)),
            pl.BlockSpec((Q_BLOCK, LANES), lambda b, q: (b * nb + q, IDX_Q_COLS // LANES + 1)),
            pl.BlockSpec((Q_BLOCK, ATT_W), lambda b, q: (b * nb + q, 2 * RET_W // ATT_W)),
            pl.BlockSpec((seq, kv_w), lambda b, q: (b, (2 * RET_W + ATT_W) // kv_w)),
            pl.BlockSpec((seq, kv_w), lambda b, q: (b, 2 * RET_W // kv_w)),
        ] + slabs,
        out_specs=[pl.BlockSpec((Q_BLOCK, ATT_W), step)] + slabs,
        out_shape=[jax.ShapeDtypeStruct((n_tok, ATT_W), BF16)]
                  + [jax.ShapeDtypeStruct(w.shape, BF16) for w in weights_to_cast],
        scratch_shapes=[pltpu.VMEM((seq, Q_BLOCK), F32), pltpu.VMEM((seq, Q_BLOCK), BF16)],
        compiler_params=_params(("parallel", "arbitrary")),
        name="sparse_attention",
    )(idx_out, idx_out, idx_out, rope_out, rope_out, plain_out, *weights_to_cast)
    return outs[0], outs[1:]


def _out_proj_kernel(x_ref, ro_ref, ao_ref, wr_ref, wa_ref, o_ref):
    o_ref[...] = x_ref[...] + _dot(ro_ref[...], wr_ref[...]) + _dot(ao_ref[...], wa_ref[...])


def _out_proj(x2d, ro, ao, w_out, tm=512):
    n_tok = x2d.shape[0]
    tile = pl.BlockSpec((tm, D_MODEL), lambda i: (i, 0))
    half = pl.BlockSpec((tm, RET_W), lambda i: (i, 0))
    w_ret = pl.BlockSpec((RET_W, D_MODEL), lambda i: (0, 0))
    w_att = pl.BlockSpec((ATT_W, D_MODEL), lambda i: (RET_W // ATT_W, 0))
    return pl.pallas_call(
        _out_proj_kernel,
        grid=(n_tok // tm,),
        in_specs=[tile, half, half, w_ret, w_att],
        out_specs=tile,
        out_shape=jax.ShapeDtypeStruct((n_tok, D_MODEL), F32),
        compiler_params=_params(("parallel",)),
        name="out_proj",
    )(x2d, ro, ao, w_out, w_out)


def _layer(x2d, tables, batch, seq, ffn1_norm, ffn1_w_gate, ffn1_w_up, ffn1_w_down, mix_norm,
           w_in_groups, ret_norm, w_out, ffn2_norm, ffn2_w_gate, ffn2_w_up, ffn2_w_down, final_norm, last):
    cos_a, sin_a, cos_b, sin_b = tables
    w_rope, w_plain, w_idx = w_in_groups
    row = lambda g: g.reshape(1, -1).astype(F32)

    x1, h = _ffn(x2d, row(ffn1_norm), ffn1_w_gate, ffn1_w_up, ffn1_w_down,
                 row(mix_norm), emit_residual=True, tf=FFN_TF_F32)
    rope_out = _proj(h, w_rope, cos_a, sin_a, mode="rope128", tm=512, out_dtype=BF16)
    plain_out = _proj(h, w_plain, cos_a, sin_a, mode="plain", tm=512, out_dtype=BF16)
    idx_out = _proj(h, w_idx, cos_b, sin_b, mode="idx", tm=1024, out_dtype=BF16)
    ro = _retention(rope_out, plain_out, row(ret_norm), batch, seq)
    ao, (w_out16, wg16, wu16, wd16) = _sparse_attention(
        idx_out, rope_out, plain_out, batch, seq, [w_out, ffn2_w_gate, ffn2_w_up, ffn2_w_down])
    x2 = _out_proj(x1, ro, ao, w_out16)
    out = _ffn(x2, row(ffn2_norm), wg16, wu16, wd16, row(final_norm), emit_residual=not last,
               tf=FFN_TF_BF16)
    return out if last else out[0]


def kernel(x, positions, ffn1_norm, ffn1_w_gate, ffn1_w_up, ffn1_w_down, mix_norm, w_in, ret_norm,
           w_out, ffn2_norm, ffn2_w_gate, ffn2_w_up, ffn2_w_down, final_norm):
    batch, seq, _ = x.shape
    depth = w_in.shape[0]
    tables, w_in_groups = _prepare(positions, jnp.swapaxes(w_in, 1, 2))
    x2d = x.reshape(batch * seq, D_MODEL)
    for l in range(depth):
        last = l == depth - 1
        groups_l = [w[l] for w in w_in_groups]
        x2d = _layer(x2d, tables, batch, seq, ffn1_norm[l], ffn1_w_gate[l], ffn1_w_up[l], ffn1_w_down[l],
                     mix_norm[l], groups_l, ret_norm[l], w_out[l], ffn2_norm[l], ffn2_w_gate[l],
                     ffn2_w_up[l], ffn2_w_down[l], final_norm, last)
    return x2d.reshape(batch, seq, D_MODEL)
```

```python
import functools

import jax
import jax.numpy as jnp
from jax import lax
from jax.experimental import pallas as pl
from jax.experimental.pallas import tpu as pltpu

D_MODEL = 2048
H_RET = 8
DK_RET = 128
DV_RET = 128
RET_CHUNK = 256
H_ATT = 8
H_KV = 2
D_HEAD = 128
H_IDX = 16
D_IDX = 64
MAX_TOPK = 256
Q_BLOCK = 256
D_FF = 5632
ROPE_THETA = 10000.0
NORM_EPS = 1e-6

RET_W = H_RET * DV_RET
ATT_W = H_ATT * D_HEAD
GROUP = H_ATT // H_KV

LANES = 128
SUBLANES = 8
VMEM_LIMIT = 60 * 1024 * 1024

ROPE_COLS = 2 * H_RET * DK_RET + ATT_W + H_KV * D_HEAD
PLAIN_COLS = 2 * RET_W + H_KV * D_HEAD
IDX_Q_COLS = H_IDX * D_IDX
IDX_COLS = IDX_Q_COLS + 2 * LANES
AQ_CHUNK0 = 2 * H_RET * DK_RET // LANES

INT_MIN = -2 ** 31
NEG_BIG = -1e30
LOG2_E = 1.4426950408889634

F32 = jnp.float32
BF16 = jnp.bfloat16


def _dot(a, b):
    return jnp.dot(a, b, preferred_element_type=F32)


def _dot_nt(a, b):
    return lax.dot_general(a, b, (((1,), (1,)), ((), ())), preferred_element_type=F32)


def _dot_tn(a, b):
    return lax.dot_general(a, b, (((0,), (0,)), ((), ())), preferred_element_type=F32)


def _rmsnorm(xf, g):
    ms = jnp.mean(xf * xf, axis=-1, keepdims=True)
    return xf * lax.rsqrt(ms + NORM_EPS) * g


def _params(sem):
    return pltpu.CompilerParams(dimension_semantics=sem, vmem_limit_bytes=VMEM_LIMIT)


_IN_SIZES = (H_RET * DK_RET, H_RET * DK_RET, RET_W, RET_W, ATT_W, H_KV * D_HEAD, H_KV * D_HEAD,
             H_IDX * D_IDX, D_IDX, H_IDX)
W_IN_SPLITS = tuple(sum(_IN_SIZES[:n]) for n in range(len(_IN_SIZES) + 1))


def _prep_kernel(pos_ref, inv_ref, sgn_a_ref, sgn_b_ref, w_in_ref,
                 cos_a_ref, sin_a_ref, cos_b_ref, sin_b_ref, w_rope_ref, w_plain_ref, w_idx_ref):
    ang = pos_ref[...].astype(F32) * inv_ref[...]
    lane = lax.broadcasted_iota(jnp.int32, ang.shape, 1)
    half, quarter = D_HEAD // 2, D_IDX // 2

    def table_a(t):
        return jnp.where(lane < half, t, pltpu.roll(t, half, axis=1))

    def table_b(t):
        upper = jnp.where(lane < half + quarter, t, pltpu.roll(t, quarter, axis=1))
        return jnp.where(lane >= half, upper, pltpu.roll(upper, half, axis=1))

    cos, sin = jnp.cos(ang), jnp.sin(ang)
    cos_a_ref[...] = table_a(cos)
    sin_a_ref[...] = table_a(sin) * sgn_a_ref[...]
    cos_b_ref[...] = table_b(cos)
    sin_b_ref[...] = table_b(sin) * sgn_b_ref[...]

    rq, rk, rv, rg, aq, ak, av, iq, ik, iw, end = W_IN_SPLITS
    for layer in range(w_in_ref.shape[0]):
        feats = lambda lo, hi: w_in_ref[layer, lo:hi, :].astype(BF16)
        w_rope_ref[layer, :rv - rq, :] = feats(rq, rv)
        w_rope_ref[layer, rv - rq:, :] = feats(aq, av)
        w_plain_ref[layer, :aq - rv, :] = feats(rv, aq)
        w_plain_ref[layer, aq - rv:, :] = feats(av, iq)
        w_idx_ref[layer, :ik - iq, :] = feats(iq, ik)
        key = feats(ik, iw)
        w_idx_ref[layer, ik - iq:ik - iq + D_IDX, :] = key
        w_idx_ref[layer, ik - iq + D_IDX:ik - iq + LANES, :] = key
        w_idx_ref[layer, ik - iq + LANES:ik - iq + LANES + H_IDX, :] = feats(iw, end)
        w_idx_ref[layer, ik - iq + LANES + H_IDX:, :] = jnp.zeros(
            (LANES - H_IDX, w_idx_ref.shape[2]), BF16)


def _prepare(positions, w_in_t):
    n_tok = positions.size
    tm = 1024
    n_steps = n_tok // tm
    depth, _, d_model = w_in_t.shape
    w_cols = d_model // n_steps
    lane = jnp.arange(LANES)

    def inv_freq(d):
        return ROPE_THETA ** (-jnp.arange(0, d, 2, dtype=F32) / d)

    unused = jnp.zeros((LANES - D_HEAD // 2 - D_IDX // 2,), F32)
    inv = jnp.concatenate([inv_freq(D_HEAD), inv_freq(D_IDX), unused])[None, :]

    def sign(d):
        return jnp.where(lane % d < d // 2, -1.0, 1.0).astype(F32)[None, :]

    row = pl.BlockSpec((1, LANES), lambda i: (0, 0))
    tab = pl.BlockSpec((tm, LANES), lambda i: (i, 0))
    out = jax.ShapeDtypeStruct((n_tok, LANES), F32)
    slab = lambda n_feats: pl.BlockSpec((depth, n_feats, w_cols), lambda i: (0, 0, i))
    w_out = lambda n_feats: jax.ShapeDtypeStruct((depth, n_feats, d_model), BF16)
    outs = pl.pallas_call(
        _prep_kernel,
        grid=(n_steps,),
        in_specs=[pl.BlockSpec((tm, 1), lambda i: (i, 0)), row, row, row, slab(w_in_t.shape[1])],
        out_specs=[tab, tab, tab, tab, slab(ROPE_COLS), slab(PLAIN_COLS), slab(IDX_COLS)],
        out_shape=[out, out, out, out, w_out(ROPE_COLS), w_out(PLAIN_COLS), w_out(IDX_COLS)],
        compiler_params=_params(("parallel",)),
        name="prepare",
    )(positions.reshape(n_tok, 1), inv, sign(D_HEAD), sign(D_IDX), w_in_t)
    return outs[:4], outs[4:]


FFN_ROW_CHUNK = 128


def _ffn_kernel(x_ref, g_ref, wg_ref, wu_ref, wd_ref, g2_ref, *refs, emit_residual):
    if emit_residual:
        res_ref, normed_ref, xn_ref = refs
        acc_ref = res_ref
    else:
        normed_ref, xn_ref = refs
        acc_ref = normed_ref
    f = pl.program_id(1)
    n_row_chunks = x_ref.shape[0] // FFN_ROW_CHUNK

    def rows_of(r):
        return pl.ds(pl.multiple_of(r * FFN_ROW_CHUNK, FFN_ROW_CHUNK), FFN_ROW_CHUNK)

    def down_projected():
        xn = xn_ref[...]
        a = _dot(xn, wg_ref[...].astype(BF16))
        b = _dot(xn, wu_ref[...].astype(BF16))
        hidden = (a * jax.nn.sigmoid(a) * b).astype(BF16)
        return _dot(hidden, wd_ref[...].astype(BF16))

    @pl.when(f == 0)
    def _():
        def prologue(r, carry):
            xn_ref[rows_of(r), :] = _rmsnorm(x_ref[rows_of(r), :], g_ref[...]).astype(BF16)
            return carry
        lax.fori_loop(0, n_row_chunks, prologue, 0)
        acc_ref[...] = down_projected()

    @pl.when(f > 0)
    def _():
        acc_ref[...] += down_projected()

    @pl.when(f == pl.num_programs(1) - 1)
    def _():
        def epilogue(r, carry):
            y = x_ref[rows_of(r), :] + 0.5 * acc_ref[rows_of(r), :]
            if emit_residual:
                res_ref[rows_of(r), :] = y
            normed_ref[rows_of(r), :] = _rmsnorm(y, g2_ref[...]).astype(normed_ref.dtype)
            return carry
        lax.fori_loop(0, n_row_chunks, epilogue, 0)


FFN_TM = 1024
FFN_TF_F32 = 256
FFN_TF_BF16 = 512


def _ffn(x2d, g, wg, wu, wd, g2, *, emit_residual, tf, tm=FFN_TM):
    n_tok = x2d.shape[0]
    tile_map = lambda i, f: (i, 0)
    x_tile = pl.BlockSpec((tm, D_MODEL), tile_map)
    out_tile = pl.BlockSpec((tm, D_MODEL), tile_map)
    normed_tile = out_tile
    row = pl.BlockSpec((1, D_MODEL), lambda i, f: (0, 0))
    if emit_residual:
        out_specs = [out_tile, normed_tile]
        out_shape = [jax.ShapeDtypeStruct((n_tok, D_MODEL), F32),
                     jax.ShapeDtypeStruct((n_tok, D_MODEL), BF16)]
    else:
        out_specs = out_tile
        out_shape = jax.ShapeDtypeStruct((n_tok, D_MODEL), F32)
    return pl.pallas_call(
        functools.partial(_ffn_kernel, emit_residual=emit_residual),
        grid=(n_tok // tm, D_FF // tf),
        in_specs=[x_tile, row,
                  pl.BlockSpec((D_MODEL, tf), lambda i, f: (0, f)),
                  pl.BlockSpec((D_MODEL, tf), lambda i, f: (0, f)),
                  pl.BlockSpec((tf, D_MODEL), lambda i, f: (f, 0)),
                  row],
        out_specs=out_specs,
        out_shape=out_shape,
        scratch_shapes=[pltpu.VMEM((tm, D_MODEL), BF16)],
        compiler_params=_params(("parallel", "arbitrary")),
        name="ffn_residual" if emit_residual else "ffn_final",
    )(x2d, g, wg, wu, wd, g2)


def _rope128(y, cos, sin):
    return y * cos + pltpu.roll(y, D_HEAD // 2, axis=1) * sin


def _rope64(y, cos, sin):
    lane = lax.broadcasted_iota(jnp.int32, y.shape, 1)
    first_half = (lane & (D_IDX - 1)) < D_IDX // 2
    partner = jnp.where(first_half,
                        pltpu.roll(y, LANES - D_IDX // 2, axis=1),
                        pltpu.roll(y, D_IDX // 2, axis=1))
    return y * cos + partner * sin


def _proj_kernel(h_ref, w_ref, cos_ref, sin_ref, o_ref, *, mode):
    y = _dot_nt(h_ref[...], w_ref[...])
    n_chunks = y.shape[1] // LANES
    if mode == "plain":
        o_ref[...] = y.astype(o_ref.dtype)
        return
    cos = cos_ref[...]
    sin = sin_ref[...]
    for c in range(n_chunks):
        yc = y[:, c * LANES:(c + 1) * LANES]
        if mode == "rope128":
            yc = _rope128(yc, cos, sin)
            if AQ_CHUNK0 <= c < AQ_CHUNK0 + H_ATT:
                yc = yc * (LOG2_E * D_HEAD ** -0.5)
        elif c < IDX_Q_COLS // LANES:
            yc = _rope64(yc, cos, sin) * (D_IDX ** -0.5)
        elif c == IDX_Q_COLS // LANES:
            yc = _rope64(yc, cos, sin)
        else:
            yc = yc * (H_IDX ** -0.5)
        o_ref[:, c * LANES:(c + 1) * LANES] = yc.astype(o_ref.dtype)


def _proj(h, w, cos, sin, *, mode, tm, out_dtype):
    n_tok, n_cols = h.shape[0], w.shape[0]
    tab = pl.BlockSpec((tm, LANES), lambda i: (i, 0))
    return pl.pallas_call(
        functools.partial(_proj_kernel, mode=mode),
        grid=(n_tok // tm,),
        in_specs=[pl.BlockSpec((tm, D_MODEL), lambda i: (i, 0)),
                  pl.BlockSpec((n_cols, D_MODEL), lambda i: (0, 0), pipeline_mode=pl.Buffered(1)),
                  tab, tab],
        out_specs=pl.BlockSpec((tm, n_cols), lambda i: (i, 0)),
        out_shape=jax.ShapeDtypeStruct((n_tok, n_cols), out_dtype),
        compiler_params=_params(("parallel",)),
        name="proj_" + mode,
    )(h, w, cos, sin)


RET_UNROLL = 8
RET_HEADS_PER_STEP = 2


def _retention_kernel(lg_ref, q_ref, k_ref, v_ref, g_ref, rn_ref, o_ref):
    C = RET_CHUNK
    n_chunks = q_ref.shape[0] // C
    diff = (lax.broadcasted_iota(jnp.int32, (C, C), 0)
            - lax.broadcasted_iota(jnp.int32, (C, C), 1)).astype(F32)
    row = lax.broadcasted_iota(jnp.int32, (C, LANES), 0).astype(F32)
    scale = DK_RET ** -0.5
    heads = []
    for j in range(RET_HEADS_PER_STEP):
        lg = lg_ref[j]
        lg_wide = jnp.concatenate([lg] * (C // LANES), axis=1)
        heads.append(dict(
            lanes=slice(j * LANES, (j + 1) * LANES),
            decay=jnp.where(diff >= 0, jnp.exp(jnp.maximum(diff, 0.0) * lg_wide), 0.0) * scale,
            k_dec=jnp.exp((C - 1 - row) * lg) * scale,
            q_dec=jnp.exp((row + 1) * lg),
            g_chunk=jnp.exp(C * lg)))

    def body(n, states):
        sl = pl.ds(pl.multiple_of(n * C, C), C)
        new_states = []
        for hd, state in zip(heads, states):
            qc = q_ref[sl, hd["lanes"]]
            kc = k_ref[sl, hd["lanes"]]
            vc = v_ref[sl, hd["lanes"]]
            s = _dot_nt(qc, kc) * hd["decay"]
            intra = _dot(s.astype(BF16), vc)
            cross = _dot(qc, state.astype(BF16)) * hd["q_dec"]
            kv = _dot_tn((kc.astype(F32) * hd["k_dec"]).astype(BF16), vc)
            o = intra + cross
            mu = jnp.mean(o, axis=-1, keepdims=True)
            d = o - mu
            var = jnp.mean(d * d, axis=-1, keepdims=True)
            y = d * lax.rsqrt(var + NORM_EPS) * rn_ref[:, hd["lanes"]]
            gate = g_ref[sl, hd["lanes"]].astype(F32)
            o_ref[sl, hd["lanes"]] = (y * (gate * jax.nn.sigmoid(gate))).astype(o_ref.dtype)
            new_states.append(state * hd["g_chunk"] + kv)
        return tuple(new_states)

    init = tuple(jnp.zeros((DK_RET, DV_RET), F32) for _ in heads)
    lax.fori_loop(0, n_chunks, body, init, unroll=RET_UNROLL)


def _retention(rope_out, plain_out, ret_norm, batch, seq):
    n_tok = batch * seq
    lg = jnp.log1p(-jnp.exp2(-5.0 - jnp.arange(H_RET, dtype=F32)))
    lg = jnp.broadcast_to(lg[:, None, None], (H_RET, 1, LANES))
    width = RET_HEADS_PER_STEP * LANES
    n_groups = H_RET // RET_HEADS_PER_STEP
    heads = lambda off: pl.BlockSpec((seq, width), lambda b, h: (b, off + h))
    return pl.pallas_call(
        _retention_kernel,
        grid=(batch, n_groups),
        in_specs=[pl.BlockSpec((RET_HEADS_PER_STEP, 1, LANES), lambda b, h: (h, 0, 0)),
                  heads(0), heads(n_groups),
                  heads(0), heads(n_groups),
                  pl.BlockSpec((1, width), lambda b, h: (0, h))],
        out_specs=heads(0),
        out_shape=jax.ShapeDtypeStruct((n_tok, RET_W), BF16),
        compiler_params=_params(("parallel", "parallel")),
        name="retention",
    )(lg, rope_out, rope_out, plain_out, plain_out, ret_norm)


KEY_TILE = 512
SCORE_ROWS = 128
FLT_MAX_KEY = 0x00800000
KEY_BITS = 32
FLT_MIN = 1.1754943508222875e-38
ATT_HEADS_PER_CHAIN = 2
ATT_LOOKAHEAD = 2


def _order_key_to_float(u):
    s = u ^ INT_MIN
    return pltpu.bitcast(s ^ ((s >> 31) & jnp.int32(0x7FFFFFFF)), F32)


def _attn_kernel(iq_ref, ik_ref, iw_ref, aq_ref, ak_ref, av_ref, *rest, top_k, n_cast):
    o_ref, sc_ref, sc16_ref = rest[n_cast], rest[-2], rest[-1]
    for w32_ref, w16_ref in zip(rest[:n_cast], rest[n_cast + 1:-2]):
        w16_ref[...] = w32_ref[...].astype(BF16)

    qb = pl.program_id(1)
    t0 = qb * Q_BLOCK
    n_tiles = (t0 + Q_BLOCK + KEY_TILE - 1) // KEY_TILE

    w_t = iw_ref[...].astype(F32).T
    w_rows = [w_t[h:h + 1, :] for h in range(H_IDX)]

    lane = lax.broadcasted_iota(jnp.int32, (Q_BLOCK, LANES), 1)
    low = lane < D_IDX
    q_pairs = []
    for c in range(IDX_Q_COLS // LANES):
        qc = iq_ref[:, c * LANES:(c + 1) * LANES]
        zero = jnp.zeros_like(qc)
        q_pairs.append(jnp.concatenate([jnp.where(low, qc, zero), jnp.where(low, zero, qc)], axis=0))

    q_pos = t0 + lax.broadcasted_iota(jnp.int32, (SCORE_ROWS, Q_BLOCK), 1)
    key_off = lax.broadcasted_iota(jnp.int32, (SCORE_ROWS, Q_BLOCK), 0)

    def score_tile(kt, carry):
        for s in range(KEY_TILE // SCORE_ROWS):
            base = pl.multiple_of(kt * KEY_TILE + s * SCORE_ROWS, SCORE_ROWS)
            kk = ik_ref[pl.ds(base, SCORE_ROWS), :]
            acc = jnp.zeros((SCORE_ROWS, Q_BLOCK), F32)
            for c, qp in enumerate(q_pairs):
                z = _dot_nt(kk, qp)
                acc = acc + jnp.maximum(z[:, :Q_BLOCK], 0.0) * w_rows[2 * c]
                acc = acc + jnp.maximum(z[:, Q_BLOCK:], 0.0) * w_rows[2 * c + 1]
            causal = base + key_off <= q_pos
            score = jnp.where(causal, acc, -jnp.inf)
            sc_ref[pl.ds(base, SCORE_ROWS), :] = score
            sc16_ref[pl.ds(base, SCORE_ROWS), :] = score.astype(BF16)
        return carry

    lax.fori_loop(0, n_tiles, score_tile, 0)

    def count_tiles(ref, indicator):
        rows = SUBLANES * 4 // ref.dtype.itemsize

        def count_tile(kt, cnt):
            base = pl.multiple_of(kt * KEY_TILE, KEY_TILE)
            part = indicator(ref[pl.ds(base, KEY_TILE), :], base)
            part = part.reshape(KEY_TILE // rows, rows, Q_BLOCK)
            while part.shape[0] > 1:
                half = part.shape[0] // 2
                part = part[:half] + part[half:]
            return cnt + part[0].astype(F32)

        cnt = lax.fori_loop(0, n_tiles, count_tile, jnp.zeros((rows, Q_BLOCK), F32))
        return jnp.sum(cnt, axis=0, keepdims=True)

    def count_ge(ref, cand):
        one, zero = jnp.ones((), ref.dtype), jnp.zeros((), ref.dtype)
        return count_tiles(ref, lambda tile, base: jnp.where(tile >= cand, one, zero))

    def search_bits(ref, tau_u, n_at_tau, first_bit, n_bits):
        def search_bit(it, carry):
            tau_u, n_at_tau = carry
            cand_u = tau_u | lax.shift_left(jnp.int32(1), first_bit - it)
            n_cand = count_ge(ref, _order_key_to_float(cand_u).astype(ref.dtype))
            accept = n_cand >= top_k
            return jnp.where(accept, cand_u, tau_u), jnp.where(accept, n_cand, n_at_tau)
        return lax.fori_loop(0, n_bits, search_bit, (tau_u, n_at_tau))

    searched = t0 + Q_BLOCK > top_k
    half_bits = jnp.where(searched, KEY_BITS // 2, 0)
    unknown = float(sc_ref.shape[0] + 1)
    no_count = jnp.full((1, Q_BLOCK), unknown, F32)
    rounded, _ = search_bits(sc16_ref, jnp.zeros((1, Q_BLOCK), jnp.int32), no_count,
                             KEY_BITS - 1, half_bits)
    n_rounded = count_ge(sc_ref, _order_key_to_float(rounded))
    fits = n_rounded >= top_k
    prefix = jnp.where(fits, rounded, rounded - (1 << KEY_BITS // 2))
    tau_u, n_ge = search_bits(sc_ref, prefix, jnp.where(fits, n_rounded, unknown),
                              KEY_BITS // 2 - 1, half_bits)
    tau_u = jnp.where(searched, tau_u, FLT_MAX_KEY)
    tau = _order_key_to_float(tau_u)

    surplus = jnp.max(n_ge) > top_k

    @pl.when(jnp.logical_and(searched, surplus))
    def _():
        key_row = lax.broadcasted_iota(jnp.int32, (KEY_TILE, Q_BLOCK), 0)
        n_ge = count_ge(sc_ref, tau)

        def tied_before(cutoff):
            return lambda tile, base: jnp.where(
                tile == tau, jnp.where(base + key_row < cutoff, 1.0, 0.0), 0.0)

        n_eq = count_tiles(sc_ref, tied_before(jnp.int32(sc_ref.shape[0])))
        keep = top_k - (n_ge - n_eq)
        n_cut_bits = sc_ref.shape[0].bit_length()

        def cutoff_bit(it, cutoff):
            cand = cutoff | lax.shift_left(jnp.int32(1), n_cut_bits - 1 - it)
            return jnp.where(count_tiles(sc_ref, tied_before(cand)) <= keep, cand, cutoff)

        cutoff = lax.fori_loop(0, n_cut_bits, cutoff_bit, jnp.zeros((1, Q_BLOCK), jnp.int32))
        below = jnp.where(tau == 0.0, -FLT_MIN, _order_key_to_float(tau_u - 1))

        def demote(kt, carry):
            base = pl.multiple_of(kt * KEY_TILE, KEY_TILE)
            tile = sc_ref[pl.ds(base, KEY_TILE), :]
            demoted = jnp.where(base + key_row < cutoff, tile, below)
            sc_ref[pl.ds(base, KEY_TILE), :] = jnp.where(tile == tau, demoted, tile)
            return carry

        lax.fori_loop(0, n_tiles, demote, 0)

    n_chains = H_ATT // ATT_HEADS_PER_CHAIN
    width = ATT_HEADS_PER_CHAIN * Q_BLOCK
    q_chains = []
    for c in range(n_chains):
        heads = range(c * ATT_HEADS_PER_CHAIN, (c + 1) * ATT_HEADS_PER_CHAIN)
        q_chains.append(jnp.concatenate([aq_ref[:, h * D_HEAD:(h + 1) * D_HEAD] for h in heads], axis=0))

    def attend_tile(kt, carry):
        base = pl.multiple_of(kt * KEY_TILE, KEY_TILE)
        bias = jnp.where(sc_ref[pl.ds(base, KEY_TILE), :] >= tau, 0.0, NEG_BIG)
        bias = jnp.concatenate([bias] * ATT_HEADS_PER_CHAIN, axis=1)
        kv_of = lambda c: c * ATT_HEADS_PER_CHAIN // GROUP
        k_tiles = [ak_ref[pl.ds(base, KEY_TILE), g * D_HEAD:(g + 1) * D_HEAD] for g in range(H_KV)]
        v_tiles = [av_ref[pl.ds(base, KEY_TILE), g * D_HEAD:(g + 1) * D_HEAD] for g in range(H_KV)]
        score = lambda c: _dot_nt(k_tiles[kv_of(c)], q_chains[c]) + bias
        logits = [score(c) for c in range(min(ATT_LOOKAHEAD, n_chains))]
        new = []
        for c in range(n_chains):
            m_old, l_old, acc_old = carry[c]
            m_new = jnp.maximum(m_old, jnp.max(logits[c], axis=0, keepdims=True))
            alpha = jnp.exp2(m_old - m_new)
            p = jnp.exp2(logits[c] - m_new)
            l_new = alpha * l_old + jnp.sum(p, axis=0, keepdims=True)
            if c + ATT_LOOKAHEAD < n_chains:
                logits.append(score(c + ATT_LOOKAHEAD))
            acc_new = alpha * acc_old + _dot_tn(v_tiles[kv_of(c)], p.astype(BF16))
            new.append((m_new, l_new, acc_new))
        return tuple(new)

    init = tuple((jnp.full((1, width), NEG_BIG, F32), jnp.zeros((1, width), F32),
                  jnp.zeros((D_HEAD, width), F32)) for _ in range(n_chains))
    final = lax.fori_loop(0, n_tiles, attend_tile, init)
    for c in range(n_chains):
        _, l_fin, acc_fin = final[c]
        out_t = acc_fin / l_fin
        for j in range(ATT_HEADS_PER_CHAIN):
            hcol = (c * ATT_HEADS_PER_CHAIN + j) * D_HEAD
            o_ref[:, hcol:hcol + D_HEAD] = out_t[:, j * Q_BLOCK:(j + 1) * Q_BLOCK].T.astype(o_ref.dtype)


def _sparse_attention(idx_out, rope_out, plain_out, batch, seq, weights_to_cast):
    n_tok = batch * seq
    nb = seq // Q_BLOCK
    n_steps = batch * nb
    top_k = min(MAX_TOPK, seq // 4)
    kv_w = H_KV * D_HEAD
    step = lambda b, q: (b * nb + q, 0)
    slabs = [pl.BlockSpec((w.shape[0] // n_steps, w.shape[1]), step) for w in weights_to_cast]
    outs = pl.pallas_call(
        functools.partial(_attn_kernel, top_k=top_k, n_cast=len(weights_to_cast)),
        grid=(batch, nb),
        in_specs=[
            pl.BlockSpec((Q_BLOCK, IDX_Q_COLS), step),
            pl.BlockSpec((seq, LANES), lambda b, q: (b, IDX_Q_COLS // LANES)),
            pl.BlockSpec((Q_BLOCK, LANES), lambda b, q: (b * nb + q, IDX_Q_COLS // LANES + 1)),
            pl.BlockSpec((Q_BLOCK, ATT_W), lambda b, q: (b * nb + q, 2 * RET_W // ATT_W)),
            pl.BlockSpec((seq, kv_w), lambda b, q: (b, (2 * RET_W + ATT_W) // kv_w)),
            pl.BlockSpec((seq, kv_w), lambda b, q: (b, 2 * RET_W // kv_w)),
        ] + slabs,
        out_specs=[pl.BlockSpec((Q_BLOCK, ATT_W), step)] + slabs,
        out_shape=[jax.ShapeDtypeStruct((n_tok, ATT_W), BF16)]
                  + [jax.ShapeDtypeStruct(w.shape, BF16) for w in weights_to_cast],
        scratch_shapes=[pltpu.VMEM((seq, Q_BLOCK), F32), pltpu.VMEM((seq, Q_BLOCK), BF16)],
        compiler_params=_params(("parallel", "arbitrary")),
        name="sparse_attention",
    )(idx_out, idx_out, idx_out, rope_out, rope_out, plain_out, *weights_to_cast)
    return outs[0], outs[1:]


def _out_proj_kernel(x_ref, ro_ref, ao_ref, wr_ref, wa_ref, o_ref):
    o_ref[...] = x_ref[...] + _dot(ro_ref[...], wr_ref[...]) + _dot(ao_ref[...], wa_ref[...])


def _out_proj(x2d, ro, ao, w_out, tm=512):
    n_tok = x2d.shape[0]
    tile = pl.BlockSpec((tm, D_MODEL), lambda i: (i, 0))
    half = pl.BlockSpec((tm, RET_W), lambda i: (i, 0))
    w_ret = pl.BlockSpec((RET_W, D_MODEL), lambda i: (0, 0))
    w_att = pl.BlockSpec((ATT_W, D_MODEL), lambda i: (RET_W // ATT_W, 0))
    return pl.pallas_call(
        _out_proj_kernel,
        grid=(n_tok // tm,),
        in_specs=[tile, half, half, w_ret, w_att],
        out_specs=tile,
        out_shape=jax.ShapeDtypeStruct((n_tok, D_MODEL), F32),
        compiler_params=_params(("parallel",)),
        name="out_proj",
    )(x2d, ro, ao, w_out, w_out)


def _layer(x2d, tables, batch, seq, ffn1_norm, ffn1_w_gate, ffn1_w_up, ffn1_w_down, mix_norm,
           w_in_groups, ret_norm, w_out, ffn2_norm, ffn2_w_gate, ffn2_w_up, ffn2_w_down, final_norm, last):
    cos_a, sin_a, cos_b, sin_b = tables
    w_rope, w_plain, w_idx = w_in_groups
    row = lambda g: g.reshape(1, -1).astype(F32)

    x1, h = _ffn(x2d, row(ffn1_norm), ffn1_w_gate, ffn1_w_up, ffn1_w_down,
                 row(mix_norm), emit_residual=True, tf=FFN_TF_F32)
    rope_out = _proj(h, w_rope, cos_a, sin_a, mode="rope128", tm=512, out_dtype=BF16)
    plain_out = _proj(h, w_plain, cos_a, sin_a, mode="plain", tm=512, out_dtype=BF16)
    idx_out = _proj(h, w_idx, cos_b, sin_b, mode="idx", tm=1024, out_dtype=BF16)
    ro = _retention(rope_out, plain_out, row(ret_norm), batch, seq)
    ao, (w_out16, wg16, wu16, wd16) = _sparse_attention(
        idx_out, rope_out, plain_out, batch, seq, [w_out, ffn2_w_gate, ffn2_w_up, ffn2_w_down])
    x2 = _out_proj(x1, ro, ao, w_out16)
    out = _ffn(x2, row(ffn2_norm), wg16, wu16, wd16, row(final_norm), emit_residual=not last,
               tf=FFN_TF_BF16)
    return out if last else out[0]


def kernel(x, positions, ffn1_norm, ffn1_w_gate, ffn1_w_up, ffn1_w_down, mix_norm, w_in, ret_norm,
           w_out, ffn2_norm, ffn2_w_gate, ffn2_w_up, ffn2_w_down, final_norm):
    batch, seq, _ = x.shape
    depth = w_in.shape[0]
    tables, w_in_groups = _prepare(positions, jnp.swapaxes(w_in, 1, 2))
    x2d = x.reshape(batch * seq, D_MODEL)
    for l in range(depth):
        last = l == depth - 1
        groups_l = [w[l] for w in w_in_groups]
        x2d = _layer(x2d, tables, batch, seq, ffn1_norm[l], ffn1_w_gate[l], ffn1_w_up[l], ffn1_w_down[l],
                     mix_norm[l], groups_l, ret_norm[l], w_out[l], ffn2_norm[l], ffn2_w_gate[l],
                     ffn2_w_up[l], ffn2_w_down[l], final_norm, last)
    return x2d.reshape(batch, seq, D_MODEL)
```
